```python
import math
import jax, jax.numpy as jnp
from jax import lax
import numpy as np

D_MODEL = 1024
BATCH = 16
SEQ = 2048
DEPTH = 1

HEAD_DIM = 128
HEADS_PER_GROUP = 4
DILATED_GROUPS = ((128, 1), (512, 4), (2048, 16))
N_GROUPS = 3
N_ATTN_HEADS = N_GROUPS * HEADS_PER_GROUP
QKV_WIDTH = N_ATTN_HEADS * HEAD_DIM
ATTN_WIDTH = HEADS_PER_GROUP * HEAD_DIM
CONV_WIDTH = D_MODEL
CONV_K = 3
N_BUCKETS = 32
MAX_EXACT = 16
MAX_DISTANCE = 2048
BLOCK = 128
DEEPNORM_ALPHA = (2.0 * DEPTH) ** 0.25
DEEPNORM_BETA = (8.0 * DEPTH) ** -0.25
LN_EPS = 1e-5
NEG_INF = -1e30
COL_WIDTHS = (QKV_WIDTH, QKV_WIDTH, QKV_WIDTH, ATTN_WIDTH,
              CONV_WIDTH, CONV_WIDTH, CONV_WIDTH, CONV_WIDTH,
              D_MODEL, D_MODEL)
IN_COLS = 4 * QKV_WIDTH // 4 * 3 // 3 * 1 * 3 + ATTN_WIDTH + 4 * CONV_WIDTH + 2 * D_MODEL

kernel_name = "hybrid_dilated_attn_shortconv_gated_merge"


def _split_cols(proj):
    offsets = []
    acc = 0
    for w in COL_WIDTHS[:-1]:
        acc += w
        offsets.append(acc)
    return jnp.split(proj, offsets, axis=-1)


def _t5_bucket(dist):
    n = jnp.maximum(dist, 1).astype(jnp.float32)
    large = MAX_EXACT + (jnp.log(n / MAX_EXACT) / math.log(MAX_DISTANCE / MAX_EXACT)
                         * (N_BUCKETS - MAX_EXACT)).astype(jnp.int32)
    large = jnp.minimum(large, N_BUCKETS - 1)
    return jnp.where(dist < MAX_EXACT, dist, large)


def _dilated_group_attention(q, k, v, bias_tab, window, dilation):
    bsz, seq, n_h, e = q.shape
    sub_len = seq // dilation
    n_blk = -(-sub_len // BLOCK)
    sub_pad = n_blk * BLOCK
    n_steps = window // dilation

    def to_sub(t):
        t = t.reshape(bsz, sub_len, dilation, n_h, e).transpose(0, 2, 3, 1, 4)
        return jnp.pad(t, ((0, 0), (0, 0), (0, 0), (0, sub_pad - sub_len), (0, 0)))

    def band(t):
        t = jnp.pad(t, ((0, 0), (0, 0), (0, 0), (BLOCK, 0), (0, 0)))
        t = t.reshape(bsz, dilation, n_h, n_blk + 1, BLOCK, e)
        return jnp.concatenate([t[:, :, :, :-1], t[:, :, :, 1:]], axis=4)

    qb = to_sub(q).reshape(bsz, dilation, n_h, n_blk, BLOCK, e)
    kb = band(to_sub(k))
    vb = band(to_sub(v))

    a_idx = jnp.arange(BLOCK)[:, None]
    b_idx = jnp.arange(2 * BLOCK)[None, :]
    steps = a_idx + BLOCK - b_idx
    key_sub = jnp.arange(n_blk)[:, None, None] * BLOCK - BLOCK + b_idx[None]
    valid = ((steps >= 0) & (steps <= n_steps))[None] & (key_sub >= 0)
    bucket = _t5_bucket(jnp.maximum(steps, 0) * dilation)
    bias = bias_tab[bucket].transpose(2, 0, 1).astype(jnp.float32)

    s = jnp.einsum('bdhnqe,bdhnke->bdhnqk', qb, kb).astype(jnp.float32) * (HEAD_DIM ** -0.5)
    s = s + bias[None, None, :, None]
    s = jnp.where(valid[None, None, None], s, NEG_INF)
    lse = jax.nn.logsumexp(s, axis=-1)
    p = jnp.exp(s - lse[..., None]).astype(v.dtype)
    o = jnp.einsum('bdhnqk,bdhnke->bdhnqe', p, vb)
    o = o.reshape(bsz, dilation, n_h, sub_pad, e)[:, :, :, :sub_len]
    lse = lse.reshape(bsz, dilation, n_h, sub_pad)[..., :sub_len]
    o = o.transpose(0, 3, 1, 2, 4).reshape(bsz, seq, n_h, e)
    lse = lse.transpose(0, 3, 1, 2).reshape(bsz, seq, n_h)
    return o, lse


def _layer_norm(x, g, b):
    xf = x.astype(jnp.float32)
    mu = jnp.mean(xf, axis=-1, keepdims=True)
    var = jnp.mean(jnp.square(xf - mu), axis=-1, keepdims=True)
    return ((xf - mu) * lax.rsqrt(var + LN_EPS) * g + b).astype(x.dtype)


def setup_inputs(seed: int = 0) -> dict:
    key = jax.random.key(seed)
    ks = jax.random.split(key, 14)
    f32 = jnp.float32
    x = jax.random.normal(ks[0], (BATCH, SEQ, D_MODEL), f32)
    c = jax.random.normal(ks[1], (BATCH, D_MODEL), f32)
    w_ada = jax.random.normal(ks[2], (DEPTH, D_MODEL, 3 * D_MODEL), f32) * (0.1 * D_MODEL ** -0.5)
    b_ada = jax.random.normal(ks[3], (DEPTH, 3 * D_MODEL), f32) * 0.01
    n_cols = sum(COL_WIDTHS)
    col_scale = np.ones((n_cols,), np.float32)
    col_scale[2 * QKV_WIDTH:3 * QKV_WIDTH] = DEEPNORM_BETA
    w_in = jax.random.normal(ks[4], (DEPTH, D_MODEL, n_cols), f32) * (D_MODEL ** -0.5) * jnp.asarray(col_scale)
    conv_w = jax.random.normal(ks[5], (DEPTH, CONV_K, CONV_WIDTH), f32) * (CONV_K ** -0.5)
    conv_b = jax.random.normal(ks[6], (DEPTH, CONV_WIDTH), f32) * 0.01
    rel_bias = jax.random.normal(ks[7], (N_BUCKETS, N_ATTN_HEADS), f32) * 0.5
    w_attn_out = jax.random.normal(ks[8], (DEPTH, ATTN_WIDTH, D_MODEL), f32) * (ATTN_WIDTH ** -0.5) * DEEPNORM_BETA
    w_conv_out = jax.random.normal(ks[9], (DEPTH, CONV_WIDTH, D_MODEL), f32) * (CONV_WIDTH ** -0.5) * DEEPNORM_BETA
    w_o = jax.random.normal(ks[10], (DEPTH, D_MODEL, D_MODEL), f32) * (D_MODEL ** -0.5) * DEEPNORM_BETA
    ln_g = 1.0 + jax.random.normal(ks[11], (DEPTH, D_MODEL), f32) * 0.01
    ln_b = jax.random.normal(ks[12], (DEPTH, D_MODEL), f32) * 0.01
    return {"x": x, "c": c, "w_ada": w_ada, "b_ada": b_ada, "w_in": w_in,
            "conv_w": conv_w, "conv_b": conv_b, "rel_bias": rel_bias,
            "w_attn_out": w_attn_out, "w_conv_out": w_conv_out, "w_o": w_o,
            "ln_g": ln_g, "ln_b": ln_b}


def reference(x, c, w_ada, b_ada, w_in, conv_w, conv_b, rel_bias,
              w_attn_out, w_conv_out, w_o, ln_g, ln_b):
    bsz, seq, _ = x.shape
    for layer in range(DEPTH):
        mod = jax.nn.silu(c) @ w_ada[layer] + b_ada[layer]
        shift, scale, gate = jnp.split(mod, 3, axis=-1)
        h = x * (1.0 + scale[:, None]) + shift[:, None]

        proj = h @ w_in[layer]
        q, k, v, g_attn, u, b_gate, c_gate, g_conv, m_attn, m_conv = _split_cols(proj)

        q = q.reshape(bsz, seq, N_ATTN_HEADS, HEAD_DIM)
        k = k.reshape(bsz, seq, N_ATTN_HEADS, HEAD_DIM)
        v = v.reshape(bsz, seq, N_ATTN_HEADS, HEAD_DIM)
        outs, lses = [], []
        for gi, (window, dilation) in enumerate(DILATED_GROUPS):
            hs = slice(gi * HEADS_PER_GROUP, (gi + 1) * HEADS_PER_GROUP)
            o_g, lse_g = _dilated_group_attention(q[:, :, hs], k[:, :, hs], v[:, :, hs],
                                                  rel_bias[:, hs], window, dilation)
            outs.append(o_g)
            lses.append(lse_g)
        o_all = jnp.stack(outs, axis=0)
        wts = jax.nn.softmax(jnp.stack(lses, axis=0), axis=0)
        o = jnp.sum(wts[..., None].astype(o_all.dtype) * o_all, axis=0).reshape(bsz, seq, ATTN_WIDTH)
        a_out = (o * jax.nn.silu(g_attn)) @ w_attn_out[layer]

        z = c_gate * u
        zp = jnp.pad(z, ((0, 0), (CONV_K - 1, 0), (0, 0)))
        cw = conv_w[layer]
        y_conv = (cw[0] * zp[:, :-2] + cw[1] * zp[:, 1:-1] + cw[2] * zp[:, 2:]) + conv_b[layer]
        s_out = (b_gate * y_conv * jax.nn.silu(g_conv)) @ w_conv_out[layer]

        merged = jax.nn.sigmoid(m_attn) * a_out + jax.nn.sigmoid(m_conv) * s_out
        y = merged @ w_o[layer]

        x = _layer_norm(DEEPNORM_ALPHA * x + (1.0 + gate[:, None]) * y, ln_g[layer], ln_b[layer])
    return x
```

```python
import functools
import math

import numpy as np
import jax
import jax.numpy as jnp
from jax import lax
from jax.experimental import pallas as pl
from jax.experimental.pallas import tpu as pltpu

F32 = jnp.float32
BF16 = jnp.bfloat16

D_MODEL = 1024
HEAD_DIM = 128
HEADS_PER_GROUP = 4
DILATED_GROUPS = ((128, 1), (512, 4), (2048, 16))
N_GROUPS = len(DILATED_GROUPS)
N_ATTN_HEADS = N_GROUPS * HEADS_PER_GROUP
QKV_WIDTH = N_ATTN_HEADS * HEAD_DIM
ATTN_WIDTH = HEADS_PER_GROUP * HEAD_DIM
CONV_WIDTH = D_MODEL
CONV_K = 3
N_BUCKETS = 32
MAX_EXACT = 16
MAX_DISTANCE = 2048
BLOCK = 128
LN_EPS = 1e-5
NEG_INF = -1e30
Q_SCALE = HEAD_DIM ** -0.5

COL_Q = 0
COL_K = QKV_WIDTH
COL_V = 2 * QKV_WIDTH
COL_GATTN = 3 * QKV_WIDTH
COL_U = COL_GATTN + ATTN_WIDTH
COL_B = COL_U + CONV_WIDTH
COL_C = COL_B + CONV_WIDTH
COL_GCONV = COL_C + CONV_WIDTH
COL_MATTN = COL_GCONV + CONV_WIDTH
COL_MCONV = COL_MATTN + D_MODEL
N_COLS = COL_MCONV + D_MODEL

V7X_VMEM_BYTES = 64 * 1024 * 1024
SUBLANES = 8
INPROJ_TM = 256
OUTPROJ_TM = 512
CONV_CHUNK = 256


def _vmem_limit(estimate_bytes):
    return int(min(V7X_VMEM_BYTES - (4 << 20), estimate_bytes * 5 // 4 + (4 << 20)))


def _bucket_table():
    a = np.arange(BLOCK)[:, None]
    b = np.arange(2 * BLOCK)[None, :]
    steps = a + BLOCK - b
    out = []
    for window, dilation in DILATED_GROUPS:
        n_steps = window // dilation
        valid = (steps >= 0) & (steps <= n_steps)
        dist = np.maximum(steps, 0) * dilation
        n = np.maximum(dist, 1).astype(np.float32)
        large = MAX_EXACT + (np.log(n / np.float32(MAX_EXACT)) / np.float32(math.log(MAX_DISTANCE / MAX_EXACT))
                             * np.float32(N_BUCKETS - MAX_EXACT)).astype(np.int32)
        large = np.minimum(large, N_BUCKETS - 1)
        bucket = np.where(dist < MAX_EXACT, dist, large)
        out.append(np.where(valid, bucket, -1).astype(np.int32))
    return np.stack(out)


_BUCKETS = _bucket_table()


def _sigmoid(v):
    return 1.0 / (1.0 + jnp.exp(-v))


def _ada_kernel(c_ref, w_ref, b_ref, o_ref):
    c = c_ref[...]
    o_ref[...] = jnp.dot(c * _sigmoid(c), w_ref[...], preferred_element_type=F32) + b_ref[...]


def _ada_call(c, w_ada, b_ada):
    bsz = c.shape[0]
    return pl.pallas_call(
        _ada_kernel,
        out_shape=jax.ShapeDtypeStruct((bsz, 3 * D_MODEL), F32),
        name="ada_mod",
    )(c, w_ada, b_ada[None, :])


def _bias_kernel(rb_ref, bucket_ref, o_ref):
    for g in range(N_GROUPS):
        bk = bucket_ref[g]
        for j in range(HEADS_PER_GROUP):
            h = g * HEADS_PER_GROUP + j
            acc = jnp.full(bk.shape, NEG_INF, F32)
            for k in range(N_BUCKETS):
                acc = jnp.where(bk == k, rb_ref[k, h], acc)
            o_ref[h] = acc


def _bias_call(rel_bias):
    return pl.pallas_call(
        _bias_kernel,
        out_shape=jax.ShapeDtypeStruct((N_ATTN_HEADS, BLOCK, 2 * BLOCK), F32),
        in_specs=[pl.BlockSpec(memory_space=pltpu.SMEM), pl.BlockSpec(memory_space=pltpu.VMEM)],
        out_specs=pl.BlockSpec(memory_space=pltpu.VMEM),
        name="bias_tables",
    )(rel_bias, jnp.asarray(_BUCKETS))


def _inproj_kernel(mod_ref, xn_ref, x4_ref, x16_ref, w_ref, wco_ref, cw_ref, cb_ref,
                   qkv1_ref, qkv2_ref, qkv3_ref, sg_ref, sa_ref, ms_ref,
                   zbuf_ref, spre_ref, *, tm):
    i = pl.program_id(1)
    shift = mod_ref[0, :, 0:D_MODEL]
    scale1 = 1.0 + mod_ref[0, :, D_MODEL:2 * D_MODEL]

    def modulate(xv):
        return (xv * scale1 + shift).astype(BF16)

    def proj(h, c0, width):
        return jnp.dot(h, w_ref[:, c0:c0 + width], preferred_element_type=F32)

    def qkv(h, g, store):
        for t, col in enumerate((COL_Q, COL_K, COL_V)):
            r = proj(h, col + g * ATTN_WIDTH, ATTN_WIDTH)
            if t == 0:
                r = r * Q_SCALE
            r = r.astype(BF16)
            for j in range(HEADS_PER_GROUP):
                store(t, j, r[:, j * HEAD_DIM:(j + 1) * HEAD_DIM])

    h_n = modulate(xn_ref[0])

    def store1(t, j, v):
        qkv1_ref[0, t, j] = v
    qkv(h_n, 0, store1)

    def store2(t, j, v):
        qkv2_ref[0, t, j, 0] = v
    qkv(modulate(x4_ref[0]), 1, store2)

    n_res = tm // BLOCK
    x16 = x16_ref[0]
    h_16 = modulate(jnp.concatenate(
        [x16[:, r * D_MODEL:(r + 1) * D_MODEL] for r in range(n_res)], axis=0))

    def store3(t, j, v):
        for r in range(n_res):
            qkv3_ref[0, t, j, r] = v[r * BLOCK:(r + 1) * BLOCK]
    qkv(h_16, 2, store3)

    ga = proj(h_n, COL_GATTN, ATTN_WIDTH)
    sg_ref[0] = (ga * _sigmoid(ga)).astype(BF16)

    for c0 in range(0, D_MODEL, 2 * CONV_CHUNK):
        ma = proj(h_n, COL_MATTN + c0, 2 * CONV_CHUNK)
        sa_ref[0, :, c0:c0 + 2 * CONV_CHUNK] = _sigmoid(ma).astype(BF16)

    @pl.when(i == 0)
    def _():
        zbuf_ref[0:SUBLANES, :] = jnp.zeros((SUBLANES, CONV_WIDTH), F32)

    for c0 in range(0, CONV_WIDTH, CONV_CHUNK):
        cs = slice(c0, c0 + CONV_CHUNK)
        u = proj(h_n, COL_U + c0, CONV_CHUNK)
        cg = proj(h_n, COL_C + c0, CONV_CHUNK)
        z = cg * u
        zbuf_ref[SUBLANES:SUBLANES + tm, cs] = z
        zm1 = zbuf_ref[SUBLANES - 1:SUBLANES - 1 + tm, cs]
        zm2 = zbuf_ref[SUBLANES - 2:SUBLANES - 2 + tm, cs]
        y = cw_ref[0:1, cs] * zm2 + cw_ref[1:2, cs] * zm1 + cw_ref[2:3, cs] * z + cb_ref[:, cs]
        bg = proj(h_n, COL_B + c0, CONV_CHUNK)
        gc = proj(h_n, COL_GCONV + c0, CONV_CHUNK)
        spre_ref[:, cs] = (bg * y * (gc * _sigmoid(gc))).astype(BF16)
    zbuf_ref[0:SUBLANES, :] = zbuf_ref[tm:tm + SUBLANES, :]

    spre = spre_ref[...]
    for c0 in range(0, D_MODEL, CONV_CHUNK):
        cs = slice(c0, c0 + CONV_CHUNK)
        so = jnp.dot(spre, wco_ref[:, cs], preferred_element_type=F32)
        mc = proj(h_n, COL_MCONV + c0, CONV_CHUNK)
        ms_ref[0, :, cs] = (_sigmoid(mc) * so).astype(BF16)


def _inproj_call(mod, x, w_in, w_conv_out, conv_w, conv_b):
    bsz, seq, _ = x.shape
    tm = INPROJ_TM
    n_t = seq // tm
    d4, d16 = DILATED_GROUPS[1][1], DILATED_GROUPS[2][1]
    sub4, sub16 = seq // d4, seq // d16
    assert seq % tm == 0 and sub4 % tm == 0 and sub16 == BLOCK and tm % BLOCK == 0
    t4 = sub4 // tm
    n_res = tm // BLOCK
    hg = HEADS_PER_GROUP

    x4 = x.reshape(bsz, sub4, d4 * D_MODEL)
    x16 = x.reshape(bsz, sub16, d16 * D_MODEL)

    in_specs = [
        pl.BlockSpec((1, 1, 3 * D_MODEL), lambda b, i: (b, 0, 0)),
        pl.BlockSpec((1, tm, D_MODEL), lambda b, i: (b, i, 0)),
        pl.BlockSpec((1, tm, D_MODEL), lambda b, i: (b, i % t4, i // t4)),
        pl.BlockSpec((1, BLOCK, n_res * D_MODEL), lambda b, i: (b, 0, i)),
        pl.BlockSpec((D_MODEL, N_COLS), lambda b, i: (0, 0), pipeline_mode=pl.Buffered(1)),
        pl.BlockSpec((CONV_WIDTH, D_MODEL), lambda b, i: (0, 0), pipeline_mode=pl.Buffered(1)),
        pl.BlockSpec((CONV_K, CONV_WIDTH), lambda b, i: (0, 0)),
        pl.BlockSpec((1, CONV_WIDTH), lambda b, i: (0, 0)),
    ]
    out_shape = [
        jax.ShapeDtypeStruct((bsz, 3, hg, seq, HEAD_DIM), BF16),
        jax.ShapeDtypeStruct((bsz, 3, hg, d4, sub4, HEAD_DIM), BF16),
        jax.ShapeDtypeStruct((bsz, 3, hg, d16, sub16, HEAD_DIM), BF16),
        jax.ShapeDtypeStruct((bsz, seq, ATTN_WIDTH), BF16),
        jax.ShapeDtypeStruct((bsz, seq, D_MODEL), BF16),
        jax.ShapeDtypeStruct((bsz, seq, D_MODEL), BF16),
    ]
    out_specs = [
        pl.BlockSpec((1, 3, hg, tm, HEAD_DIM), lambda b, i: (b, 0, 0, i, 0)),
        pl.BlockSpec((1, 3, hg, 1, tm, HEAD_DIM), lambda b, i: (b, 0, 0, i // t4, i % t4, 0)),
        pl.BlockSpec((1, 3, hg, n_res, BLOCK, HEAD_DIM), lambda b, i: (b, 0, 0, i, 0, 0)),
        pl.BlockSpec((1, tm, ATTN_WIDTH), lambda b, i: (b, i, 0)),
        pl.BlockSpec((1, tm, D_MODEL), lambda b, i: (b, i, 0)),
        pl.BlockSpec((1, tm, D_MODEL), lambda b, i: (b, i, 0)),
    ]
    weights = D_MODEL * N_COLS * 2 + CONV_WIDTH * D_MODEL * 2
    x_tiles = 3 * tm * D_MODEL * 4 * 2
    out_tiles = (3 * 3 * ATTN_WIDTH + ATTN_WIDTH + 2 * D_MODEL) * tm * 2 * 2
    scratch = (tm + SUBLANES) * CONV_WIDTH * 4 + tm * CONV_WIDTH * 2
    return pl.pallas_call(
        functools.partial(_inproj_kernel, tm=tm),
        grid=(bsz, n_t),
        in_specs=in_specs,
        out_specs=out_specs,
        out_shape=out_shape,
        scratch_shapes=[pltpu.VMEM((tm + SUBLANES, CONV_WIDTH), F32),
                        pltpu.VMEM((tm, CONV_WIDTH), BF16)],
        compiler_params=pltpu.CompilerParams(
            dimension_semantics=("arbitrary", "arbitrary"),
            vmem_limit_bytes=_vmem_limit(weights + x_tiles + out_tiles + scratch)),
        name="in_proj",
    )(mod, x, x4, x16, w_in, w_conv_out, conv_w, conv_b)


def _attn_block(q, k, v, bias):
    s = lax.dot_general(q, k, (((1,), (1,)), ((), ())), preferred_element_type=F32) + bias
    m = jnp.max(s, axis=1, keepdims=True)
    p = jnp.exp(s - m)
    l = jnp.sum(p, axis=1, keepdims=True)
    acc = jnp.dot(p.astype(BF16), v, preferred_element_type=F32)
    return acc * (1.0 / l), m + jnp.log(l)


def _attn_kernel(qkv1_ref, qkv2_ref, qkv3_ref, bias_ref, o_ref,
                 o1_ref, o2_ref, o3_ref, l1_ref, l2_ref, l3_ref, *, seq):
    def band(load, n, g):
        rows = slice(n * BLOCK, (n + 1) * BLOCK)
        if n == 0:
            return _attn_block(load(0, rows), load(1, rows), load(2, rows),
                               bias_ref[g, 0, :, BLOCK:2 * BLOCK])
        krows = slice((n - 1) * BLOCK, (n + 1) * BLOCK)
        return _attn_block(load(0, rows), load(1, krows), load(2, krows), bias_ref[g, 0])

    def put(o_dst, l_dst, idx, o, lse):
        o_dst[idx, :] = o
        l_dst[idx, :] = jnp.broadcast_to(lse, (BLOCK, HEAD_DIM))

    d4, d16 = DILATED_GROUPS[1][1], DILATED_GROUPS[2][1]
    for n in range(seq // BLOCK):
        o, lse = band(lambda t, rows: qkv1_ref[0, t, 0, rows, :], n, 0)
        put(o1_ref, l1_ref, slice(n * BLOCK, (n + 1) * BLOCK), o, lse)
    for r in range(d4):
        for n in range(seq // d4 // BLOCK):
            o, lse = band(lambda t, rows: qkv2_ref[0, t, 0, r, rows, :], n, 1)
            put(o2_ref, l2_ref, pl.ds(n * BLOCK * d4 + r, BLOCK, stride=d4), o, lse)
    for r in range(d16):
        for n in range(seq // d16 // BLOCK):
            o, lse = band(lambda t, rows: qkv3_ref[0, t, 0, r, rows, :], n, 2)
            put(o3_ref, l3_ref, pl.ds(n * BLOCK * d16 + r, BLOCK, stride=d16), o, lse)

    chunk = 2 * BLOCK
    for c0 in range(0, seq, chunk):
        rows = slice(c0, c0 + chunk)
        la, lb, lc = l1_ref[rows, :], l2_ref[rows, :], l3_ref[rows, :]
        m = jnp.maximum(jnp.maximum(la, lb), lc)
        wa, wb, wc = jnp.exp(la - m), jnp.exp(lb - m), jnp.exp(lc - m)
        num = wa * o1_ref[rows, :] + wb * o2_ref[rows, :] + wc * o3_ref[rows, :]
        o_ref[0, 0, rows, :] = (num * (1.0 / (wa + wb + wc))).astype(BF16)


def _attn_call(qkv1, qkv2, qkv3, bias):
    bsz, _, hg, seq, _ = qkv1.shape
    d4, d16 = DILATED_GROUPS[1][1], DILATED_GROUPS[2][1]
    bias = bias.reshape(N_GROUPS, hg, BLOCK, 2 * BLOCK)
    in_tiles = 3 * 3 * seq * HEAD_DIM * 2 * 2 + N_GROUPS * BLOCK * 2 * BLOCK * 4 * 2
    out_tiles = seq * HEAD_DIM * 2 * 2
    scratch = 6 * seq * HEAD_DIM * 4
    return pl.pallas_call(
        functools.partial(_attn_kernel, seq=seq),
        grid=(bsz, hg),
        in_specs=[
            pl.BlockSpec((1, 3, 1, seq, HEAD_DIM), lambda b, j: (b, 0, j, 0, 0)),
            pl.BlockSpec((1, 3, 1, d4, seq // d4, HEAD_DIM), lambda b, j: (b, 0, j, 0, 0, 0)),
            pl.BlockSpec((1, 3, 1, d16, seq // d16, HEAD_DIM), lambda b, j: (b, 0, j, 0, 0, 0)),
            pl.BlockSpec((N_GROUPS, 1, BLOCK, 2 * BLOCK), lambda b, j: (0, j, 0, 0)),
        ],
        out_specs=pl.BlockSpec((1, 1, seq, HEAD_DIM), lambda b, j: (b, j, 0, 0)),
        out_shape=jax.ShapeDtypeStruct((bsz, hg, seq, HEAD_DIM), BF16),
        scratch_shapes=[pltpu.VMEM((seq, HEAD_DIM), F32) for _ in range(6)],
        compiler_params=pltpu.CompilerParams(
            dimension_semantics=("arbitrary", "arbitrary"),
            vmem_limit_bytes=_vmem_limit(in_tiles + out_tiles + scratch)),
        name="dilated_attn",
    )(qkv1, qkv2, qkv3, bias)


def _outproj_kernel(mod_ref, o_ref, sg_ref, sa_ref, ms_ref, x_ref, wao_ref, wo_ref,
                    lng_ref, lnb_ref, out_ref, *, alpha):
    gate1 = 1.0 + mod_ref[0, :, 2 * D_MODEL:3 * D_MODEL]
    o = jnp.concatenate([o_ref[0, j] for j in range(HEADS_PER_GROUP)], axis=1)
    a_in = (o.astype(F32) * sg_ref[0].astype(F32)).astype(BF16)
    a_out = jnp.dot(a_in, wao_ref[...], preferred_element_type=F32)
    merged = (sa_ref[0].astype(F32) * a_out + ms_ref[0].astype(F32)).astype(BF16)
    y = jnp.dot(merged, wo_ref[...], preferred_element_type=F32)
    v = alpha * x_ref[0] + gate1 * y
    mu = jnp.mean(v, axis=-1, keepdims=True)
    dv = v - mu
    var = jnp.mean(dv * dv, axis=-1, keepdims=True)
    out_ref[0] = dv * lax.rsqrt(var + LN_EPS) * lng_ref[...] + lnb_ref[...]


def _outproj_call(mod, o, sg, sa, ms, x, w_attn_out, w_o, ln_g, ln_b, alpha):
    bsz, seq, _ = x.shape
    tm = OUTPROJ_TM
    assert seq % tm == 0
    hg = HEADS_PER_GROUP
    tok = lambda b, i: (b, i, 0)
    const = lambda b, i: (0, 0)
    tiles = (2 * ATTN_WIDTH * 2 + 2 * D_MODEL * 2 + 2 * D_MODEL * 4) * tm * 2
    weights = (ATTN_WIDTH + D_MODEL) * D_MODEL * 2 * 2
    temps = 4 * tm * D_MODEL * 4
    return pl.pallas_call(
        functools.partial(_outproj_kernel, alpha=alpha),
        grid=(bsz, seq // tm),
        in_specs=[
            pl.BlockSpec((1, 1, 3 * D_MODEL), lambda b, i: (b, 0, 0)),
            pl.BlockSpec((1, hg, tm, HEAD_DIM), lambda b, i: (b, 0, i, 0)),
            pl.BlockSpec((1, tm, ATTN_WIDTH), tok),
            pl.BlockSpec((1, tm, D_MODEL), tok),
            pl.BlockSpec((1, tm, D_MODEL), tok),
            pl.BlockSpec((1, tm, D_MODEL), tok),
            pl.BlockSpec((ATTN_WIDTH, D_MODEL), const),
            pl.BlockSpec((D_MODEL, D_MODEL), const),
            pl.BlockSpec((1, D_MODEL), const),
            pl.BlockSpec((1, D_MODEL), const),
        ],
        out_specs=pl.BlockSpec((1, tm, D_MODEL), tok),
        out_shape=jax.ShapeDtypeStruct((bsz, seq, D_MODEL), F32),
        compiler_params=pltpu.CompilerParams(
            dimension_semantics=("arbitrary", "arbitrary"),
            vmem_limit_bytes=_vmem_limit(tiles + weights + temps)),
        name="out_proj",
    )(mod, o, sg, sa, ms, x, w_attn_out, w_o, ln_g, ln_b)


def kernel(x, c, w_ada, b_ada, w_in, conv_w, conv_b, rel_bias, w_attn_out, w_conv_out, w_o, ln_g, ln_b):
    bsz = x.shape[0]
    depth = w_in.shape[0]
    alpha = (2.0 * depth) ** 0.25
    bias = _bias_call(rel_bias)
    for layer in range(depth):
        mod = _ada_call(c, w_ada[layer], b_ada[layer]).reshape(bsz, 1, 3 * D_MODEL)
        qkv1, qkv2, qkv3, sg, sa, ms = _inproj_call(
            mod, x, w_in[layer].astype(BF16), w_conv_out[layer].astype(BF16),
            conv_w[layer], conv_b[layer][None, :])
        o = _attn_call(qkv1, qkv2, qkv3, bias)
        x = _outproj_call(mod, o, sg, sa, ms, x, w_attn_out[layer].astype(BF16),
                          w_o[layer].astype(BF16), ln_g[layer][None, :], ln_b[layer][None, :], alpha)
    return x
```

```python
import functools
import math

import numpy as np
import jax
import jax.numpy as jnp
from jax import lax
from jax.experimental import pallas as pl
from jax.experimental.pallas import tpu as pltpu

F32 = jnp.float32
BF16 = jnp.bfloat16

D_MODEL = 1024
HEAD_DIM = 128
HEADS_PER_GROUP = 4
DILATED_GROUPS = ((128, 1), (512, 4), (2048, 16))
N_GROUPS = len(DILATED_GROUPS)
N_ATTN_HEADS = N_GROUPS * HEADS_PER_GROUP
QKV_WIDTH = N_ATTN_HEADS * HEAD_DIM
ATTN_WIDTH = HEADS_PER_GROUP * HEAD_DIM
CONV_WIDTH = D_MODEL
CONV_K = 3
N_BUCKETS = 32
MAX_EXACT = 16
MAX_DISTANCE = 2048
BLOCK = 128
LN_EPS = 1e-5
NEG_INF = -1e30
Q_SCALE = HEAD_DIM ** -0.5

COL_Q = 0
COL_K = QKV_WIDTH
COL_V = 2 * QKV_WIDTH
COL_GATTN = 3 * QKV_WIDTH
COL_U = COL_GATTN + ATTN_WIDTH
COL_B = COL_U + CONV_WIDTH
COL_C = COL_B + CONV_WIDTH
COL_GCONV = COL_C + CONV_WIDTH
COL_MATTN = COL_GCONV + CONV_WIDTH
COL_MCONV = COL_MATTN + D_MODEL
N_COLS = COL_MCONV + D_MODEL

V7X_VMEM_BYTES = 64 * 1024 * 1024
SUBLANES = 8
INPROJ_TM = 256
OUTPROJ_TM = 512
CONV_CHUNK = 256


def _vmem_limit(estimate_bytes):
    return int(min(V7X_VMEM_BYTES - (4 << 20), estimate_bytes * 5 // 4 + (4 << 20)))


def _bucket_table():
    a = np.arange(BLOCK)[:, None]
    b = np.arange(2 * BLOCK)[None, :]
    steps = a + BLOCK - b
    out = []
    for window, dilation in DILATED_GROUPS:
        n_steps = window // dilation
        valid = (steps >= 0) & (steps <= n_steps)
        dist = np.maximum(steps, 0) * dilation
        n = np.maximum(dist, 1).astype(np.float32)
        large = MAX_EXACT + (np.log(n / np.float32(MAX_EXACT)) / np.float32(math.log(MAX_DISTANCE / MAX_EXACT))
                             * np.float32(N_BUCKETS - MAX_EXACT)).astype(np.int32)
        large = np.minimum(large, N_BUCKETS - 1)
        bucket = np.where(dist < MAX_EXACT, dist, large)
        out.append(np.where(valid, bucket, -1).astype(np.int32))
    return np.stack(out)


_BUCKETS = _bucket_table()


def _sigmoid(v):
    return 1.0 / (1.0 + jnp.exp(-v))


def _ada_kernel(c_ref, w_ref, b_ref, o_ref):
    c = c_ref[...]
    o_ref[...] = jnp.dot(c * _sigmoid(c), w_ref[...], preferred_element_type=F32) + b_ref[...]


def _ada_call(c, w_ada, b_ada):
    bsz = c.shape[0]
    return pl.pallas_call(
        _ada_kernel,
        out_shape=jax.ShapeDtypeStruct((bsz, 3 * D_MODEL), F32),
        name="ada_mod",
    )(c, w_ada, b_ada[None, :])


def _bias_kernel(rb_ref, bucket_ref, o_ref):
    for g in range(N_GROUPS):
        bk = bucket_ref[g]
        for j in range(HEADS_PER_GROUP):
            h = g * HEADS_PER_GROUP + j
            acc = jnp.full(bk.shape, NEG_INF, F32)
            for k in range(N_BUCKETS):
                acc = jnp.where(bk == k, rb_ref[k, h], acc)
            o_ref[h] = acc


def _bias_call(rel_bias):
    return pl.pallas_call(
        _bias_kernel,
        out_shape=jax.ShapeDtypeStruct((N_ATTN_HEADS, BLOCK, 2 * BLOCK), F32),
        in_specs=[pl.BlockSpec(memory_space=pltpu.SMEM), pl.BlockSpec(memory_space=pltpu.VMEM)],
        out_specs=pl.BlockSpec(memory_space=pltpu.VMEM),
        name="bias_tables",
    )(rel_bias, jnp.asarray(_BUCKETS))


def _inproj_kernel(mod_ref, x_ref, w_ref, wco_ref, cw_ref, cb_ref,
                   qkv1_ref, qkv2_ref, qkv3_ref, sg_ref, sa_ref, ms_ref,
                   hs_ref, zbuf_ref, spre_ref, *, tm):
    i = pl.program_id(1)
    shift = mod_ref[0, :, 0:D_MODEL]
    scale1 = 1.0 + mod_ref[0, :, D_MODEL:2 * D_MODEL]
    h = x_ref[0] * scale1 + shift
    h_n = h.astype(BF16)

    n_slab = D_MODEL // HEAD_DIM
    for s in range(n_slab):
        hs_ref[s] = h[:, s * HEAD_DIM:(s + 1) * HEAD_DIM]

    def by_residue(d):
        return jnp.concatenate(
            [jnp.concatenate([hs_ref[s, pl.ds(r, tm // d, stride=d), :] for s in range(n_slab)], axis=1)
             for r in range(d)], axis=0).astype(BF16)

    def proj(hv, c0, width):
        return jnp.dot(hv, w_ref[:, c0:c0 + width], preferred_element_type=F32)

    def qkv(hv, g, store):
        for t, col in enumerate((COL_Q, COL_K, COL_V)):
            r = proj(hv, col + g * ATTN_WIDTH, ATTN_WIDTH)
            if t == 0:
                r = r * Q_SCALE
            r = r.astype(BF16)
            for j in range(HEADS_PER_GROUP):
                store(t, j, r[:, j * HEAD_DIM:(j + 1) * HEAD_DIM])

    def store1(t, j, v):
        qkv1_ref[0, t, j] = v
    qkv(h_n, 0, store1)

    def store_sub(ref, d):
        def store(t, j, v):
            rows = tm // d
            for r in range(d):
                ref[0, t, j, r] = v[r * rows:(r + 1) * rows]
        return store
    d4, d16 = DILATED_GROUPS[1][1], DILATED_GROUPS[2][1]
    qkv(by_residue(d4), 1, store_sub(qkv2_ref, d4))
    qkv(by_residue(d16), 2, store_sub(qkv3_ref, d16))

    ga = proj(h_n, COL_GATTN, ATTN_WIDTH)
    sg_ref[0] = (ga * _sigmoid(ga)).astype(BF16)

    for c0 in range(0, D_MODEL, 2 * CONV_CHUNK):
        ma = proj(h_n, COL_MATTN + c0, 2 * CONV_CHUNK)
        sa_ref[0, :, c0:c0 + 2 * CONV_CHUNK] = _sigmoid(ma).astype(BF16)

    @pl.when(i == 0)
    def _():
        zbuf_ref[0:SUBLANES, :] = jnp.zeros((SUBLANES, CONV_WIDTH), F32)

    for c0 in range(0, CONV_WIDTH, CONV_CHUNK):
        cs = slice(c0, c0 + CONV_CHUNK)
        u = proj(h_n, COL_U + c0, CONV_CHUNK)
        cg = proj(h_n, COL_C + c0, CONV_CHUNK)
        z = cg * u
        zbuf_ref[SUBLANES:SUBLANES + tm, cs] = z
        zm1 = zbuf_ref[SUBLANES - 1:SUBLANES - 1 + tm, cs]
        zm2 = zbuf_ref[SUBLANES - 2:SUBLANES - 2 + tm, cs]
        y = cw_ref[0:1, cs] * zm2 + cw_ref[1:2, cs] * zm1 + cw_ref[2:3, cs] * z + cb_ref[:, cs]
        bg = proj(h_n, COL_B + c0, CONV_CHUNK)
        gc = proj(h_n, COL_GCONV + c0, CONV_CHUNK)
        spre_ref[:, cs] = (bg * y * (gc * _sigmoid(gc))).astype(BF16)
    zbuf_ref[0:SUBLANES, :] = zbuf_ref[tm:tm + SUBLANES, :]

    spre = spre_ref[...]
    for c0 in range(0, D_MODEL, CONV_CHUNK):
        cs = slice(c0, c0 + CONV_CHUNK)
        so = jnp.dot(spre, wco_ref[:, cs], preferred_element_type=F32)
        mc = proj(h_n, COL_MCONV + c0, CONV_CHUNK)
        ms_ref[0, :, cs] = (_sigmoid(mc) * so).astype(BF16)


def _inproj_call(mod, x, w_in, w_conv_out, conv_w, conv_b):
    bsz, seq, _ = x.shape
    tm = INPROJ_TM
    n_t = seq // tm
    d4, d16 = DILATED_GROUPS[1][1], DILATED_GROUPS[2][1]
    sub4, sub16 = seq // d4, seq // d16
    assert seq % tm == 0 and tm % (d16 * 16) == 0
    hg = HEADS_PER_GROUP

    in_specs = [
        pl.BlockSpec((1, 1, 3 * D_MODEL), lambda b, i: (b, 0, 0)),
        pl.BlockSpec((1, tm, D_MODEL), lambda b, i: (b, i, 0)),
        pl.BlockSpec((D_MODEL, N_COLS), lambda b, i: (0, 0), pipeline_mode=pl.Buffered(1)),
        pl.BlockSpec((CONV_WIDTH, D_MODEL), lambda b, i: (0, 0), pipeline_mode=pl.Buffered(1)),
        pl.BlockSpec((CONV_K, CONV_WIDTH), lambda b, i: (0, 0)),
        pl.BlockSpec((1, CONV_WIDTH), lambda b, i: (0, 0)),
    ]
    out_shape = [
        jax.ShapeDtypeStruct((bsz, 3, hg, seq, HEAD_DIM), BF16),
        jax.ShapeDtypeStruct((bsz, 3, hg, d4, sub4, HEAD_DIM), BF16),
        jax.ShapeDtypeStruct((bsz, 3, hg, d16, sub16, HEAD_DIM), BF16),
        jax.ShapeDtypeStruct((bsz, seq, ATTN_WIDTH), BF16),
        jax.ShapeDtypeStruct((bsz, seq, D_MODEL), BF16),
        jax.ShapeDtypeStruct((bsz, seq, D_MODEL), BF16),
    ]
    out_specs = [
        pl.BlockSpec((1, 3, hg, tm, HEAD_DIM), lambda b, i: (b, 0, 0, i, 0)),
        pl.BlockSpec((1, 3, hg, d4, tm // d4, HEAD_DIM), lambda b, i: (b, 0, 0, 0, i, 0)),
        pl.BlockSpec((1, 3, hg, d16, tm // d16, HEAD_DIM), lambda b, i: (b, 0, 0, 0, i, 0)),
        pl.BlockSpec((1, tm, ATTN_WIDTH), lambda b, i: (b, i, 0)),
        pl.BlockSpec((1, tm, D_MODEL), lambda b, i: (b, i, 0)),
        pl.BlockSpec((1, tm, D_MODEL), lambda b, i: (b, i, 0)),
    ]
    weights = D_MODEL * N_COLS * 2 + CONV_WIDTH * D_MODEL * 2
    x_tiles = tm * D_MODEL * 4 * 2
    out_tiles = (3 * 3 * ATTN_WIDTH + ATTN_WIDTH + 2 * D_MODEL) * tm * 2 * 2
    scratch = tm * D_MODEL * 4 + (tm + SUBLANES) * CONV_WIDTH * 4 + tm * CONV_WIDTH * 2
    return pl.pallas_call(
        functools.partial(_inproj_kernel, tm=tm),
        grid=(bsz, n_t),
        in_specs=in_specs,
        out_specs=out_specs,
        out_shape=out_shape,
        scratch_shapes=[pltpu.VMEM((D_MODEL // HEAD_DIM, tm, HEAD_DIM), F32),
                        pltpu.VMEM((tm + SUBLANES, CONV_WIDTH), F32),
                        pltpu.VMEM((tm, CONV_WIDTH), BF16)],
        compiler_params=pltpu.CompilerParams(
            dimension_semantics=("arbitrary", "arbitrary"),
            vmem_limit_bytes=_vmem_limit(weights + x_tiles + out_tiles + scratch)),
        name="in_proj",
    )(mod, x, w_in, w_conv_out, conv_w, conv_b)


def _attn_block(q, k, v, bias):
    s = lax.dot_general(q, k, (((1,), (1,)), ((), ())), preferred_element_type=F32) + bias
    m = jnp.max(s, axis=1, keepdims=True)
    p = jnp.exp(s - m)
    l = jnp.sum(p, axis=1, keepdims=True)
    acc = jnp.dot(p.astype(BF16), v, preferred_element_type=F32)
    return acc * (1.0 / l), m + jnp.log(l)


def _attn_kernel(qkv1_ref, qkv2_ref, qkv3_ref, bias_ref, o_ref,
                 o1_ref, o2_ref, o3_ref, l1_ref, l2_ref, l3_ref, *, seq):
    def band(load, n, g):
        rows = slice(n * BLOCK, (n + 1) * BLOCK)
        if n == 0:
            return _attn_block(load(0, rows), load(1, rows), load(2, rows),
                               bias_ref[g, 0, :, BLOCK:2 * BLOCK])
        krows = slice((n - 1) * BLOCK, (n + 1) * BLOCK)
        return _attn_block(load(0, rows), load(1, krows), load(2, krows), bias_ref[g, 0])

    def put(o_dst, l_dst, idx, o, lse):
        o_dst[idx, :] = o
        l_dst[idx, :] = jnp.broadcast_to(lse, (BLOCK, HEAD_DIM))

    d4, d16 = DILATED_GROUPS[1][1], DILATED_GROUPS[2][1]
    for n in range(seq // BLOCK):
        o, lse = band(lambda t, rows: qkv1_ref[0, t, 0, rows, :], n, 0)
        put(o1_ref, l1_ref, slice(n * BLOCK, (n + 1) * BLOCK), o, lse)
    for r in range(d4):
        for n in range(seq // d4 // BLOCK):
            o, lse = band(lambda t, rows: qkv2_ref[0, t, 0, r, rows, :], n, 1)
            put(o2_ref, l2_ref, pl.ds(n * BLOCK * d4 + r, BLOCK, stride=d4), o, lse)
    for r in range(d16):
        for n in range(seq // d16 // BLOCK):
            o, lse = band(lambda t, rows: qkv3_ref[0, t, 0, r, rows, :], n, 2)
            put(o3_ref, l3_ref, pl.ds(n * BLOCK * d16 + r, BLOCK, stride=d16), o, lse)

    chunk = 2 * BLOCK
    for c0 in range(0, seq, chunk):
        rows = slice(c0, c0 + chunk)
        la, lb, lc = l1_ref[rows, :], l2_ref[rows, :], l3_ref[rows, :]
        m = jnp.maximum(jnp.maximum(la, lb), lc)
        wa, wb, wc = jnp.exp(la - m), jnp.exp(lb - m), jnp.exp(lc - m)
        num = wa * o1_ref[rows, :] + wb * o2_ref[rows, :] + wc * o3_ref[rows, :]
        o_ref[0, 0, rows, :] = (num * (1.0 / (wa + wb + wc))).astype(BF16)


def _attn_call(qkv1, qkv2, qkv3, bias):
    bsz, _, hg, seq, _ = qkv1.shape
    d4, d16 = DILATED_GROUPS[1][1], DILATED_GROUPS[2][1]
    bias = bias.reshape(N_GROUPS, hg, BLOCK, 2 * BLOCK)
    in_tiles = 3 * 3 * seq * HEAD_DIM * 2 * 2 + N_GROUPS * BLOCK * 2 * BLOCK * 4 * 2
    out_tiles = seq * HEAD_DIM * 2 * 2
    scratch = 6 * seq * HEAD_DIM * 4
    return pl.pallas_call(
        functools.partial(_attn_kernel, seq=seq),
        grid=(bsz, hg),
        in_specs=[
            pl.BlockSpec((1, 3, 1, seq, HEAD_DIM), lambda b, j: (b, 0, j, 0, 0)),
            pl.BlockSpec((1, 3, 1, d4, seq // d4, HEAD_DIM), lambda b, j: (b, 0, j, 0, 0, 0)),
            pl.BlockSpec((1, 3, 1, d16, seq // d16, HEAD_DIM), lambda b, j: (b, 0, j, 0, 0, 0)),
            pl.BlockSpec((N_GROUPS, 1, BLOCK, 2 * BLOCK), lambda b, j: (0, j, 0, 0)),
        ],
        out_specs=pl.BlockSpec((1, 1, seq, HEAD_DIM), lambda b, j: (b, j, 0, 0)),
        out_shape=jax.ShapeDtypeStruct((bsz, hg, seq, HEAD_DIM), BF16),
        scratch_shapes=[pltpu.VMEM((seq, HEAD_DIM), F32) for _ in range(6)],
        compiler_params=pltpu.CompilerParams(
            dimension_semantics=("arbitrary", "arbitrary"),
            vmem_limit_bytes=_vmem_limit(in_tiles + out_tiles + scratch)),
        name="dilated_attn",
    )(qkv1, qkv2, qkv3, bias)


def _outproj_kernel(mod_ref, o_ref, sg_ref, sa_ref, ms_ref, x_ref, wao_ref, wo_ref,
                    lng_ref, lnb_ref, out_ref, *, alpha):
    gate1 = 1.0 + mod_ref[0, :, 2 * D_MODEL:3 * D_MODEL]
    o = jnp.concatenate([o_ref[0, j] for j in range(HEADS_PER_GROUP)], axis=1)
    a_in = (o.astype(F32) * sg_ref[0].astype(F32)).astype(BF16)
    a_out = jnp.dot(a_in, wao_ref[...], preferred_element_type=F32)
    merged = (sa_ref[0].astype(F32) * a_out + ms_ref[0].astype(F32)).astype(BF16)
    y = jnp.dot(merged, wo_ref[...], preferred_element_type=F32)
    v = alpha * x_ref[0] + gate1 * y
    mu = jnp.mean(v, axis=-1, keepdims=True)
    dv = v - mu
    var = jnp.mean(dv * dv, axis=-1, keepdims=True)
    out_ref[0] = dv * lax.rsqrt(var + LN_EPS) * lng_ref[...] + lnb_ref[...]


def _outproj_call(mod, o, sg, sa, ms, x, w_attn_out, w_o, ln_g, ln_b, alpha):
    bsz, seq, _ = x.shape
    tm = OUTPROJ_TM
    assert seq % tm == 0
    hg = HEADS_PER_GROUP
    tok = lambda b, i: (b, i, 0)
    const = lambda b, i: (0, 0)
    tiles = (2 * ATTN_WIDTH * 2 + 2 * D_MODEL * 2 + 2 * D_MODEL * 4) * tm * 2
    weights = (ATTN_WIDTH + D_MODEL) * D_MODEL * 2 * 2
    temps = 4 * tm * D_MODEL * 4
    return pl.pallas_call(
        functools.partial(_outproj_kernel, alpha=alpha),
        grid=(bsz, seq // tm),
        in_specs=[
            pl.BlockSpec((1, 1, 3 * D_MODEL), lambda b, i: (b, 0, 0)),
            pl.BlockSpec((1, hg, tm, HEAD_DIM), lambda b, i: (b, 0, i, 0)),
            pl.BlockSpec((1, tm, ATTN_WIDTH), tok),
            pl.BlockSpec((1, tm, D_MODEL), tok),
            pl.BlockSpec((1, tm, D_MODEL), tok),
            pl.BlockSpec((1, tm, D_MODEL), tok),
            pl.BlockSpec((ATTN_WIDTH, D_MODEL), const),
            pl.BlockSpec((D_MODEL, D_MODEL), const),
            pl.BlockSpec((1, D_MODEL), const),
            pl.BlockSpec((1, D_MODEL), const),
        ],
        out_specs=pl.BlockSpec((1, tm, D_MODEL), tok),
        out_shape=jax.ShapeDtypeStruct((bsz, seq, D_MODEL), F32),
        compiler_params=pltpu.CompilerParams(
            dimension_semantics=("arbitrary", "arbitrary"),
            vmem_limit_bytes=_vmem_limit(tiles + weights + temps)),
        name="out_proj",
    )(mod, o, sg, sa, ms, x, w_attn_out, w_o, ln_g, ln_b)


def kernel(x, c, w_ada, b_ada, w_in, conv_w, conv_b, rel_bias, w_attn_out, w_conv_out, w_o, ln_g, ln_b):
    bsz = x.shape[0]
    depth = w_in.shape[0]
    alpha = (2.0 * depth) ** 0.25
    bias = _bias_call(rel_bias)
    for layer in range(depth):
        mod = _ada_call(c, w_ada[layer], b_ada[layer]).reshape(bsz, 1, 3 * D_MODEL)
        qkv1, qkv2, qkv3, sg, sa, ms = _inproj_call(
            mod, x, w_in[layer].astype(BF16), w_conv_out[layer].astype(BF16),
            conv_w[layer], conv_b[layer][None, :])
        o = _attn_call(qkv1, qkv2, qkv3, bias)
        x = _outproj_call(mod, o, sg, sa, ms, x, w_attn_out[layer].astype(BF16),
                          w_o[layer].astype(BF16), ln_g[layer][None, :], ln_b[layer][None, :], alpha)
    return x
```

```python
import functools
import math

import numpy as np
import jax
import jax.numpy as jnp
from jax import lax
from jax.experimental import pallas as pl
from jax.experimental.pallas import tpu as pltpu

F32 = jnp.float32
BF16 = jnp.bfloat16

D_MODEL = 1024
HEAD_DIM = 128
HEADS_PER_GROUP = 4
DILATED_GROUPS = ((128, 1), (512, 4), (2048, 16))
N_GROUPS = len(DILATED_GROUPS)
N_ATTN_HEADS = N_GROUPS * HEADS_PER_GROUP
QKV_WIDTH = N_ATTN_HEADS * HEAD_DIM
ATTN_WIDTH = HEADS_PER_GROUP * HEAD_DIM
CONV_WIDTH = D_MODEL
CONV_K = 3
N_BUCKETS = 32
MAX_EXACT = 16
MAX_DISTANCE = 2048
BLOCK = 128
LN_EPS = 1e-5
NEG_INF = -1e30
Q_SCALE = HEAD_DIM ** -0.5

COL_Q = 0
COL_K = QKV_WIDTH
COL_V = 2 * QKV_WIDTH
COL_GATTN = 3 * QKV_WIDTH
COL_U = COL_GATTN + ATTN_WIDTH
COL_B = COL_U + CONV_WIDTH
COL_C = COL_B + CONV_WIDTH
COL_GCONV = COL_C + CONV_WIDTH
COL_MATTN = COL_GCONV + CONV_WIDTH
COL_MCONV = COL_MATTN + D_MODEL
N_COLS = COL_MCONV + D_MODEL

V7X_VMEM_BYTES = 64 * 1024 * 1024
SUBLANES = 8
INPROJ_TM = 512
OUTPROJ_TM = 512
CONV_CHUNK = 256


def _vmem_limit(estimate_bytes):
    return int(min(V7X_VMEM_BYTES - (4 << 20), estimate_bytes * 5 // 4 + (4 << 20)))


def _bucket_table():
    a = np.arange(BLOCK)[:, None]
    b = np.arange(2 * BLOCK)[None, :]
    steps = a + BLOCK - b
    out = []
    for window, dilation in DILATED_GROUPS:
        n_steps = window // dilation
        valid = (steps >= 0) & (steps <= n_steps)
        dist = np.maximum(steps, 0) * dilation
        n = np.maximum(dist, 1).astype(np.float32)
        large = MAX_EXACT + (np.log(n / np.float32(MAX_EXACT)) / np.float32(math.log(MAX_DISTANCE / MAX_EXACT))
                             * np.float32(N_BUCKETS - MAX_EXACT)).astype(np.int32)
        large = np.minimum(large, N_BUCKETS - 1)
        bucket = np.where(dist < MAX_EXACT, dist, large)
        out.append(np.where(valid, bucket, -1).astype(np.int32))
    return np.stack(out)


_BUCKETS = _bucket_table()


def _sigmoid(v):
    return 1.0 / (1.0 + jnp.exp(-v))


def _ada_kernel(c_ref, w_ref, b_ref, o_ref):
    c = c_ref[...]
    o_ref[...] = jnp.dot(c * _sigmoid(c), w_ref[...], preferred_element_type=F32) + b_ref[...]


def _ada_call(c, w_ada, b_ada):
    bsz = c.shape[0]
    return pl.pallas_call(
        _ada_kernel,
        out_shape=jax.ShapeDtypeStruct((bsz, 3 * D_MODEL), F32),
        name="ada_mod",
    )(c, w_ada, b_ada[None, :])


def _bias_kernel(rb_ref, bucket_ref, o_ref):
    for g in range(N_GROUPS):
        bk = bucket_ref[g]
        for j in range(HEADS_PER_GROUP):
            h = g * HEADS_PER_GROUP + j
            acc = jnp.full(bk.shape, NEG_INF, F32)
            for k in range(N_BUCKETS):
                acc = jnp.where(bk == k, rb_ref[k, h], acc)
            o_ref[h, 0] = acc
            o_ref[h, 1] = jnp.concatenate(
                [acc[:, BLOCK:], jnp.full((BLOCK, BLOCK), NEG_INF, F32)], axis=1)


def _bias_call(rel_bias):
    return pl.pallas_call(
        _bias_kernel,
        out_shape=jax.ShapeDtypeStruct((N_ATTN_HEADS, 2, BLOCK, 2 * BLOCK), F32),
        in_specs=[pl.BlockSpec(memory_space=pltpu.SMEM), pl.BlockSpec(memory_space=pltpu.VMEM)],
        out_specs=pl.BlockSpec(memory_space=pltpu.VMEM),
        name="bias_tables",
    )(rel_bias, jnp.asarray(_BUCKETS))


def _inproj_kernel(mod_ref, x_ref, w_ref, wco_ref, cw_ref, cb_ref,
                   qkv1_ref, qkv2_ref, qkv3_ref, sg_ref, sa_ref, ms_ref,
                   hs_ref, zbuf_ref, spre_ref, *, tm):
    i = pl.program_id(1)
    shift = mod_ref[0, :, 0:D_MODEL]
    scale1 = 1.0 + mod_ref[0, :, D_MODEL:2 * D_MODEL]
    h = x_ref[0] * scale1 + shift
    h_n = h.astype(BF16)

    n_slab = D_MODEL // HEAD_DIM
    for s in range(n_slab):
        hs_ref[s] = h[:, s * HEAD_DIM:(s + 1) * HEAD_DIM]

    def by_residue(d):
        return jnp.concatenate(
            [jnp.concatenate([hs_ref[s, pl.ds(r, tm // d, stride=d), :] for s in range(n_slab)], axis=1)
             for r in range(d)], axis=0).astype(BF16)

    def proj(hv, c0, width):
        return jnp.dot(hv, w_ref[:, c0:c0 + width], preferred_element_type=F32)

    def qkv(hv, g, store):
        for t, col in enumerate((COL_Q, COL_K, COL_V)):
            r = proj(hv, col + g * ATTN_WIDTH, ATTN_WIDTH)
            if t == 0:
                r = r * Q_SCALE
            r = r.astype(BF16)
            for j in range(HEADS_PER_GROUP):
                store(t, j, r[:, j * HEAD_DIM:(j + 1) * HEAD_DIM])

    def store1(t, j, v):
        qkv1_ref[0, t, j] = v
    qkv(h_n, 0, store1)

    def store_sub(ref, d):
        def store(t, j, v):
            rows = tm // d
            for r in range(d):
                ref[0, t, j, r] = v[r * rows:(r + 1) * rows]
        return store
    d4, d16 = DILATED_GROUPS[1][1], DILATED_GROUPS[2][1]
    qkv(by_residue(d4), 1, store_sub(qkv2_ref, d4))
    qkv(by_residue(d16), 2, store_sub(qkv3_ref, d16))

    ga = proj(h_n, COL_GATTN, ATTN_WIDTH)
    sg_ref[0] = (ga * _sigmoid(ga)).astype(BF16)

    for c0 in range(0, D_MODEL, 2 * CONV_CHUNK):
        ma = proj(h_n, COL_MATTN + c0, 2 * CONV_CHUNK)
        sa_ref[0, :, c0:c0 + 2 * CONV_CHUNK] = _sigmoid(ma).astype(BF16)

    @pl.when(i == 0)
    def _():
        zbuf_ref[0:SUBLANES, :] = jnp.zeros((SUBLANES, CONV_WIDTH), F32)

    for c0 in range(0, CONV_WIDTH, CONV_CHUNK):
        cs = slice(c0, c0 + CONV_CHUNK)
        u = proj(h_n, COL_U + c0, CONV_CHUNK)
        cg = proj(h_n, COL_C + c0, CONV_CHUNK)
        z = cg * u
        zbuf_ref[SUBLANES:SUBLANES + tm, cs] = z
        zm1 = zbuf_ref[SUBLANES - 1:SUBLANES - 1 + tm, cs]
        zm2 = zbuf_ref[SUBLANES - 2:SUBLANES - 2 + tm, cs]
        y = cw_ref[0:1, cs] * zm2 + cw_ref[1:2, cs] * zm1 + cw_ref[2:3, cs] * z + cb_ref[:, cs]
        bg = proj(h_n, COL_B + c0, CONV_CHUNK)
        gc = proj(h_n, COL_GCONV + c0, CONV_CHUNK)
        spre_ref[:, cs] = (bg * y * (gc * _sigmoid(gc))).astype(BF16)
    zbuf_ref[0:SUBLANES, :] = zbuf_ref[tm:tm + SUBLANES, :]

    spre = spre_ref[...]
    for c0 in range(0, D_MODEL, CONV_CHUNK):
        cs = slice(c0, c0 + CONV_CHUNK)
        so = jnp.dot(spre, wco_ref[:, cs], preferred_element_type=F32)
        mc = proj(h_n, COL_MCONV + c0, CONV_CHUNK)
        ms_ref[0, :, cs] = (_sigmoid(mc) * so).astype(BF16)


def _inproj_call(mod, x, w_in, w_conv_out, conv_w, conv_b):
    bsz, seq, _ = x.shape
    tm = INPROJ_TM
    n_t = seq // tm
    d4, d16 = DILATED_GROUPS[1][1], DILATED_GROUPS[2][1]
    sub4, sub16 = seq // d4, seq // d16
    assert seq % tm == 0 and tm % (d16 * 16) == 0
    hg = HEADS_PER_GROUP

    in_specs = [
        pl.BlockSpec((1, 1, 3 * D_MODEL), lambda b, i: (b, 0, 0)),
        pl.BlockSpec((1, tm, D_MODEL), lambda b, i: (b, i, 0)),
        pl.BlockSpec((D_MODEL, N_COLS), lambda b, i: (0, 0), pipeline_mode=pl.Buffered(1)),
        pl.BlockSpec((CONV_WIDTH, D_MODEL), lambda b, i: (0, 0), pipeline_mode=pl.Buffered(1)),
        pl.BlockSpec((CONV_K, CONV_WIDTH), lambda b, i: (0, 0)),
        pl.BlockSpec((1, CONV_WIDTH), lambda b, i: (0, 0)),
    ]
    out_shape = [
        jax.ShapeDtypeStruct((bsz, 3, hg, seq, HEAD_DIM), BF16),
        jax.ShapeDtypeStruct((bsz, 3, hg, d4, sub4, HEAD_DIM), BF16),
        jax.ShapeDtypeStruct((bsz, 3, hg, d16, sub16, HEAD_DIM), BF16),
        jax.ShapeDtypeStruct((bsz, seq, ATTN_WIDTH), BF16),
        jax.ShapeDtypeStruct((bsz, seq, D_MODEL), BF16),
        jax.ShapeDtypeStruct((bsz, seq, D_MODEL), BF16),
    ]
    out_specs = [
        pl.BlockSpec((1, 3, hg, tm, HEAD_DIM), lambda b, i: (b, 0, 0, i, 0)),
        pl.BlockSpec((1, 3, hg, d4, tm // d4, HEAD_DIM), lambda b, i: (b, 0, 0, 0, i, 0)),
        pl.BlockSpec((1, 3, hg, d16, tm // d16, HEAD_DIM), lambda b, i: (b, 0, 0, 0, i, 0)),
        pl.BlockSpec((1, tm, ATTN_WIDTH), lambda b, i: (b, i, 0)),
        pl.BlockSpec((1, tm, D_MODEL), lambda b, i: (b, i, 0)),
        pl.BlockSpec((1, tm, D_MODEL), lambda b, i: (b, i, 0)),
    ]
    weights = D_MODEL * N_COLS * 2 + CONV_WIDTH * D_MODEL * 2
    x_tiles = tm * D_MODEL * 4 * 2
    out_tiles = (3 * 3 * ATTN_WIDTH + ATTN_WIDTH + 2 * D_MODEL) * tm * 2 * 2
    scratch = tm * D_MODEL * 4 + (tm + SUBLANES) * CONV_WIDTH * 4 + tm * CONV_WIDTH * 2
    return pl.pallas_call(
        functools.partial(_inproj_kernel, tm=tm),
        grid=(bsz, n_t),
        in_specs=in_specs,
        out_specs=out_specs,
        out_shape=out_shape,
        scratch_shapes=[pltpu.VMEM((D_MODEL // HEAD_DIM, tm, HEAD_DIM), F32),
                        pltpu.VMEM((tm + SUBLANES, CONV_WIDTH), F32),
                        pltpu.VMEM((tm, CONV_WIDTH), BF16)],
        compiler_params=pltpu.CompilerParams(
            dimension_semantics=("arbitrary", "arbitrary"),
            vmem_limit_bytes=_vmem_limit(weights + x_tiles + out_tiles + scratch)),
        name="in_proj",
    )(mod, x, w_in, w_conv_out, conv_w, conv_b)


def _nt_dot(q, k):
    return lax.dot_general(q, k, (((1,), (1,)), ((), ())), preferred_element_type=F32)


def _softmax_pv(s, v):
    m = jnp.max(s, axis=1, keepdims=True)
    p = jnp.exp(s - m)
    l = jnp.sum(p, axis=1, keepdims=True)
    acc = jnp.dot(p.astype(BF16), v, preferred_element_type=F32)
    return acc * (1.0 / l), m + jnp.log(l)


def _attn_kernel(qkv1_ref, qkv2_ref, qkv3_ref, bias_ref, o_ref,
                 sa1_ref, sa2_ref, sa3_ref, sb1_ref, sb2_ref, sb3_ref,
                 o1_ref, o2_ref, o3_ref, l1_ref, l2_ref, l3_ref, *, seq):
    d4, d16 = DILATED_GROUPS[1][1], DILATED_GROUPS[2][1]
    n_it = seq // BLOCK
    blk4 = seq // d4 // BLOCK
    assert seq // d16 == BLOCK and blk4 & (blk4 - 1) == 0 and n_it % 2 == 0
    shift4 = blk4.bit_length() - 1

    def rows(start, size):
        if isinstance(start, int):
            return pl.ds(start, size)
        return pl.ds(pl.multiple_of(start, BLOCK), size)

    def plan(it):
        if isinstance(it, int):
            f1, nb, r4 = int(it == 0), it & (blk4 - 1), it >> shift4
            f2 = int(nb == 0)
        else:
            f1 = (it == 0).astype(jnp.int32)
            nb = jnp.bitwise_and(it, blk4 - 1)
            f2 = (nb == 0).astype(jnp.int32)
            r4 = lax.shift_right_logical(it, shift4)
        return dict(q=it * BLOCK, k1=(it - 1 + f1) * BLOCK, f1=f1, k2=(it - 1 + f2) * BLOCK, f2=f2,
                    out2=nb * (BLOCK * d4) + r4, out3=it)

    def scores(it, s_refs):
        p = plan(it)
        q = rows(p["q"], BLOCK)
        s_refs[0][...] = _nt_dot(qkv1_ref[0, 0, 0, q, :], qkv1_ref[0, 1, 0, rows(p["k1"], 2 * BLOCK), :]) \
            + bias_ref[0, 0, p["f1"]]
        s_refs[1][...] = _nt_dot(qkv2_ref[0, 0, 0, q, :], qkv2_ref[0, 1, 0, rows(p["k2"], 2 * BLOCK), :]) \
            + bias_ref[1, 0, p["f2"]]
        s_refs[2][...] = _nt_dot(qkv3_ref[0, 0, 0, q, :], qkv3_ref[0, 1, 0, q, :]) \
            + bias_ref[2, 0, 1, :, 0:BLOCK]

    def put(o_dst, l_dst, idx, o, lse):
        o_dst[idx, :] = o
        l_dst[idx, :] = jnp.broadcast_to(lse, (BLOCK, HEAD_DIM))

    def finish(it, s_refs):
        p = plan(it)
        q = rows(p["q"], BLOCK)
        o, lse = _softmax_pv(s_refs[0][...], qkv1_ref[0, 2, 0, rows(p["k1"], 2 * BLOCK), :])
        put(o1_ref, l1_ref, q, o, lse)
        o, lse = _softmax_pv(s_refs[1][...], qkv2_ref[0, 2, 0, rows(p["k2"], 2 * BLOCK), :])
        put(o2_ref, l2_ref, pl.ds(p["out2"], BLOCK, stride=d4), o, lse)
        o, lse = _softmax_pv(s_refs[2][...], qkv3_ref[0, 2, 0, q, :])
        put(o3_ref, l3_ref, pl.ds(p["out3"], BLOCK, stride=d16), o, lse)

    s_a = (sa1_ref, sa2_ref, sa3_ref)
    s_b = (sb1_ref, sb2_ref, sb3_ref)
    scores(0, s_a)

    def body(i, carry):
        scores(2 * i + 1, s_b)
        finish(2 * i, s_a)
        scores(2 * i + 2, s_a)
        finish(2 * i + 1, s_b)
        return carry
    lax.fori_loop(0, n_it // 2 - 1, body, 0)
    scores(n_it - 1, s_b)
    finish(n_it - 2, s_a)
    finish(n_it - 1, s_b)

    chunk = 2 * BLOCK
    for c0 in range(0, seq, chunk):
        rws = slice(c0, c0 + chunk)
        la, lb, lc = l1_ref[rws, :], l2_ref[rws, :], l3_ref[rws, :]
        m = jnp.maximum(jnp.maximum(la, lb), lc)
        wa, wb, wc = jnp.exp(la - m), jnp.exp(lb - m), jnp.exp(lc - m)
        num = wa * o1_ref[rws, :] + wb * o2_ref[rws, :] + wc * o3_ref[rws, :]
        o_ref[0, 0, rws, :] = (num * (1.0 / (wa + wb + wc))).astype(BF16)


def _attn_call(qkv1, qkv2, qkv3, bias):
    bsz, _, hg, seq, _ = qkv1.shape
    qkv2 = qkv2.reshape(qkv1.shape)
    qkv3 = qkv3.reshape(qkv1.shape)
    bias = bias.reshape(N_GROUPS, hg, 2, BLOCK, 2 * BLOCK)
    in_tiles = 3 * 3 * seq * HEAD_DIM * 2 * 2 + N_GROUPS * 2 * BLOCK * 2 * BLOCK * 4 * 2
    out_tiles = seq * HEAD_DIM * 2 * 2
    score_shapes = [(BLOCK, 2 * BLOCK), (BLOCK, 2 * BLOCK), (BLOCK, BLOCK)] * 2
    scratch = 6 * seq * HEAD_DIM * 4 + sum(a * b * 4 for a, b in score_shapes)
    qkv_spec = pl.BlockSpec((1, 3, 1, seq, HEAD_DIM), lambda b, j: (b, 0, j, 0, 0))
    return pl.pallas_call(
        functools.partial(_attn_kernel, seq=seq),
        grid=(bsz, hg),
        in_specs=[
            qkv_spec, qkv_spec, qkv_spec,
            pl.BlockSpec((N_GROUPS, 1, 2, BLOCK, 2 * BLOCK), lambda b, j: (0, j, 0, 0, 0)),
        ],
        out_specs=pl.BlockSpec((1, 1, seq, HEAD_DIM), lambda b, j: (b, j, 0, 0)),
        out_shape=jax.ShapeDtypeStruct((bsz, hg, seq, HEAD_DIM), BF16),
        scratch_shapes=[pltpu.VMEM(s, F32) for s in score_shapes]
        + [pltpu.VMEM((seq, HEAD_DIM), F32) for _ in range(6)],
        compiler_params=pltpu.CompilerParams(
            dimension_semantics=("arbitrary", "arbitrary"),
            vmem_limit_bytes=_vmem_limit(in_tiles + out_tiles + scratch)),
        name="dilated_attn",
    )(qkv1, qkv2, qkv3, bias)


def _outproj_kernel(mod_ref, o_ref, sg_ref, sa_ref, ms_ref, x_ref, wao_ref, wo_ref,
                    lng_ref, lnb_ref, out_ref, *, alpha):
    gate1 = 1.0 + mod_ref[0, :, 2 * D_MODEL:3 * D_MODEL]
    o = jnp.concatenate([o_ref[0, j] for j in range(HEADS_PER_GROUP)], axis=1)
    a_in = (o.astype(F32) * sg_ref[0].astype(F32)).astype(BF16)
    a_out = jnp.dot(a_in, wao_ref[...], preferred_element_type=F32)
    merged = (sa_ref[0].astype(F32) * a_out + ms_ref[0].astype(F32)).astype(BF16)
    y = jnp.dot(merged, wo_ref[...], preferred_element_type=F32)
    v = alpha * x_ref[0] + gate1 * y
    mu = jnp.mean(v, axis=-1, keepdims=True)
    dv = v - mu
    var = jnp.mean(dv * dv, axis=-1, keepdims=True)
    out_ref[0] = dv * lax.rsqrt(var + LN_EPS) * lng_ref[...] + lnb_ref[...]


def _outproj_call(mod, o, sg, sa, ms, x, w_attn_out, w_o, ln_g, ln_b, alpha):
    bsz, seq, _ = x.shape
    tm = OUTPROJ_TM
    assert seq % tm == 0
    hg = HEADS_PER_GROUP
    tok = lambda b, i: (b, i, 0)
    const = lambda b, i: (0, 0)
    tiles = (2 * ATTN_WIDTH * 2 + 2 * D_MODEL * 2 + 2 * D_MODEL * 4) * tm * 2
    weights = (ATTN_WIDTH + D_MODEL) * D_MODEL * 2 * 2
    temps = 4 * tm * D_MODEL * 4
    return pl.pallas_call(
        functools.partial(_outproj_kernel, alpha=alpha),
        grid=(bsz, seq // tm),
        in_specs=[
            pl.BlockSpec((1, 1, 3 * D_MODEL), lambda b, i: (b, 0, 0)),
            pl.BlockSpec((1, hg, tm, HEAD_DIM), lambda b, i: (b, 0, i, 0)),
            pl.BlockSpec((1, tm, ATTN_WIDTH), tok),
            pl.BlockSpec((1, tm, D_MODEL), tok),
            pl.BlockSpec((1, tm, D_MODEL), tok),
            pl.BlockSpec((1, tm, D_MODEL), tok),
            pl.BlockSpec((ATTN_WIDTH, D_MODEL), const),
            pl.BlockSpec((D_MODEL, D_MODEL), const),
            pl.BlockSpec((1, D_MODEL), const),
            pl.BlockSpec((1, D_MODEL), const),
        ],
        out_specs=pl.BlockSpec((1, tm, D_MODEL), tok),
        out_shape=jax.ShapeDtypeStruct((bsz, seq, D_MODEL), F32),
        compiler_params=pltpu.CompilerParams(
            dimension_semantics=("arbitrary", "arbitrary"),
            vmem_limit_bytes=_vmem_limit(tiles + weights + temps)),
        name="out_proj",
    )(mod, o, sg, sa, ms, x, w_attn_out, w_o, ln_g, ln_b)


def kernel(x, c, w_ada, b_ada, w_in, conv_w, conv_b, rel_bias, w_attn_out, w_conv_out, w_o, ln_g, ln_b):
    bsz = x.shape[0]
    depth = w_in.shape[0]
    alpha = (2.0 * depth) ** 0.25
    bias = _bias_call(rel_bias)
    for layer in range(depth):
        mod = _ada_call(c, w_ada[layer], b_ada[layer]).reshape(bsz, 1, 3 * D_MODEL)
        qkv1, qkv2, qkv3, sg, sa, ms = _inproj_call(
            mod, x, w_in[layer].astype(BF16), w_conv_out[layer].astype(BF16),
            conv_w[layer], conv_b[layer][None, :])
        o = _attn_call(qkv1, qkv2, qkv3, bias)
        x = _outproj_call(mod, o, sg, sa, ms, x, w_attn_out[layer].astype(BF16),
                          w_o[layer].astype(BF16), ln_g[layer][None, :], ln_b[layer][None, :], alpha)
    return x
```

```python
import functools
import math

import numpy as np
import jax
import jax.numpy as jnp
from jax import lax
from jax.experimental import pallas as pl
from jax.experimental.pallas import tpu as pltpu

F32 = jnp.float32
BF16 = jnp.bfloat16

D_MODEL = 1024
HEAD_DIM = 128
HEADS_PER_GROUP = 4
DILATED_GROUPS = ((128, 1), (512, 4), (2048, 16))
N_GROUPS = len(DILATED_GROUPS)
N_ATTN_HEADS = N_GROUPS * HEADS_PER_GROUP
QKV_WIDTH = N_ATTN_HEADS * HEAD_DIM
ATTN_WIDTH = HEADS_PER_GROUP * HEAD_DIM
CONV_WIDTH = D_MODEL
CONV_K = 3
N_BUCKETS = 32
MAX_EXACT = 16
MAX_DISTANCE = 2048
BLOCK = 128
LN_EPS = 1e-5
NEG_INF = -1e30
Q_SCALE = HEAD_DIM ** -0.5

COL_Q = 0
COL_K = QKV_WIDTH
COL_V = 2 * QKV_WIDTH
COL_GATTN = 3 * QKV_WIDTH
COL_U = COL_GATTN + ATTN_WIDTH
COL_B = COL_U + CONV_WIDTH
COL_C = COL_B + CONV_WIDTH
COL_GCONV = COL_C + CONV_WIDTH
COL_MATTN = COL_GCONV + CONV_WIDTH
COL_MCONV = COL_MATTN + D_MODEL
N_COLS = COL_MCONV + D_MODEL

V7X_VMEM_BYTES = 64 * 1024 * 1024
SUBLANES = 8
INPROJ_TM = 512
OUTPROJ_TM = 512
CONV_CHUNK = 256
ATTN_BLOCKS_PER_STAGE = 2


def _vmem_limit(estimate_bytes):
    return int(min(V7X_VMEM_BYTES - (4 << 20), estimate_bytes * 5 // 4 + (4 << 20)))


def _bucket_table():
    a = np.arange(BLOCK)[:, None]
    b = np.arange(2 * BLOCK)[None, :]
    steps = a + BLOCK - b
    out = []
    for window, dilation in DILATED_GROUPS:
        n_steps = window // dilation
        valid = (steps >= 0) & (steps <= n_steps)
        dist = np.maximum(steps, 0) * dilation
        n = np.maximum(dist, 1).astype(np.float32)
        large = MAX_EXACT + (np.log(n / np.float32(MAX_EXACT)) / np.float32(math.log(MAX_DISTANCE / MAX_EXACT))
                             * np.float32(N_BUCKETS - MAX_EXACT)).astype(np.int32)
        large = np.minimum(large, N_BUCKETS - 1)
        bucket = np.where(dist < MAX_EXACT, dist, large)
        out.append(np.where(valid, bucket, -1).astype(np.int32))
    return np.stack(out)


_BUCKETS = _bucket_table()


def _sigmoid(v):
    return 1.0 / (1.0 + jnp.exp(-v))


def _ada_kernel(c_ref, w_ref, b_ref, o_ref):
    c = c_ref[...]
    o_ref[...] = jnp.dot(c * _sigmoid(c), w_ref[...], preferred_element_type=F32) + b_ref[...]


def _ada_call(c, w_ada, b_ada):
    bsz = c.shape[0]
    return pl.pallas_call(
        _ada_kernel,
        out_shape=jax.ShapeDtypeStruct((bsz, 3 * D_MODEL), F32),
        name="ada_mod",
    )(c, w_ada, b_ada[None, :])


def _bias_kernel(rb_ref, bucket_ref, o_ref):
    for g in range(N_GROUPS):
        bk = bucket_ref[g]
        for j in range(HEADS_PER_GROUP):
            h = g * HEADS_PER_GROUP + j
            acc = jnp.full(bk.shape, NEG_INF, F32)
            for k in range(N_BUCKETS):
                acc = jnp.where(bk == k, rb_ref[k, h], acc)
            o_ref[h, 0] = acc
            o_ref[h, 1] = jnp.concatenate(
                [acc[:, BLOCK:], jnp.full((BLOCK, BLOCK), NEG_INF, F32)], axis=1)


def _bias_call(rel_bias):
    return pl.pallas_call(
        _bias_kernel,
        out_shape=jax.ShapeDtypeStruct((N_ATTN_HEADS, 2, BLOCK, 2 * BLOCK), F32),
        in_specs=[pl.BlockSpec(memory_space=pltpu.SMEM), pl.BlockSpec(memory_space=pltpu.VMEM)],
        out_specs=pl.BlockSpec(memory_space=pltpu.VMEM),
        name="bias_tables",
    )(rel_bias, jnp.asarray(_BUCKETS))


def _inproj_kernel(mod_ref, x_ref, w_ref, wco_ref, cw_ref, cb_ref,
                   qkv1_ref, qkv2_ref, qkv3_ref, sg_ref, sa_ref, ms_ref,
                   hs_ref, zbuf_ref, spre_ref, *, tm):
    i = pl.program_id(1)
    shift = mod_ref[0, :, 0:D_MODEL]
    scale1 = 1.0 + mod_ref[0, :, D_MODEL:2 * D_MODEL]
    h = x_ref[0] * scale1 + shift
    h_n = h.astype(BF16)

    n_slab = D_MODEL // HEAD_DIM
    for s in range(n_slab):
        hs_ref[s] = h[:, s * HEAD_DIM:(s + 1) * HEAD_DIM]

    def by_residue(d):
        return jnp.concatenate(
            [jnp.concatenate([hs_ref[s, pl.ds(r, tm // d, stride=d), :] for s in range(n_slab)], axis=1)
             for r in range(d)], axis=0).astype(BF16)

    def proj(hv, c0, width):
        return jnp.dot(hv, w_ref[:, c0:c0 + width], preferred_element_type=F32)

    def qkv(hv, g, store):
        for t, col in enumerate((COL_Q, COL_K, COL_V)):
            r = proj(hv, col + g * ATTN_WIDTH, ATTN_WIDTH)
            if t == 0:
                r = r * Q_SCALE
            r = r.astype(BF16)
            for j in range(HEADS_PER_GROUP):
                store(t, j, r[:, j * HEAD_DIM:(j + 1) * HEAD_DIM])

    def store1(t, j, v):
        qkv1_ref[0, t, j] = v
    qkv(h_n, 0, store1)

    def store_sub(ref, d):
        def store(t, j, v):
            rows = tm // d
            for r in range(d):
                ref[0, t, j, r] = v[r * rows:(r + 1) * rows]
        return store
    d4, d16 = DILATED_GROUPS[1][1], DILATED_GROUPS[2][1]
    qkv(by_residue(d4), 1, store_sub(qkv2_ref, d4))
    qkv(by_residue(d16), 2, store_sub(qkv3_ref, d16))

    ga = proj(h_n, COL_GATTN, ATTN_WIDTH)
    sg_ref[0] = (ga * _sigmoid(ga)).astype(BF16)

    for c0 in range(0, D_MODEL, 2 * CONV_CHUNK):
        ma = proj(h_n, COL_MATTN + c0, 2 * CONV_CHUNK)
        sa_ref[0, :, c0:c0 + 2 * CONV_CHUNK] = _sigmoid(ma).astype(BF16)

    @pl.when(i == 0)
    def _():
        zbuf_ref[0:SUBLANES, :] = jnp.zeros((SUBLANES, CONV_WIDTH), F32)

    for c0 in range(0, CONV_WIDTH, CONV_CHUNK):
        cs = slice(c0, c0 + CONV_CHUNK)
        u = proj(h_n, COL_U + c0, CONV_CHUNK)
        cg = proj(h_n, COL_C + c0, CONV_CHUNK)
        z = cg * u
        zbuf_ref[SUBLANES:SUBLANES + tm, cs] = z
        zm1 = zbuf_ref[SUBLANES - 1:SUBLANES - 1 + tm, cs]
        zm2 = zbuf_ref[SUBLANES - 2:SUBLANES - 2 + tm, cs]
        y = cw_ref[0:1, cs] * zm2 + cw_ref[1:2, cs] * zm1 + cw_ref[2:3, cs] * z + cb_ref[:, cs]
        bg = proj(h_n, COL_B + c0, CONV_CHUNK)
        gc = proj(h_n, COL_GCONV + c0, CONV_CHUNK)
        spre_ref[:, cs] = (bg * y * (gc * _sigmoid(gc))).astype(BF16)
    zbuf_ref[0:SUBLANES, :] = zbuf_ref[tm:tm + SUBLANES, :]

    spre = spre_ref[...]
    for c0 in range(0, D_MODEL, CONV_CHUNK):
        cs = slice(c0, c0 + CONV_CHUNK)
        so = jnp.dot(spre, wco_ref[:, cs], preferred_element_type=F32)
        mc = proj(h_n, COL_MCONV + c0, CONV_CHUNK)
        ms_ref[0, :, cs] = (_sigmoid(mc) * so).astype(BF16)


def _inproj_call(mod, x, w_in, w_conv_out, conv_w, conv_b):
    bsz, seq, _ = x.shape
    tm = INPROJ_TM
    n_t = seq // tm
    d4, d16 = DILATED_GROUPS[1][1], DILATED_GROUPS[2][1]
    sub4, sub16 = seq // d4, seq // d16
    assert seq % tm == 0 and tm % (d16 * 16) == 0
    hg = HEADS_PER_GROUP

    in_specs = [
        pl.BlockSpec((1, 1, 3 * D_MODEL), lambda b, i: (b, 0, 0)),
        pl.BlockSpec((1, tm, D_MODEL), lambda b, i: (b, i, 0)),
        pl.BlockSpec((D_MODEL, N_COLS), lambda b, i: (0, 0), pipeline_mode=pl.Buffered(1)),
        pl.BlockSpec((CONV_WIDTH, D_MODEL), lambda b, i: (0, 0), pipeline_mode=pl.Buffered(1)),
        pl.BlockSpec((CONV_K, CONV_WIDTH), lambda b, i: (0, 0)),
        pl.BlockSpec((1, CONV_WIDTH), lambda b, i: (0, 0)),
    ]
    out_shape = [
        jax.ShapeDtypeStruct((bsz, 3, hg, seq, HEAD_DIM), BF16),
        jax.ShapeDtypeStruct((bsz, 3, hg, d4, sub4, HEAD_DIM), BF16),
        jax.ShapeDtypeStruct((bsz, 3, hg, d16, sub16, HEAD_DIM), BF16),
        jax.ShapeDtypeStruct((bsz, seq, ATTN_WIDTH), BF16),
        jax.ShapeDtypeStruct((bsz, seq, D_MODEL), BF16),
        jax.ShapeDtypeStruct((bsz, seq, D_MODEL), BF16),
    ]
    out_specs = [
        pl.BlockSpec((1, 3, hg, tm, HEAD_DIM), lambda b, i: (b, 0, 0, i, 0)),
        pl.BlockSpec((1, 3, hg, d4, tm // d4, HEAD_DIM), lambda b, i: (b, 0, 0, 0, i, 0)),
        pl.BlockSpec((1, 3, hg, d16, tm // d16, HEAD_DIM), lambda b, i: (b, 0, 0, 0, i, 0)),
        pl.BlockSpec((1, tm, ATTN_WIDTH), lambda b, i: (b, i, 0)),
        pl.BlockSpec((1, tm, D_MODEL), lambda b, i: (b, i, 0)),
        pl.BlockSpec((1, tm, D_MODEL), lambda b, i: (b, i, 0)),
    ]
    weights = D_MODEL * N_COLS * 2 + CONV_WIDTH * D_MODEL * 2
    x_tiles = tm * D_MODEL * 4 * 2
    out_tiles = (3 * 3 * ATTN_WIDTH + ATTN_WIDTH + 2 * D_MODEL) * tm * 2 * 2
    scratch = tm * D_MODEL * 4 + (tm + SUBLANES) * CONV_WIDTH * 4 + tm * CONV_WIDTH * 2
    return pl.pallas_call(
        functools.partial(_inproj_kernel, tm=tm),
        grid=(bsz, n_t),
        in_specs=in_specs,
        out_specs=out_specs,
        out_shape=out_shape,
        scratch_shapes=[pltpu.VMEM((D_MODEL // HEAD_DIM, tm, HEAD_DIM), F32),
                        pltpu.VMEM((tm + SUBLANES, CONV_WIDTH), F32),
                        pltpu.VMEM((tm, CONV_WIDTH), BF16)],
        compiler_params=pltpu.CompilerParams(
            dimension_semantics=("arbitrary", "arbitrary"),
            vmem_limit_bytes=_vmem_limit(weights + x_tiles + out_tiles + scratch)),
        name="in_proj",
    )(mod, x, w_in, w_conv_out, conv_w, conv_b)


def _nt_dot(q, k):
    return lax.dot_general(q, k, (((1,), (1,)), ((), ())), preferred_element_type=F32)


def _softmax_pv(s, v):
    m = jnp.max(s, axis=1, keepdims=True)
    p = jnp.exp(s - m)
    l = jnp.sum(p, axis=1, keepdims=True)
    acc = jnp.dot(p.astype(BF16), v, preferred_element_type=F32)
    return acc * (1.0 / l), m + jnp.log(l)


def _attn_kernel(qkv1_ref, qkv2_ref, qkv3_ref, bias_ref, o_ref,
                 sa1_ref, sa2_ref, sa3_ref, sb1_ref, sb2_ref, sb3_ref,
                 o1_ref, o2_ref, o3_ref, l1_ref, l2_ref, l3_ref, *, seq):
    d4, d16 = DILATED_GROUPS[1][1], DILATED_GROUPS[2][1]
    per = ATTN_BLOCKS_PER_STAGE
    n_it = seq // BLOCK // per
    blk4 = seq // d4 // BLOCK
    assert seq // d16 == BLOCK and blk4 & (blk4 - 1) == 0 and n_it % 2 == 0 and n_it * per * BLOCK == seq
    shift4 = blk4.bit_length() - 1

    def rows(start, size):
        if isinstance(start, int):
            return pl.ds(start, size)
        return pl.ds(pl.multiple_of(start, BLOCK), size)

    def plan(it, u):
        it = it * per + u
        if isinstance(it, int):
            f1, nb, r4 = int(it == 0), it & (blk4 - 1), it >> shift4
            f2 = int(nb == 0)
        else:
            f1 = (it == 0).astype(jnp.int32)
            nb = jnp.bitwise_and(it, blk4 - 1)
            f2 = (nb == 0).astype(jnp.int32)
            r4 = lax.shift_right_logical(it, shift4)
        return dict(q=it * BLOCK, k1=(it - 1 + f1) * BLOCK, f1=f1, k2=(it - 1 + f2) * BLOCK, f2=f2,
                    out2=nb * (BLOCK * d4) + r4, out3=it)

    def scores(it, s_refs):
        for u in range(per):
            p = plan(it, u)
            q = rows(p["q"], BLOCK)
            s_refs[0][u] = _nt_dot(qkv1_ref[0, 0, 0, q, :], qkv1_ref[0, 1, 0, rows(p["k1"], 2 * BLOCK), :]) \
                + bias_ref[0, 0, p["f1"]]
            s_refs[1][u] = _nt_dot(qkv2_ref[0, 0, 0, q, :], qkv2_ref[0, 1, 0, rows(p["k2"], 2 * BLOCK), :]) \
                + bias_ref[1, 0, p["f2"]]
            s_refs[2][u] = _nt_dot(qkv3_ref[0, 0, 0, q, :], qkv3_ref[0, 1, 0, q, :]) \
                + bias_ref[2, 0, 1, :, 0:BLOCK]

    def put(o_dst, l_dst, idx, o, lse):
        o_dst[idx, :] = o
        l_dst[idx, :] = jnp.broadcast_to(lse, (BLOCK, HEAD_DIM))

    def finish(it, s_refs):
        for u in range(per):
            p = plan(it, u)
            q = rows(p["q"], BLOCK)
            o, lse = _softmax_pv(s_refs[0][u], qkv1_ref[0, 2, 0, rows(p["k1"], 2 * BLOCK), :])
            put(o1_ref, l1_ref, q, o, lse)
            o, lse = _softmax_pv(s_refs[1][u], qkv2_ref[0, 2, 0, rows(p["k2"], 2 * BLOCK), :])
            put(o2_ref, l2_ref, pl.ds(p["out2"], BLOCK, stride=d4), o, lse)
            o, lse = _softmax_pv(s_refs[2][u], qkv3_ref[0, 2, 0, q, :])
            put(o3_ref, l3_ref, pl.ds(p["out3"], BLOCK, stride=d16), o, lse)

    s_a = (sa1_ref, sa2_ref, sa3_ref)
    s_b = (sb1_ref, sb2_ref, sb3_ref)
    scores(0, s_a)

    def body(i, carry):
        scores(2 * i + 1, s_b)
        finish(2 * i, s_a)
        scores(2 * i + 2, s_a)
        finish(2 * i + 1, s_b)
        return carry
    lax.fori_loop(0, n_it // 2 - 1, body, 0)
    scores(n_it - 1, s_b)
    finish(n_it - 2, s_a)
    finish(n_it - 1, s_b)

    chunk = 2 * BLOCK
    for c0 in range(0, seq, chunk):
        rws = slice(c0, c0 + chunk)
        la, lb, lc = l1_ref[rws, :], l2_ref[rws, :], l3_ref[rws, :]
        m = jnp.maximum(jnp.maximum(la, lb), lc)
        wa, wb, wc = jnp.exp(la - m), jnp.exp(lb - m), jnp.exp(lc - m)
        num = wa * o1_ref[rws, :] + wb * o2_ref[rws, :] + wc * o3_ref[rws, :]
        o_ref[0, 0, rws, :] = (num * (1.0 / (wa + wb + wc))).astype(BF16)


def _attn_call(qkv1, qkv2, qkv3, bias):
    bsz, _, hg, seq, _ = qkv1.shape
    qkv2 = qkv2.reshape(qkv1.shape)
    qkv3 = qkv3.reshape(qkv1.shape)
    bias = bias.reshape(N_GROUPS, hg, 2, BLOCK, 2 * BLOCK)
    in_tiles = 3 * 3 * seq * HEAD_DIM * 2 * 2 + N_GROUPS * 2 * BLOCK * 2 * BLOCK * 4 * 2
    out_tiles = seq * HEAD_DIM * 2 * 2
    per = ATTN_BLOCKS_PER_STAGE
    score_shapes = [(per, BLOCK, 2 * BLOCK), (per, BLOCK, 2 * BLOCK), (per, BLOCK, BLOCK)] * 2
    scratch = 6 * seq * HEAD_DIM * 4 + sum(p * a * b * 4 for p, a, b in score_shapes)
    qkv_spec = pl.BlockSpec((1, 3, 1, seq, HEAD_DIM), lambda b, j: (b, 0, j, 0, 0))
    return pl.pallas_call(
        functools.partial(_attn_kernel, seq=seq),
        grid=(bsz, hg),
        in_specs=[
            qkv_spec, qkv_spec, qkv_spec,
            pl.BlockSpec((N_GROUPS, 1, 2, BLOCK, 2 * BLOCK), lambda b, j: (0, j, 0, 0, 0)),
        ],
        out_specs=pl.BlockSpec((1, 1, seq, HEAD_DIM), lambda b, j: (b, j, 0, 0)),
        out_shape=jax.ShapeDtypeStruct((bsz, hg, seq, HEAD_DIM), BF16),
        scratch_shapes=[pltpu.VMEM(s, F32) for s in score_shapes]
        + [pltpu.VMEM((seq, HEAD_DIM), F32) for _ in range(6)],
        compiler_params=pltpu.CompilerParams(
            dimension_semantics=("arbitrary", "arbitrary"),
            vmem_limit_bytes=_vmem_limit(in_tiles + out_tiles + scratch)),
        name="dilated_attn",
    )(qkv1, qkv2, qkv3, bias)


def _outproj_kernel(mod_ref, o_ref, sg_ref, sa_ref, ms_ref, x_ref, wao_ref, wo_ref,
                    lng_ref, lnb_ref, out_ref, *, alpha):
    gate1 = 1.0 + mod_ref[0, :, 2 * D_MODEL:3 * D_MODEL]
    o = jnp.concatenate([o_ref[0, j] for j in range(HEADS_PER_GROUP)], axis=1)
    a_in = (o.astype(F32) * sg_ref[0].astype(F32)).astype(BF16)
    a_out = jnp.dot(a_in, wao_ref[...], preferred_element_type=F32)
    merged = (sa_ref[0].astype(F32) * a_out + ms_ref[0].astype(F32)).astype(BF16)
    y = jnp.dot(merged, wo_ref[...], preferred_element_type=F32)
    v = alpha * x_ref[0] + gate1 * y
    mu = jnp.mean(v, axis=-1, keepdims=True)
    dv = v - mu
    var = jnp.mean(dv * dv, axis=-1, keepdims=True)
    out_ref[0] = dv * lax.rsqrt(var + LN_EPS) * lng_ref[...] + lnb_ref[...]


def _outproj_call(mod, o, sg, sa, ms, x, w_attn_out, w_o, ln_g, ln_b, alpha):
    bsz, seq, _ = x.shape
    tm = OUTPROJ_TM
    assert seq % tm == 0
    hg = HEADS_PER_GROUP
    tok = lambda b, i: (b, i, 0)
    const = lambda b, i: (0, 0)
    tiles = (2 * ATTN_WIDTH * 2 + 2 * D_MODEL * 2 + 2 * D_MODEL * 4) * tm * 2
    weights = (ATTN_WIDTH + D_MODEL) * D_MODEL * 2 * 2
    temps = 4 * tm * D_MODEL * 4
    return pl.pallas_call(
        functools.partial(_outproj_kernel, alpha=alpha),
        grid=(bsz, seq // tm),
        in_specs=[
            pl.BlockSpec((1, 1, 3 * D_MODEL), lambda b, i: (b, 0, 0)),
            pl.BlockSpec((1, hg, tm, HEAD_DIM), lambda b, i: (b, 0, i, 0)),
            pl.BlockSpec((1, tm, ATTN_WIDTH), tok),
            pl.BlockSpec((1, tm, D_MODEL), tok),
            pl.BlockSpec((1, tm, D_MODEL), tok),
            pl.BlockSpec((1, tm, D_MODEL), tok),
            pl.BlockSpec((ATTN_WIDTH, D_MODEL), const),
            pl.BlockSpec((D_MODEL, D_MODEL), const),
            pl.BlockSpec((1, D_MODEL), const),
            pl.BlockSpec((1, D_MODEL), const),
        ],
        out_specs=pl.BlockSpec((1, tm, D_MODEL), tok),
        out_shape=jax.ShapeDtypeStruct((bsz, seq, D_MODEL), F32),
        compiler_params=pltpu.CompilerParams(
            dimension_semantics=("arbitrary", "arbitrary"),
            vmem_limit_bytes=_vmem_limit(tiles + weights + temps)),
        name="out_proj",
    )(mod, o, sg, sa, ms, x, w_attn_out, w_o, ln_g, ln_b)


def kernel(x, c, w_ada, b_ada, w_in, conv_w, conv_b, rel_bias, w_attn_out, w_conv_out, w_o, ln_g, ln_b):
    bsz = x.shape[0]
    depth = w_in.shape[0]
    alpha = (2.0 * depth) ** 0.25
    bias = _bias_call(rel_bias)
    for layer in range(depth):
        mod = _ada_call(c, w_ada[layer], b_ada[layer]).reshape(bsz, 1, 3 * D_MODEL)
        qkv1, qkv2, qkv3, sg, sa, ms = _inproj_call(
            mod, x, w_in[layer].astype(BF16), w_conv_out[layer].astype(BF16),
            conv_w[layer], conv_b[layer][None, :])
        o = _attn_call(qkv1, qkv2, qkv3, bias)
        x = _outproj_call(mod, o, sg, sa, ms, x, w_attn_out[layer].astype(BF16),
                          w_o[layer].astype(BF16), ln_g[layer][None, :], ln_b[layer][None, :], alpha)
    return x
```

```python
import functools
import math

import numpy as np
import jax
import jax.numpy as jnp
from jax import lax
from jax.experimental import pallas as pl
from jax.experimental.pallas import tpu as pltpu

F32 = jnp.float32
BF16 = jnp.bfloat16

D_MODEL = 1024
HEAD_DIM = 128
HEADS_PER_GROUP = 4
DILATED_GROUPS = ((128, 1), (512, 4), (2048, 16))
N_GROUPS = len(DILATED_GROUPS)
N_ATTN_HEADS = N_GROUPS * HEADS_PER_GROUP
QKV_WIDTH = N_ATTN_HEADS * HEAD_DIM
ATTN_WIDTH = HEADS_PER_GROUP * HEAD_DIM
CONV_WIDTH = D_MODEL
CONV_K = 3
N_BUCKETS = 32
MAX_EXACT = 16
MAX_DISTANCE = 2048
BLOCK = 128
LN_EPS = 1e-5
NEG_INF = -1e30
Q_SCALE = HEAD_DIM ** -0.5

COL_Q = 0
COL_K = QKV_WIDTH
COL_V = 2 * QKV_WIDTH
COL_GATTN = 3 * QKV_WIDTH
COL_U = COL_GATTN + ATTN_WIDTH
COL_B = COL_U + CONV_WIDTH
COL_C = COL_B + CONV_WIDTH
COL_GCONV = COL_C + CONV_WIDTH
COL_MATTN = COL_GCONV + CONV_WIDTH
COL_MCONV = COL_MATTN + D_MODEL
N_COLS = COL_MCONV + D_MODEL

V7X_VMEM_BYTES = 64 * 1024 * 1024
SUBLANES = 8
INPROJ_TM = 512
OUTPROJ_TM = 512
OUTPROJ_ROWS = 256
CONV_CHUNK = 256
ATTN_BLOCKS_PER_STAGE = 2


def _vmem_limit(estimate_bytes):
    return int(min(V7X_VMEM_BYTES - (4 << 20), estimate_bytes * 5 // 4 + (4 << 20)))


def _bucket_table():
    a = np.arange(BLOCK)[:, None]
    b = np.arange(2 * BLOCK)[None, :]
    steps = a + BLOCK - b
    out = []
    for window, dilation in DILATED_GROUPS:
        n_steps = window // dilation
        valid = (steps >= 0) & (steps <= n_steps)
        dist = np.maximum(steps, 0) * dilation
        n = np.maximum(dist, 1).astype(np.float32)
        large = MAX_EXACT + (np.log(n / np.float32(MAX_EXACT)) / np.float32(math.log(MAX_DISTANCE / MAX_EXACT))
                             * np.float32(N_BUCKETS - MAX_EXACT)).astype(np.int32)
        large = np.minimum(large, N_BUCKETS - 1)
        bucket = np.where(dist < MAX_EXACT, dist, large)
        out.append(np.where(valid, bucket, -1).astype(np.int32))
    return np.stack(out)


_BUCKETS = _bucket_table()


def _sigmoid(v):
    return 1.0 / (1.0 + jnp.exp(-v))


def _ada_kernel(c_ref, w_ref, b_ref, o_ref):
    c = c_ref[...]
    o_ref[...] = jnp.dot(c * _sigmoid(c), w_ref[...], preferred_element_type=F32) + b_ref[...]


def _ada_call(c, w_ada, b_ada):
    bsz = c.shape[0]
    return pl.pallas_call(
        _ada_kernel,
        out_shape=jax.ShapeDtypeStruct((bsz, 3 * D_MODEL), F32),
        name="ada_mod",
    )(c, w_ada, b_ada[None, :])


def _bias_kernel(rb_ref, bucket_ref, o_ref):
    for g in range(N_GROUPS):
        bk = bucket_ref[g]
        for j in range(HEADS_PER_GROUP):
            h = g * HEADS_PER_GROUP + j
            acc = jnp.full(bk.shape, NEG_INF, F32)
            for k in range(N_BUCKETS):
                acc = jnp.where(bk == k, rb_ref[k, h], acc)
            o_ref[h, 0] = acc
            o_ref[h, 1] = jnp.concatenate(
                [acc[:, BLOCK:], jnp.full((BLOCK, BLOCK), NEG_INF, F32)], axis=1)


def _bias_call(rel_bias):
    return pl.pallas_call(
        _bias_kernel,
        out_shape=jax.ShapeDtypeStruct((N_ATTN_HEADS, 2, BLOCK, 2 * BLOCK), F32),
        in_specs=[pl.BlockSpec(memory_space=pltpu.SMEM), pl.BlockSpec(memory_space=pltpu.VMEM)],
        out_specs=pl.BlockSpec(memory_space=pltpu.VMEM),
        name="bias_tables",
    )(rel_bias, jnp.asarray(_BUCKETS))


def _inproj_kernel(mod_ref, x_ref, w_ref, wco_ref, cw_ref, cb_ref,
                   qkv1_ref, qkv2_ref, qkv3_ref, sg_ref, sa_ref, ms_ref,
                   hs_ref, zbuf_ref, spre_ref, *, tm):
    i = pl.program_id(1)
    shift = mod_ref[0, :, 0:D_MODEL]
    scale1 = 1.0 + mod_ref[0, :, D_MODEL:2 * D_MODEL]
    h = x_ref[0] * scale1 + shift
    h_n = h.astype(BF16)

    n_slab = D_MODEL // HEAD_DIM
    for s in range(n_slab):
        hs_ref[s] = h[:, s * HEAD_DIM:(s + 1) * HEAD_DIM]

    def by_residue(d):
        return jnp.concatenate(
            [jnp.concatenate([hs_ref[s, pl.ds(r, tm // d, stride=d), :] for s in range(n_slab)], axis=1)
             for r in range(d)], axis=0).astype(BF16)

    def proj(hv, c0, width):
        return jnp.dot(hv, w_ref[:, c0:c0 + width], preferred_element_type=F32)

    def qkv(hv, g, store):
        for t, col in enumerate((COL_Q, COL_K, COL_V)):
            r = proj(hv, col + g * ATTN_WIDTH, ATTN_WIDTH)
            if t == 0:
                r = r * Q_SCALE
            r = r.astype(BF16)
            for j in range(HEADS_PER_GROUP):
                store(t, j, r[:, j * HEAD_DIM:(j + 1) * HEAD_DIM])

    def store1(t, j, v):
        qkv1_ref[0, t, j] = v
    qkv(h_n, 0, store1)

    def store_sub(ref, d):
        def store(t, j, v):
            rows = tm // d
            for r in range(d):
                ref[0, t, j, r] = v[r * rows:(r + 1) * rows]
        return store
    d4, d16 = DILATED_GROUPS[1][1], DILATED_GROUPS[2][1]
    qkv(by_residue(d4), 1, store_sub(qkv2_ref, d4))
    qkv(by_residue(d16), 2, store_sub(qkv3_ref, d16))

    ga = proj(h_n, COL_GATTN, ATTN_WIDTH)
    sgv = (ga * _sigmoid(ga)).astype(BF16)
    for j in range(HEADS_PER_GROUP):
        sg_ref[0, j] = sgv[:, j * HEAD_DIM:(j + 1) * HEAD_DIM]

    for c0 in range(0, D_MODEL, 2 * CONV_CHUNK):
        ma = proj(h_n, COL_MATTN + c0, 2 * CONV_CHUNK)
        sa_ref[0, :, c0:c0 + 2 * CONV_CHUNK] = _sigmoid(ma).astype(BF16)

    @pl.when(i == 0)
    def _():
        zbuf_ref[0:SUBLANES, :] = jnp.zeros((SUBLANES, CONV_WIDTH), F32)

    for c0 in range(0, CONV_WIDTH, CONV_CHUNK):
        cs = slice(c0, c0 + CONV_CHUNK)
        u = proj(h_n, COL_U + c0, CONV_CHUNK)
        cg = proj(h_n, COL_C + c0, CONV_CHUNK)
        z = cg * u
        zbuf_ref[SUBLANES:SUBLANES + tm, cs] = z
        zm1 = zbuf_ref[SUBLANES - 1:SUBLANES - 1 + tm, cs]
        zm2 = zbuf_ref[SUBLANES - 2:SUBLANES - 2 + tm, cs]
        y = cw_ref[0:1, cs] * zm2 + cw_ref[1:2, cs] * zm1 + cw_ref[2:3, cs] * z + cb_ref[:, cs]
        bg = proj(h_n, COL_B + c0, CONV_CHUNK)
        gc = proj(h_n, COL_GCONV + c0, CONV_CHUNK)
        spre_ref[:, cs] = (bg * y * (gc * _sigmoid(gc))).astype(BF16)
    zbuf_ref[0:SUBLANES, :] = zbuf_ref[tm:tm + SUBLANES, :]

    spre = spre_ref[...]
    for c0 in range(0, D_MODEL, CONV_CHUNK):
        cs = slice(c0, c0 + CONV_CHUNK)
        so = jnp.dot(spre, wco_ref[:, cs], preferred_element_type=F32)
        mc = proj(h_n, COL_MCONV + c0, CONV_CHUNK)
        ms_ref[0, :, cs] = (_sigmoid(mc) * so).astype(BF16)


def _inproj_call(mod, x, w_in, w_conv_out, conv_w, conv_b):
    bsz, seq, _ = x.shape
    tm = INPROJ_TM
    n_t = seq // tm
    d4, d16 = DILATED_GROUPS[1][1], DILATED_GROUPS[2][1]
    sub4, sub16 = seq // d4, seq // d16
    assert seq % tm == 0 and tm % (d16 * 16) == 0
    hg = HEADS_PER_GROUP

    in_specs = [
        pl.BlockSpec((1, 1, 3 * D_MODEL), lambda b, i: (b, 0, 0)),
        pl.BlockSpec((1, tm, D_MODEL), lambda b, i: (b, i, 0)),
        pl.BlockSpec((D_MODEL, N_COLS), lambda b, i: (0, 0), pipeline_mode=pl.Buffered(1)),
        pl.BlockSpec((CONV_WIDTH, D_MODEL), lambda b, i: (0, 0), pipeline_mode=pl.Buffered(1)),
        pl.BlockSpec((CONV_K, CONV_WIDTH), lambda b, i: (0, 0)),
        pl.BlockSpec((1, CONV_WIDTH), lambda b, i: (0, 0)),
    ]
    out_shape = [
        jax.ShapeDtypeStruct((bsz, 3, hg, seq, HEAD_DIM), BF16),
        jax.ShapeDtypeStruct((bsz, 3, hg, d4, sub4, HEAD_DIM), BF16),
        jax.ShapeDtypeStruct((bsz, 3, hg, d16, sub16, HEAD_DIM), BF16),
        jax.ShapeDtypeStruct((bsz, hg, seq, HEAD_DIM), BF16),
        jax.ShapeDtypeStruct((bsz, seq, D_MODEL), BF16),
        jax.ShapeDtypeStruct((bsz, seq, D_MODEL), BF16),
    ]
    out_specs = [
        pl.BlockSpec((1, 3, hg, tm, HEAD_DIM), lambda b, i: (b, 0, 0, i, 0)),
        pl.BlockSpec((1, 3, hg, d4, tm // d4, HEAD_DIM), lambda b, i: (b, 0, 0, 0, i, 0)),
        pl.BlockSpec((1, 3, hg, d16, tm // d16, HEAD_DIM), lambda b, i: (b, 0, 0, 0, i, 0)),
        pl.BlockSpec((1, hg, tm, HEAD_DIM), lambda b, i: (b, 0, i, 0)),
        pl.BlockSpec((1, tm, D_MODEL), lambda b, i: (b, i, 0)),
        pl.BlockSpec((1, tm, D_MODEL), lambda b, i: (b, i, 0)),
    ]
    weights = D_MODEL * N_COLS * 2 + CONV_WIDTH * D_MODEL * 2
    x_tiles = tm * D_MODEL * 4 * 2
    out_tiles = (3 * 3 * ATTN_WIDTH + ATTN_WIDTH + 2 * D_MODEL) * tm * 2 * 2
    scratch = tm * D_MODEL * 4 + (tm + SUBLANES) * CONV_WIDTH * 4 + tm * CONV_WIDTH * 2
    return pl.pallas_call(
        functools.partial(_inproj_kernel, tm=tm),
        grid=(bsz, n_t),
        in_specs=in_specs,
        out_specs=out_specs,
        out_shape=out_shape,
        scratch_shapes=[pltpu.VMEM((D_MODEL // HEAD_DIM, tm, HEAD_DIM), F32),
                        pltpu.VMEM((tm + SUBLANES, CONV_WIDTH), F32),
                        pltpu.VMEM((tm, CONV_WIDTH), BF16)],
        compiler_params=pltpu.CompilerParams(
            dimension_semantics=("arbitrary", "arbitrary"),
            vmem_limit_bytes=_vmem_limit(weights + x_tiles + out_tiles + scratch)),
        name="in_proj",
    )(mod, x, w_in, w_conv_out, conv_w, conv_b)


def _nt_dot(q, k):
    return lax.dot_general(q, k, (((1,), (1,)), ((), ())), preferred_element_type=F32)


def _softmax_pv(s, v):
    m = jnp.max(s, axis=1, keepdims=True)
    p = jnp.exp(s - m)
    l = jnp.sum(p, axis=1, keepdims=True)
    acc = jnp.dot(p.astype(BF16), v, preferred_element_type=F32)
    return acc * (1.0 / l), m + jnp.log(l)


def _attn_kernel(qkv1_ref, qkv2_ref, qkv3_ref, bias_ref, sg_ref, o_ref,
                 sa1_ref, sa2_ref, sa3_ref, sb1_ref, sb2_ref, sb3_ref,
                 o1_ref, o2_ref, o3_ref, l1_ref, l2_ref, l3_ref, *, seq):
    d4, d16 = DILATED_GROUPS[1][1], DILATED_GROUPS[2][1]
    per = ATTN_BLOCKS_PER_STAGE
    n_it = seq // BLOCK // per
    blk4 = seq // d4 // BLOCK
    assert seq // d16 == BLOCK and blk4 & (blk4 - 1) == 0 and n_it % 2 == 0 and n_it * per * BLOCK == seq
    shift4 = blk4.bit_length() - 1

    def rows(start, size):
        if isinstance(start, int):
            return pl.ds(start, size)
        return pl.ds(pl.multiple_of(start, BLOCK), size)

    def plan(it, u):
        it = it * per + u
        if isinstance(it, int):
            f1, nb, r4 = int(it == 0), it & (blk4 - 1), it >> shift4
            f2 = int(nb == 0)
        else:
            f1 = (it == 0).astype(jnp.int32)
            nb = jnp.bitwise_and(it, blk4 - 1)
            f2 = (nb == 0).astype(jnp.int32)
            r4 = lax.shift_right_logical(it, shift4)
        return dict(q=it * BLOCK, k1=(it - 1 + f1) * BLOCK, f1=f1, k2=(it - 1 + f2) * BLOCK, f2=f2,
                    out2=nb * (BLOCK * d4) + r4, out3=it)

    def scores(it, s_refs):
        for u in range(per):
            p = plan(it, u)
            q = rows(p["q"], BLOCK)
            s_refs[0][u] = _nt_dot(qkv1_ref[0, 0, 0, q, :], qkv1_ref[0, 1, 0, rows(p["k1"], 2 * BLOCK), :]) \
                + bias_ref[0, 0, p["f1"]]
            s_refs[1][u] = _nt_dot(qkv2_ref[0, 0, 0, q, :], qkv2_ref[0, 1, 0, rows(p["k2"], 2 * BLOCK), :]) \
                + bias_ref[1, 0, p["f2"]]
            s_refs[2][u] = _nt_dot(qkv3_ref[0, 0, 0, q, :], qkv3_ref[0, 1, 0, q, :]) \
                + bias_ref[2, 0, 1, :, 0:BLOCK]

    def put(o_dst, l_dst, idx, o, lse):
        o_dst[idx, :] = o
        l_dst[idx, :] = jnp.broadcast_to(lse, (BLOCK, HEAD_DIM))

    def finish(it, s_refs):
        for u in range(per):
            p = plan(it, u)
            q = rows(p["q"], BLOCK)
            o, lse = _softmax_pv(s_refs[0][u], qkv1_ref[0, 2, 0, rows(p["k1"], 2 * BLOCK), :])
            put(o1_ref, l1_ref, q, o, lse)
            o, lse = _softmax_pv(s_refs[1][u], qkv2_ref[0, 2, 0, rows(p["k2"], 2 * BLOCK), :])
            put(o2_ref, l2_ref, pl.ds(p["out2"], BLOCK, stride=d4), o, lse)
            o, lse = _softmax_pv(s_refs[2][u], qkv3_ref[0, 2, 0, q, :])
            put(o3_ref, l3_ref, pl.ds(p["out3"], BLOCK, stride=d16), o, lse)

    s_a = (sa1_ref, sa2_ref, sa3_ref)
    s_b = (sb1_ref, sb2_ref, sb3_ref)
    scores(0, s_a)

    def body(i, carry):
        scores(2 * i + 1, s_b)
        finish(2 * i, s_a)
        scores(2 * i + 2, s_a)
        finish(2 * i + 1, s_b)
        return carry
    lax.fori_loop(0, n_it // 2 - 1, body, 0)
    scores(n_it - 1, s_b)
    finish(n_it - 2, s_a)
    finish(n_it - 1, s_b)

    chunk = 2 * BLOCK
    for c0 in range(0, seq, chunk):
        rws = slice(c0, c0 + chunk)
        la, lb, lc = l1_ref[rws, :], l2_ref[rws, :], l3_ref[rws, :]
        m = jnp.maximum(jnp.maximum(la, lb), lc)
        wa, wb, wc = jnp.exp(la - m), jnp.exp(lb - m), jnp.exp(lc - m)
        num = wa * o1_ref[rws, :] + wb * o2_ref[rws, :] + wc * o3_ref[rws, :]
        gate = sg_ref[0, 0, rws, :].astype(F32)
        o_ref[0, 0, rws, :] = (num * (1.0 / (wa + wb + wc)) * gate).astype(BF16)


def _attn_call(qkv1, qkv2, qkv3, bias, sg):
    bsz, _, hg, seq, _ = qkv1.shape
    qkv2 = qkv2.reshape(qkv1.shape)
    qkv3 = qkv3.reshape(qkv1.shape)
    bias = bias.reshape(N_GROUPS, hg, 2, BLOCK, 2 * BLOCK)
    in_tiles = (3 * 3 + 1) * seq * HEAD_DIM * 2 * 2 + N_GROUPS * 2 * BLOCK * 2 * BLOCK * 4 * 2
    out_tiles = seq * HEAD_DIM * 2 * 2
    per = ATTN_BLOCKS_PER_STAGE
    score_shapes = [(per, BLOCK, 2 * BLOCK), (per, BLOCK, 2 * BLOCK), (per, BLOCK, BLOCK)] * 2
    scratch = 6 * seq * HEAD_DIM * 4 + sum(p * a * b * 4 for p, a, b in score_shapes)
    qkv_spec = pl.BlockSpec((1, 3, 1, seq, HEAD_DIM), lambda b, j: (b, 0, j, 0, 0))
    return pl.pallas_call(
        functools.partial(_attn_kernel, seq=seq),
        grid=(bsz, hg),
        in_specs=[
            qkv_spec, qkv_spec, qkv_spec,
            pl.BlockSpec((N_GROUPS, 1, 2, BLOCK, 2 * BLOCK), lambda b, j: (0, j, 0, 0, 0)),
            pl.BlockSpec((1, 1, seq, HEAD_DIM), lambda b, j: (b, j, 0, 0)),
        ],
        out_specs=pl.BlockSpec((1, 1, seq, HEAD_DIM), lambda b, j: (b, j, 0, 0)),
        out_shape=jax.ShapeDtypeStruct((bsz, hg, seq, HEAD_DIM), BF16),
        scratch_shapes=[pltpu.VMEM(s, F32) for s in score_shapes]
        + [pltpu.VMEM((seq, HEAD_DIM), F32) for _ in range(6)],
        compiler_params=pltpu.CompilerParams(
            dimension_semantics=("arbitrary", "arbitrary"),
            vmem_limit_bytes=_vmem_limit(in_tiles + out_tiles + scratch)),
        name="dilated_attn",
    )(qkv1, qkv2, qkv3, bias, sg)


def _outproj_kernel(mod_ref, o_ref, sa_ref, ms_ref, x_ref, wao_ref, wo_ref,
                    lng_ref, lnb_ref, out_ref, *, alpha, tm):
    gate1 = 1.0 + mod_ref[0, :, 2 * D_MODEL:3 * D_MODEL]
    for r0 in range(0, tm, OUTPROJ_ROWS):
        rws = slice(r0, r0 + OUTPROJ_ROWS)
        a_in = jnp.concatenate([o_ref[0, j, rws, :] for j in range(HEADS_PER_GROUP)], axis=1)
        a_out = jnp.dot(a_in, wao_ref[...], preferred_element_type=F32)
        merged = (sa_ref[0, rws, :].astype(F32) * a_out + ms_ref[0, rws, :].astype(F32)).astype(BF16)
        y = jnp.dot(merged, wo_ref[...], preferred_element_type=F32)
        v = alpha * x_ref[0, rws, :] + gate1 * y
        mu = jnp.mean(v, axis=-1, keepdims=True)
        dv = v - mu
        var = jnp.mean(dv * dv, axis=-1, keepdims=True)
        out_ref[0, rws, :] = dv * lax.rsqrt(var + LN_EPS) * lng_ref[...] + lnb_ref[...]


def _outproj_call(mod, o, sa, ms, x, w_attn_out, w_o, ln_g, ln_b, alpha):
    bsz, seq, _ = x.shape
    tm = OUTPROJ_TM
    assert seq % tm == 0 and tm % OUTPROJ_ROWS == 0
    hg = HEADS_PER_GROUP
    tok = lambda b, i: (b, i, 0)
    const = lambda b, i: (0, 0)
    tiles = (ATTN_WIDTH * 2 + 2 * D_MODEL * 2 + 2 * D_MODEL * 4) * tm * 2
    weights = (ATTN_WIDTH + D_MODEL) * D_MODEL * 2 * 2
    temps = 4 * tm * D_MODEL * 4
    return pl.pallas_call(
        functools.partial(_outproj_kernel, alpha=alpha, tm=tm),
        grid=(bsz, seq // tm),
        in_specs=[
            pl.BlockSpec((1, 1, 3 * D_MODEL), lambda b, i: (b, 0, 0)),
            pl.BlockSpec((1, hg, tm, HEAD_DIM), lambda b, i: (b, 0, i, 0)),
            pl.BlockSpec((1, tm, D_MODEL), tok),
            pl.BlockSpec((1, tm, D_MODEL), tok),
            pl.BlockSpec((1, tm, D_MODEL), tok),
            pl.BlockSpec((ATTN_WIDTH, D_MODEL), const),
            pl.BlockSpec((D_MODEL, D_MODEL), const),
            pl.BlockSpec((1, D_MODEL), const),
            pl.BlockSpec((1, D_MODEL), const),
        ],
        out_specs=pl.BlockSpec((1, tm, D_MODEL), tok),
        out_shape=jax.ShapeDtypeStruct((bsz, seq, D_MODEL), F32),
        compiler_params=pltpu.CompilerParams(
            dimension_semantics=("arbitrary", "arbitrary"),
            vmem_limit_bytes=_vmem_limit(tiles + weights + temps)),
        name="out_proj",
    )(mod, o, sa, ms, x, w_attn_out, w_o, ln_g, ln_b)


def kernel(x, c, w_ada, b_ada, w_in, conv_w, conv_b, rel_bias, w_attn_out, w_conv_out, w_o, ln_g, ln_b):
    bsz = x.shape[0]
    depth = w_in.shape[0]
    alpha = (2.0 * depth) ** 0.25
    bias = _bias_call(rel_bias)
    for layer in range(depth):
        mod = _ada_call(c, w_ada[layer], b_ada[layer]).reshape(bsz, 1, 3 * D_MODEL)
        qkv1, qkv2, qkv3, sg, sa, ms = _inproj_call(
            mod, x, w_in[layer].astype(BF16), w_conv_out[layer].astype(BF16),
            conv_w[layer], conv_b[layer][None, :])
        o = _attn_call(qkv1, qkv2, qkv3, bias, sg)
        x = _outproj_call(mod, o, sa, ms, x, w_attn_out[layer].astype(BF16),
                          w_o[layer].astype(BF16), ln_g[layer][None, :], ln_b[layer][None, :], alpha)
    return x
```

```python
import functools
import math

import numpy as np
import jax
import jax.numpy as jnp
from jax import lax
from jax.experimental import pallas as pl
from jax.experimental.pallas import tpu as pltpu

F32 = jnp.float32
BF16 = jnp.bfloat16

D_MODEL = 1024
HEAD_DIM = 128
HEADS_PER_GROUP = 4
DILATED_GROUPS = ((128, 1), (512, 4), (2048, 16))
N_GROUPS = len(DILATED_GROUPS)
N_ATTN_HEADS = N_GROUPS * HEADS_PER_GROUP
QKV_WIDTH = N_ATTN_HEADS * HEAD_DIM
ATTN_WIDTH = HEADS_PER_GROUP * HEAD_DIM
CONV_WIDTH = D_MODEL
CONV_K = 3
N_BUCKETS = 32
MAX_EXACT = 16
MAX_DISTANCE = 2048
BLOCK = 128
LN_EPS = 1e-5
NEG_INF = -1e30
Q_SCALE = HEAD_DIM ** -0.5

COL_Q = 0
COL_K = QKV_WIDTH
COL_V = 2 * QKV_WIDTH
COL_GATTN = 3 * QKV_WIDTH
COL_U = COL_GATTN + ATTN_WIDTH
COL_B = COL_U + CONV_WIDTH
COL_C = COL_B + CONV_WIDTH
COL_GCONV = COL_C + CONV_WIDTH
COL_MATTN = COL_GCONV + CONV_WIDTH
COL_MCONV = COL_MATTN + D_MODEL
N_COLS = COL_MCONV + D_MODEL

V7X_VMEM_BYTES = 64 * 1024 * 1024
SUBLANES = 8
INPROJ_TM = 512
OUTPROJ_TM = 1024
OUTPROJ_ROWS = 256
CONV_CHUNK = 256
ATTN_BLOCKS_PER_STAGE = 2


def _vmem_limit(estimate_bytes):
    return int(min(V7X_VMEM_BYTES - (4 << 20), estimate_bytes * 5 // 4 + (4 << 20)))


def _bucket_table():
    a = np.arange(BLOCK)[:, None]
    b = np.arange(2 * BLOCK)[None, :]
    steps = a + BLOCK - b
    out = []
    for window, dilation in DILATED_GROUPS:
        n_steps = window // dilation
        valid = (steps >= 0) & (steps <= n_steps)
        dist = np.maximum(steps, 0) * dilation
        n = np.maximum(dist, 1).astype(np.float32)
        large = MAX_EXACT + (np.log(n / np.float32(MAX_EXACT)) / np.float32(math.log(MAX_DISTANCE / MAX_EXACT))
                             * np.float32(N_BUCKETS - MAX_EXACT)).astype(np.int32)
        large = np.minimum(large, N_BUCKETS - 1)
        bucket = np.where(dist < MAX_EXACT, dist, large)
        out.append(np.where(valid, bucket, -1).astype(np.int32))
    return np.stack(out)


_BUCKETS = _bucket_table()


def _sigmoid(v):
    return 0.5 * jnp.tanh(0.5 * v) + 0.5


def _ada_kernel(c_ref, w_ref, b_ref, o_ref):
    c = c_ref[...]
    o_ref[...] = jnp.dot(c * _sigmoid(c), w_ref[...], preferred_element_type=F32) + b_ref[...]


def _ada_call(c, w_ada, b_ada):
    bsz = c.shape[0]
    return pl.pallas_call(
        _ada_kernel,
        out_shape=jax.ShapeDtypeStruct((bsz, 3 * D_MODEL), F32),
        name="ada_mod",
    )(c, w_ada, b_ada[None, :])


def _bias_kernel(rb_ref, bucket_ref, o_ref):
    for g in range(N_GROUPS):
        bk = bucket_ref[g]
        for j in range(HEADS_PER_GROUP):
            h = g * HEADS_PER_GROUP + j
            acc = jnp.full(bk.shape, NEG_INF, F32)
            for k in range(N_BUCKETS):
                acc = jnp.where(bk == k, rb_ref[k, h], acc)
            o_ref[h, 0] = acc
            o_ref[h, 1] = jnp.concatenate(
                [acc[:, BLOCK:], jnp.full((BLOCK, BLOCK), NEG_INF, F32)], axis=1)


def _bias_call(rel_bias):
    return pl.pallas_call(
        _bias_kernel,
        out_shape=jax.ShapeDtypeStruct((N_ATTN_HEADS, 2, BLOCK, 2 * BLOCK), F32),
        in_specs=[pl.BlockSpec(memory_space=pltpu.SMEM), pl.BlockSpec(memory_space=pltpu.VMEM)],
        out_specs=pl.BlockSpec(memory_space=pltpu.VMEM),
        name="bias_tables",
    )(rel_bias, jnp.asarray(_BUCKETS))


def _inproj_kernel(mod_ref, x_ref, w_ref, wco_ref, cw_ref, cb_ref,
                   qkv1_ref, qkv2_ref, qkv3_ref, sg_ref, sa_ref, ms_ref,
                   hs_ref, hb_ref, zbuf_ref, spre_ref, *, tm):
    i = pl.program_id(1)

    @pl.when(i == 0)
    def _():
        zbuf_ref[0:SUBLANES, :] = jnp.zeros((SUBLANES, CONV_WIDTH), F32)

    shift = mod_ref[0, :, 0:D_MODEL]
    scale1 = 1.0 + mod_ref[0, :, D_MODEL:2 * D_MODEL]
    h = x_ref[0] * scale1 + shift
    hb_ref[0] = h.astype(BF16)

    n_slab = D_MODEL // HEAD_DIM
    for s in range(n_slab):
        hs_ref[s] = h[:, s * HEAD_DIM:(s + 1) * HEAD_DIM]

    def by_residue(d):
        return jnp.concatenate(
            [jnp.concatenate([hs_ref[s, pl.ds(r, tm // d, stride=d), :] for s in range(n_slab)], axis=1)
             for r in range(d)], axis=0).astype(BF16)
    d4, d16 = DILATED_GROUPS[1][1], DILATED_GROUPS[2][1]
    hb_ref[1] = by_residue(d4)
    hb_ref[2] = by_residue(d16)

    def proj(g, c0, width):
        return jnp.dot(hb_ref[g], w_ref[:, c0:c0 + width], preferred_element_type=F32)

    def qkv(g, store):
        for t, col in enumerate((COL_Q, COL_K, COL_V)):
            r = proj(g, col + g * ATTN_WIDTH, ATTN_WIDTH)
            if t == 0:
                r = r * Q_SCALE
            r = r.astype(BF16)
            for j in range(HEADS_PER_GROUP):
                store(t, j, r[:, j * HEAD_DIM:(j + 1) * HEAD_DIM])

    def store1(t, j, v):
        qkv1_ref[0, t, j] = v
    qkv(0, store1)

    def store_sub(ref, d):
        def store(t, j, v):
            rows = tm // d
            for r in range(d):
                ref[0, t, j, r] = v[r * rows:(r + 1) * rows]
        return store
    qkv(1, store_sub(qkv2_ref, d4))
    qkv(2, store_sub(qkv3_ref, d16))

    ga = proj(0, COL_GATTN, ATTN_WIDTH)
    sgv = (ga * _sigmoid(ga)).astype(BF16)
    for j in range(HEADS_PER_GROUP):
        sg_ref[0, j] = sgv[:, j * HEAD_DIM:(j + 1) * HEAD_DIM]

    for c0 in range(0, D_MODEL, CONV_CHUNK):
        ma = proj(0, COL_MATTN + c0, CONV_CHUNK)
        sa_ref[0, :, c0:c0 + CONV_CHUNK] = _sigmoid(ma).astype(BF16)

    for c0 in range(0, CONV_WIDTH, CONV_CHUNK):
        cs = slice(c0, c0 + CONV_CHUNK)
        u = proj(0, COL_U + c0, CONV_CHUNK)
        cg = proj(0, COL_C + c0, CONV_CHUNK)
        z = cg * u
        zbuf_ref[SUBLANES:SUBLANES + tm, cs] = z
        zm1 = zbuf_ref[SUBLANES - 1:SUBLANES - 1 + tm, cs]
        zm2 = zbuf_ref[SUBLANES - 2:SUBLANES - 2 + tm, cs]
        y = cw_ref[0:1, cs] * zm2 + cw_ref[1:2, cs] * zm1 + cw_ref[2:3, cs] * z + cb_ref[:, cs]
        bg = proj(0, COL_B + c0, CONV_CHUNK)
        gc = proj(0, COL_GCONV + c0, CONV_CHUNK)
        spre_ref[:, cs] = (bg * y * (gc * _sigmoid(gc))).astype(BF16)
    zbuf_ref[0:SUBLANES, :] = zbuf_ref[tm:tm + SUBLANES, :]

    for c0 in range(0, D_MODEL, CONV_CHUNK):
        cs = slice(c0, c0 + CONV_CHUNK)
        so = jnp.dot(spre_ref[...], wco_ref[:, cs], preferred_element_type=F32)
        mc = proj(0, COL_MCONV + c0, CONV_CHUNK)
        ms_ref[0, :, cs] = (_sigmoid(mc) * so).astype(BF16)


def _inproj_call(mod, x, w_in, w_conv_out, conv_w, conv_b):
    bsz, seq, _ = x.shape
    tm = INPROJ_TM
    n_t = seq // tm
    d4, d16 = DILATED_GROUPS[1][1], DILATED_GROUPS[2][1]
    sub4, sub16 = seq // d4, seq // d16
    assert seq % tm == 0 and tm % (d16 * 16) == 0
    hg = HEADS_PER_GROUP

    in_specs = [
        pl.BlockSpec((1, 1, 3 * D_MODEL), lambda b, i: (b, 0, 0)),
        pl.BlockSpec((1, tm, D_MODEL), lambda b, i: (b, i, 0)),
        pl.BlockSpec((D_MODEL, N_COLS), lambda b, i: (0, 0), pipeline_mode=pl.Buffered(1)),
        pl.BlockSpec((CONV_WIDTH, D_MODEL), lambda b, i: (0, 0), pipeline_mode=pl.Buffered(1)),
        pl.BlockSpec((CONV_K, CONV_WIDTH), lambda b, i: (0, 0)),
        pl.BlockSpec((1, CONV_WIDTH), lambda b, i: (0, 0)),
    ]
    out_shape = [
        jax.ShapeDtypeStruct((bsz, 3, hg, seq, HEAD_DIM), BF16),
        jax.ShapeDtypeStruct((bsz, 3, hg, d4, sub4, HEAD_DIM), BF16),
        jax.ShapeDtypeStruct((bsz, 3, hg, d16, sub16, HEAD_DIM), BF16),
        jax.ShapeDtypeStruct((bsz, hg, seq, HEAD_DIM), BF16),
        jax.ShapeDtypeStruct((bsz, seq, D_MODEL), BF16),
        jax.ShapeDtypeStruct((bsz, seq, D_MODEL), BF16),
    ]
    out_specs = [
        pl.BlockSpec((1, 3, hg, tm, HEAD_DIM), lambda b, i: (b, 0, 0, i, 0)),
        pl.BlockSpec((1, 3, hg, d4, tm // d4, HEAD_DIM), lambda b, i: (b, 0, 0, 0, i, 0)),
        pl.BlockSpec((1, 3, hg, d16, tm // d16, HEAD_DIM), lambda b, i: (b, 0, 0, 0, i, 0)),
        pl.BlockSpec((1, hg, tm, HEAD_DIM), lambda b, i: (b, 0, i, 0)),
        pl.BlockSpec((1, tm, D_MODEL), lambda b, i: (b, i, 0)),
        pl.BlockSpec((1, tm, D_MODEL), lambda b, i: (b, i, 0)),
    ]
    weights = D_MODEL * N_COLS * 2 + CONV_WIDTH * D_MODEL * 2
    x_tiles = tm * D_MODEL * 4 * 2
    out_tiles = (3 * 3 * ATTN_WIDTH + ATTN_WIDTH + 2 * D_MODEL) * tm * 2 * 2
    scratch = (tm * D_MODEL * 4 + N_GROUPS * tm * D_MODEL * 2 + (tm + SUBLANES) * CONV_WIDTH * 4
               + tm * CONV_WIDTH * 2)
    return pl.pallas_call(
        functools.partial(_inproj_kernel, tm=tm),
        grid=(bsz, n_t),
        in_specs=in_specs,
        out_specs=out_specs,
        out_shape=out_shape,
        scratch_shapes=[pltpu.VMEM((D_MODEL // HEAD_DIM, tm, HEAD_DIM), F32),
                        pltpu.VMEM((N_GROUPS, tm, D_MODEL), BF16),
                        pltpu.VMEM((tm + SUBLANES, CONV_WIDTH), F32),
                        pltpu.VMEM((tm, CONV_WIDTH), BF16)],
        compiler_params=pltpu.CompilerParams(
            dimension_semantics=("arbitrary", "arbitrary"),
            vmem_limit_bytes=_vmem_limit(weights + x_tiles + out_tiles + scratch)),
        name="in_proj",
    )(mod, x, w_in, w_conv_out, conv_w, conv_b)


def _nt_dot(q, k):
    return lax.dot_general(q, k, (((1,), (1,)), ((), ())), preferred_element_type=F32)


def _softmax_pv(s, v):
    m = jnp.max(s, axis=1, keepdims=True)
    p = jnp.exp(s - m)
    l = jnp.sum(p, axis=1, keepdims=True)
    acc = jnp.dot(p.astype(BF16), v, preferred_element_type=F32)
    return acc * (1.0 / l), m + jnp.log(l)


def _attn_kernel(qkv1_ref, qkv2_ref, qkv3_ref, bias_ref, sg_ref, o_ref,
                 sa1_ref, sa2_ref, sa3_ref, sb1_ref, sb2_ref, sb3_ref,
                 o1_ref, o2_ref, o3_ref, l1_ref, l2_ref, l3_ref, *, seq):
    d4, d16 = DILATED_GROUPS[1][1], DILATED_GROUPS[2][1]
    per = ATTN_BLOCKS_PER_STAGE
    n_it = seq // BLOCK // per
    blk4 = seq // d4 // BLOCK
    assert seq // d16 == BLOCK and blk4 & (blk4 - 1) == 0 and n_it % 2 == 0 and n_it * per * BLOCK == seq
    shift4 = blk4.bit_length() - 1

    def rows(start, size):
        if isinstance(start, int):
            return pl.ds(start, size)
        return pl.ds(pl.multiple_of(start, BLOCK), size)

    def plan(it, u):
        it = it * per + u
        if isinstance(it, int):
            f1, nb, r4 = int(it == 0), it & (blk4 - 1), it >> shift4
            f2 = int(nb == 0)
        else:
            f1 = (it == 0).astype(jnp.int32)
            nb = jnp.bitwise_and(it, blk4 - 1)
            f2 = (nb == 0).astype(jnp.int32)
            r4 = lax.shift_right_logical(it, shift4)
        return dict(q=it * BLOCK, k1=(it - 1 + f1) * BLOCK, f1=f1, k2=(it - 1 + f2) * BLOCK, f2=f2,
                    out2=nb * (BLOCK * d4) + r4, out3=it)

    def scores(it, s_refs):
        for u in range(per):
            p = plan(it, u)
            q = rows(p["q"], BLOCK)
            s_refs[0][u] = _nt_dot(qkv1_ref[0, 0, 0, q, :], qkv1_ref[0, 1, 0, rows(p["k1"], 2 * BLOCK), :]) \
                + bias_ref[0, 0, p["f1"]]
            s_refs[1][u] = _nt_dot(qkv2_ref[0, 0, 0, q, :], qkv2_ref[0, 1, 0, rows(p["k2"], 2 * BLOCK), :]) \
                + bias_ref[1, 0, p["f2"]]
            s_refs[2][u] = _nt_dot(qkv3_ref[0, 0, 0, q, :], qkv3_ref[0, 1, 0, q, :]) \
                + bias_ref[2, 0, 1, :, 0:BLOCK]

    def put(o_dst, l_dst, idx, o, lse):
        o_dst[idx, :] = o
        l_dst[idx, :] = jnp.broadcast_to(lse, (BLOCK, HEAD_DIM))

    def finish(it, s_refs):
        for u in range(per):
            p = plan(it, u)
            q = rows(p["q"], BLOCK)
            o, lse = _softmax_pv(s_refs[0][u], qkv1_ref[0, 2, 0, rows(p["k1"], 2 * BLOCK), :])
            put(o1_ref, l1_ref, q, o, lse)
            o, lse = _softmax_pv(s_refs[1][u], qkv2_ref[0, 2, 0, rows(p["k2"], 2 * BLOCK), :])
            put(o2_ref, l2_ref, pl.ds(p["out2"], BLOCK, stride=d4), o, lse)
            o, lse = _softmax_pv(s_refs[2][u], qkv3_ref[0, 2, 0, q, :])
            put(o3_ref, l3_ref, pl.ds(p["out3"], BLOCK, stride=d16), o, lse)

    s_a = (sa1_ref, sa2_ref, sa3_ref)
    s_b = (sb1_ref, sb2_ref, sb3_ref)
    scores(0, s_a)

    def body(i, carry):
        scores(2 * i + 1, s_b)
        finish(2 * i, s_a)
        scores(2 * i + 2, s_a)
        finish(2 * i + 1, s_b)
        return carry
    lax.fori_loop(0, n_it // 2 - 1, body, 0)
    scores(n_it - 1, s_b)
    finish(n_it - 2, s_a)
    finish(n_it - 1, s_b)

    chunk = 2 * BLOCK
    for c0 in range(0, seq, chunk):
        rws = slice(c0, c0 + chunk)
        la, lb, lc = l1_ref[rws, :], l2_ref[rws, :], l3_ref[rws, :]
        m = jnp.maximum(jnp.maximum(la, lb), lc)
        wa, wb, wc = jnp.exp(la - m), jnp.exp(lb - m), jnp.exp(lc - m)
        num = wa * o1_ref[rws, :] + wb * o2_ref[rws, :] + wc * o3_ref[rws, :]
        gate = sg_ref[0, 0, rws, :].astype(F32)
        o_ref[0, 0, rws, :] = (num * (1.0 / (wa + wb + wc)) * gate).astype(BF16)


def _attn_call(qkv1, qkv2, qkv3, bias, sg):
    bsz, _, hg, seq, _ = qkv1.shape
    qkv2 = qkv2.reshape(qkv1.shape)
    qkv3 = qkv3.reshape(qkv1.shape)
    bias = bias.reshape(N_GROUPS, hg, 2, BLOCK, 2 * BLOCK)
    in_tiles = (3 * 3 + 1) * seq * HEAD_DIM * 2 * 2 + N_GROUPS * 2 * BLOCK * 2 * BLOCK * 4 * 2
    out_tiles = seq * HEAD_DIM * 2 * 2
    per = ATTN_BLOCKS_PER_STAGE
    score_shapes = [(per, BLOCK, 2 * BLOCK), (per, BLOCK, 2 * BLOCK), (per, BLOCK, BLOCK)] * 2
    scratch = 6 * seq * HEAD_DIM * 4 + sum(p * a * b * 4 for p, a, b in score_shapes)
    qkv_spec = pl.BlockSpec((1, 3, 1, seq, HEAD_DIM), lambda b, j: (b, 0, j, 0, 0))
    return pl.pallas_call(
        functools.partial(_attn_kernel, seq=seq),
        grid=(bsz, hg),
        in_specs=[
            qkv_spec, qkv_spec, qkv_spec,
            pl.BlockSpec((N_GROUPS, 1, 2, BLOCK, 2 * BLOCK), lambda b, j: (0, j, 0, 0, 0)),
            pl.BlockSpec((1, 1, seq, HEAD_DIM), lambda b, j: (b, j, 0, 0)),
        ],
        out_specs=pl.BlockSpec((1, 1, seq, HEAD_DIM), lambda b, j: (b, j, 0, 0)),
        out_shape=jax.ShapeDtypeStruct((bsz, hg, seq, HEAD_DIM), BF16),
        scratch_shapes=[pltpu.VMEM(s, F32) for s in score_shapes]
        + [pltpu.VMEM((seq, HEAD_DIM), F32) for _ in range(6)],
        compiler_params=pltpu.CompilerParams(
            dimension_semantics=("arbitrary", "arbitrary"),
            vmem_limit_bytes=_vmem_limit(in_tiles + out_tiles + scratch)),
        name="dilated_attn",
    )(qkv1, qkv2, qkv3, bias, sg)


def _outproj_kernel(mod_ref, o_ref, sa_ref, ms_ref, x_ref, wao_ref, wo_ref,
                    lng_ref, lnb_ref, out_ref, *, alpha, tm):
    gate1 = 1.0 + mod_ref[0, :, 2 * D_MODEL:3 * D_MODEL]
    for r0 in range(0, tm, OUTPROJ_ROWS):
        rws = slice(r0, r0 + OUTPROJ_ROWS)
        a_in = jnp.concatenate([o_ref[0, j, rws, :] for j in range(HEADS_PER_GROUP)], axis=1)
        a_out = jnp.dot(a_in, wao_ref[...], preferred_element_type=F32)
        merged = (sa_ref[0, rws, :].astype(F32) * a_out + ms_ref[0, rws, :].astype(F32)).astype(BF16)
        y = jnp.dot(merged, wo_ref[...], preferred_element_type=F32)
        v = alpha * x_ref[0, rws, :] + gate1 * y
        mu = jnp.mean(v, axis=-1, keepdims=True)
        dv = v - mu
        var = jnp.mean(dv * dv, axis=-1, keepdims=True)
        out_ref[0, rws, :] = dv * lax.rsqrt(var + LN_EPS) * lng_ref[...] + lnb_ref[...]


def _outproj_call(mod, o, sa, ms, x, w_attn_out, w_o, ln_g, ln_b, alpha):
    bsz, seq, _ = x.shape
    tm = OUTPROJ_TM
    assert seq % tm == 0 and tm % OUTPROJ_ROWS == 0
    hg = HEADS_PER_GROUP
    tok = lambda b, i: (b, i, 0)
    const = lambda b, i: (0, 0)
    tiles = (ATTN_WIDTH * 2 + 2 * D_MODEL * 2 + 2 * D_MODEL * 4) * tm * 2
    weights = (ATTN_WIDTH + D_MODEL) * D_MODEL * 2 * 2
    temps = 4 * tm * D_MODEL * 4
    return pl.pallas_call(
        functools.partial(_outproj_kernel, alpha=alpha, tm=tm),
        grid=(bsz, seq // tm),
        in_specs=[
            pl.BlockSpec((1, 1, 3 * D_MODEL), lambda b, i: (b, 0, 0)),
            pl.BlockSpec((1, hg, tm, HEAD_DIM), lambda b, i: (b, 0, i, 0)),
            pl.BlockSpec((1, tm, D_MODEL), tok),
            pl.BlockSpec((1, tm, D_MODEL), tok),
            pl.BlockSpec((1, tm, D_MODEL), tok),
            pl.BlockSpec((ATTN_WIDTH, D_MODEL), const),
            pl.BlockSpec((D_MODEL, D_MODEL), const),
            pl.BlockSpec((1, D_MODEL), const),
            pl.BlockSpec((1, D_MODEL), const),
        ],
        out_specs=pl.BlockSpec((1, tm, D_MODEL), tok),
        out_shape=jax.ShapeDtypeStruct((bsz, seq, D_MODEL), F32),
        compiler_params=pltpu.CompilerParams(
            dimension_semantics=("arbitrary", "arbitrary"),
            vmem_limit_bytes=_vmem_limit(tiles + weights + temps)),
        name="out_proj",
    )(mod, o, sa, ms, x, w_attn_out, w_o, ln_g, ln_b)


def kernel(x, c, w_ada, b_ada, w_in, conv_w, conv_b, rel_bias, w_attn_out, w_conv_out, w_o, ln_g, ln_b):
    bsz = x.shape[0]
    depth = w_in.shape[0]
    alpha = (2.0 * depth) ** 0.25
    bias = _bias_call(rel_bias)
    for layer in range(depth):
        mod = _ada_call(c, w_ada[layer], b_ada[layer]).reshape(bsz, 1, 3 * D_MODEL)
        qkv1, qkv2, qkv3, sg, sa, ms = _inproj_call(
            mod, x, w_in[layer].astype(BF16), w_conv_out[layer].astype(BF16),
            conv_w[layer], conv_b[layer][None, :])
        o = _attn_call(qkv1, qkv2, qkv3, bias, sg)
        x = _outproj_call(mod, o, sa, ms, x, w_attn_out[layer].astype(BF16),
                          w_o[layer].astype(BF16), ln_g[layer][None, :], ln_b[layer][None, :], alpha)
    return x
```

```python
import functools
import math

import numpy as np
import jax
import jax.numpy as jnp
from jax import lax
from jax.experimental import pallas as pl
from jax.experimental.pallas import tpu as pltpu

F32 = jnp.float32
BF16 = jnp.bfloat16

D_MODEL = 1024
HEAD_DIM = 128
HEADS_PER_GROUP = 4
DILATED_GROUPS = ((128, 1), (512, 4), (2048, 16))
N_GROUPS = len(DILATED_GROUPS)
N_ATTN_HEADS = N_GROUPS * HEADS_PER_GROUP
QKV_WIDTH = N_ATTN_HEADS * HEAD_DIM
ATTN_WIDTH = HEADS_PER_GROUP * HEAD_DIM
CONV_WIDTH = D_MODEL
CONV_K = 3
N_BUCKETS = 32
MAX_EXACT = 16
MAX_DISTANCE = 2048
BLOCK = 128
LN_EPS = 1e-5
NEG_INF = -1e30
Q_SCALE = HEAD_DIM ** -0.5

COL_Q = 0
COL_K = QKV_WIDTH
COL_V = 2 * QKV_WIDTH
COL_GATTN = 3 * QKV_WIDTH
COL_U = COL_GATTN + ATTN_WIDTH
COL_B = COL_U + CONV_WIDTH
COL_C = COL_B + CONV_WIDTH
COL_GCONV = COL_C + CONV_WIDTH
COL_MATTN = COL_GCONV + CONV_WIDTH
COL_MCONV = COL_MATTN + D_MODEL
N_COLS = COL_MCONV + D_MODEL

V7X_VMEM_BYTES = 64 * 1024 * 1024
SUBLANES = 8
INPROJ_TM = 512
OUTPROJ_TM = 1024
OUTPROJ_ROWS = 256
CONV_CHUNK = 256
ATTN_BLOCKS_PER_STAGE = 2


def _vmem_limit(estimate_bytes):
    return int(min(V7X_VMEM_BYTES - (4 << 20), estimate_bytes * 5 // 4 + (4 << 20)))


def _bucket_table():
    a = np.arange(BLOCK)[:, None]
    b = np.arange(2 * BLOCK)[None, :]
    steps = a + BLOCK - b
    out = []
    for window, dilation in DILATED_GROUPS:
        n_steps = window // dilation
        valid = (steps >= 0) & (steps <= n_steps)
        dist = np.maximum(steps, 0) * dilation
        n = np.maximum(dist, 1).astype(np.float32)
        large = MAX_EXACT + (np.log(n / np.float32(MAX_EXACT)) / np.float32(math.log(MAX_DISTANCE / MAX_EXACT))
                             * np.float32(N_BUCKETS - MAX_EXACT)).astype(np.int32)
        large = np.minimum(large, N_BUCKETS - 1)
        bucket = np.where(dist < MAX_EXACT, dist, large)
        out.append(np.where(valid, bucket, -1).astype(np.int32))
    return np.stack(out)


_BUCKETS = _bucket_table()


def _sigmoid(v):
    return 0.5 * jnp.tanh(0.5 * v) + 0.5


def _ada_kernel(c_ref, w_ref, b_ref, o_ref):
    c = c_ref[...]
    o_ref[...] = jnp.dot(c * _sigmoid(c), w_ref[...], preferred_element_type=F32) + b_ref[...]


def _ada_call(c, w_ada, b_ada):
    bsz = c.shape[0]
    return pl.pallas_call(
        _ada_kernel,
        out_shape=jax.ShapeDtypeStruct((bsz, 3 * D_MODEL), F32),
        name="ada_mod",
    )(c, w_ada, b_ada[None, :])


def _bias_kernel(rb_ref, bucket_ref, o_ref):
    for g in range(N_GROUPS):
        bk = bucket_ref[g]
        for j in range(HEADS_PER_GROUP):
            h = g * HEADS_PER_GROUP + j
            acc = jnp.full(bk.shape, NEG_INF, F32)
            for k in range(N_BUCKETS):
                acc = jnp.where(bk == k, rb_ref[k, h], acc)
            o_ref[h, 0] = acc
            o_ref[h, 1] = jnp.concatenate(
                [acc[:, BLOCK:], jnp.full((BLOCK, BLOCK), NEG_INF, F32)], axis=1)


def _bias_call(rel_bias):
    return pl.pallas_call(
        _bias_kernel,
        out_shape=jax.ShapeDtypeStruct((N_ATTN_HEADS, 2, BLOCK, 2 * BLOCK), F32),
        in_specs=[pl.BlockSpec(memory_space=pltpu.SMEM), pl.BlockSpec(memory_space=pltpu.VMEM)],
        out_specs=pl.BlockSpec(memory_space=pltpu.VMEM),
        name="bias_tables",
    )(rel_bias, jnp.asarray(_BUCKETS))


def _inproj_kernel(mod_ref, x_ref, w_ref, wco_ref, cw_ref, cb_ref,
                   qkv1_ref, qkv2_ref, qkv3_ref, sg_ref, sa_ref, ms_ref,
                   hs_ref, hb_ref, zbuf_ref, spre_ref, *, tm):
    i = pl.program_id(1)

    @pl.when(i == 0)
    def _():
        zbuf_ref[0:SUBLANES, :] = jnp.zeros((SUBLANES, CONV_WIDTH), F32)

    shift = mod_ref[0, :, 0:D_MODEL]
    scale1 = 1.0 + mod_ref[0, :, D_MODEL:2 * D_MODEL]
    h = x_ref[0] * scale1 + shift
    hb_ref[0] = h.astype(BF16)

    n_slab = D_MODEL // HEAD_DIM
    for s in range(n_slab):
        hs_ref[s] = h[:, s * HEAD_DIM:(s + 1) * HEAD_DIM]

    def by_residue(d):
        return jnp.concatenate(
            [jnp.concatenate([hs_ref[s, pl.ds(r, tm // d, stride=d), :] for s in range(n_slab)], axis=1)
             for r in range(d)], axis=0).astype(BF16)
    d4, d16 = DILATED_GROUPS[1][1], DILATED_GROUPS[2][1]
    hb_ref[1] = by_residue(d4)
    hb_ref[2] = by_residue(d16)

    def proj(g, c0, width):
        return jnp.dot(hb_ref[g], w_ref[:, c0:c0 + width], preferred_element_type=F32)

    def qkv(g, store):
        for t, col in enumerate((COL_Q, COL_K, COL_V)):
            r = proj(g, col + g * ATTN_WIDTH, ATTN_WIDTH)
            if t == 0:
                r = r * Q_SCALE
            r = r.astype(BF16)
            for j in range(HEADS_PER_GROUP):
                store(t, j, r[:, j * HEAD_DIM:(j + 1) * HEAD_DIM])

    def store1(t, j, v):
        qkv1_ref[0, t, j] = v
    qkv(0, store1)

    def store_sub(ref, d):
        def store(t, j, v):
            rows = tm // d
            for r in range(d):
                ref[0, t, j, r] = v[r * rows:(r + 1) * rows]
        return store
    qkv(1, store_sub(qkv2_ref, d4))
    qkv(2, store_sub(qkv3_ref, d16))

    ga = proj(0, COL_GATTN, ATTN_WIDTH)
    sgv = (ga * _sigmoid(ga)).astype(BF16)
    for j in range(HEADS_PER_GROUP):
        sg_ref[0, j] = sgv[:, j * HEAD_DIM:(j + 1) * HEAD_DIM]

    for c0 in range(0, D_MODEL, CONV_CHUNK):
        ma = proj(0, COL_MATTN + c0, CONV_CHUNK)
        sa_ref[0, :, c0:c0 + CONV_CHUNK] = _sigmoid(ma).astype(BF16)

    for c0 in range(0, CONV_WIDTH, CONV_CHUNK):
        cs = slice(c0, c0 + CONV_CHUNK)
        u = proj(0, COL_U + c0, CONV_CHUNK)
        cg = proj(0, COL_C + c0, CONV_CHUNK)
        z = cg * u
        zbuf_ref[SUBLANES:SUBLANES + tm, cs] = z
        zm1 = zbuf_ref[SUBLANES - 1:SUBLANES - 1 + tm, cs]
        zm2 = zbuf_ref[SUBLANES - 2:SUBLANES - 2 + tm, cs]
        y = cw_ref[0:1, cs] * zm2 + cw_ref[1:2, cs] * zm1 + cw_ref[2:3, cs] * z + cb_ref[:, cs]
        bg = proj(0, COL_B + c0, CONV_CHUNK)
        gc = proj(0, COL_GCONV + c0, CONV_CHUNK)
        spre_ref[:, cs] = (bg * y * (gc * _sigmoid(gc))).astype(BF16)
    zbuf_ref[0:SUBLANES, :] = zbuf_ref[tm:tm + SUBLANES, :]

    for c0 in range(0, D_MODEL, CONV_CHUNK):
        cs = slice(c0, c0 + CONV_CHUNK)
        so = jnp.dot(spre_ref[...], wco_ref[:, cs], preferred_element_type=F32)
        mc = proj(0, COL_MCONV + c0, CONV_CHUNK)
        ms_ref[0, :, cs] = (_sigmoid(mc) * so).astype(BF16)


def _inproj_call(mod, x, w_in, w_conv_out, conv_w, conv_b):
    bsz, seq, _ = x.shape
    tm = INPROJ_TM
    n_t = seq // tm
    d4, d16 = DILATED_GROUPS[1][1], DILATED_GROUPS[2][1]
    sub4, sub16 = seq // d4, seq // d16
    assert seq % tm == 0 and tm % (d16 * 16) == 0
    hg = HEADS_PER_GROUP

    in_specs = [
        pl.BlockSpec((1, 1, 3 * D_MODEL), lambda b, i: (b, 0, 0)),
        pl.BlockSpec((1, tm, D_MODEL), lambda b, i: (b, i, 0)),
        pl.BlockSpec((D_MODEL, N_COLS), lambda b, i: (0, 0), pipeline_mode=pl.Buffered(1)),
        pl.BlockSpec((CONV_WIDTH, D_MODEL), lambda b, i: (0, 0), pipeline_mode=pl.Buffered(1)),
        pl.BlockSpec((CONV_K, CONV_WIDTH), lambda b, i: (0, 0)),
        pl.BlockSpec((1, CONV_WIDTH), lambda b, i: (0, 0)),
    ]
    out_shape = [
        jax.ShapeDtypeStruct((bsz, 3, hg, seq, HEAD_DIM), BF16),
        jax.ShapeDtypeStruct((bsz, 3, hg, d4, sub4, HEAD_DIM), BF16),
        jax.ShapeDtypeStruct((bsz, 3, hg, d16, sub16, HEAD_DIM), BF16),
        jax.ShapeDtypeStruct((bsz, hg, seq, HEAD_DIM), BF16),
        jax.ShapeDtypeStruct((bsz, seq, D_MODEL), BF16),
        jax.ShapeDtypeStruct((bsz, seq, D_MODEL), BF16),
    ]
    out_specs = [
        pl.BlockSpec((1, 3, hg, tm, HEAD_DIM), lambda b, i: (b, 0, 0, i, 0)),
        pl.BlockSpec((1, 3, hg, d4, tm // d4, HEAD_DIM), lambda b, i: (b, 0, 0, 0, i, 0)),
        pl.BlockSpec((1, 3, hg, d16, tm // d16, HEAD_DIM), lambda b, i: (b, 0, 0, 0, i, 0)),
        pl.BlockSpec((1, hg, tm, HEAD_DIM), lambda b, i: (b, 0, i, 0)),
        pl.BlockSpec((1, tm, D_MODEL), lambda b, i: (b, i, 0)),
        pl.BlockSpec((1, tm, D_MODEL), lambda b, i: (b, i, 0)),
    ]
    weights = D_MODEL * N_COLS * 2 + CONV_WIDTH * D_MODEL * 2
    x_tiles = tm * D_MODEL * 4 * 2
    out_tiles = (3 * 3 * ATTN_WIDTH + ATTN_WIDTH + 2 * D_MODEL) * tm * 2 * 2
    scratch = (tm * D_MODEL * 4 + N_GROUPS * tm * D_MODEL * 2 + (tm + SUBLANES) * CONV_WIDTH * 4
               + tm * CONV_WIDTH * 2)
    return pl.pallas_call(
        functools.partial(_inproj_kernel, tm=tm),
        grid=(bsz, n_t),
        in_specs=in_specs,
        out_specs=out_specs,
        out_shape=out_shape,
        scratch_shapes=[pltpu.VMEM((D_MODEL // HEAD_DIM, tm, HEAD_DIM), F32),
                        pltpu.VMEM((N_GROUPS, tm, D_MODEL), BF16),
                        pltpu.VMEM((tm + SUBLANES, CONV_WIDTH), F32),
                        pltpu.VMEM((tm, CONV_WIDTH), BF16)],
        compiler_params=pltpu.CompilerParams(
            dimension_semantics=("arbitrary", "arbitrary"),
            vmem_limit_bytes=_vmem_limit(weights + x_tiles + out_tiles + scratch)),
        name="in_proj",
    )(mod, x, w_in, w_conv_out, conv_w, conv_b)


def _nt_dot(q, k):
    return lax.dot_general(q, k, (((1,), (1,)), ((), ())), preferred_element_type=F32)


def _softmax_pv(s, v):
    m = jnp.max(s, axis=1, keepdims=True)
    p = jnp.exp(s - m)
    l = jnp.sum(p, axis=1, keepdims=True)
    acc = jnp.dot(p.astype(BF16), v, preferred_element_type=F32)
    return acc * (1.0 / l), m + jnp.log(l)


def _attn_kernel(qkv1_ref, qkv2_ref, qkv3_ref, bias_ref, sg_ref, o_ref,
                 sa1_ref, sa2_ref, sa3_ref, sb1_ref, sb2_ref, sb3_ref,
                 o1_ref, o2_ref, o3_ref, l1_ref, l2_ref, l3_ref, t3o_ref, t3l_ref, *, seq):
    d4, d16 = DILATED_GROUPS[1][1], DILATED_GROUPS[2][1]
    per = ATTN_BLOCKS_PER_STAGE
    n_it = seq // BLOCK // per
    blk4 = seq // d4 // BLOCK
    assert seq // d16 == BLOCK and blk4 & (blk4 - 1) == 0 and n_it % 2 == 0 and n_it * per * BLOCK == seq
    assert d16 == d4 * d4 and d4 == blk4
    shift4 = blk4.bit_length() - 1
    sub4 = seq // d4

    def rows(start, size):
        if isinstance(start, int):
            return pl.ds(start, size)
        return pl.ds(pl.multiple_of(start, BLOCK), size)

    def plan(it, u):
        it = it * per + u
        if isinstance(it, int):
            f1, nb, r4 = int(it == 0), it & (blk4 - 1), it >> shift4
            f2 = int(nb == 0)
        else:
            f1 = (it == 0).astype(jnp.int32)
            nb = jnp.bitwise_and(it, blk4 - 1)
            f2 = (nb == 0).astype(jnp.int32)
            r4 = lax.shift_right_logical(it, shift4)
        return dict(q=it * BLOCK, k1=(it - 1 + f1) * BLOCK, f1=f1, k2=(it - 1 + f2) * BLOCK, f2=f2,
                    out2=nb * (BLOCK * d4) + r4, out3=nb * sub4 + r4)

    def scores(it, s_refs):
        for u in range(per):
            p = plan(it, u)
            q = rows(p["q"], BLOCK)
            s_refs[0][u] = _nt_dot(qkv1_ref[0, 0, 0, q, :], qkv1_ref[0, 1, 0, rows(p["k1"], 2 * BLOCK), :]) \
                + bias_ref[0, 0, p["f1"]]
            s_refs[1][u] = _nt_dot(qkv2_ref[0, 0, 0, q, :], qkv2_ref[0, 1, 0, rows(p["k2"], 2 * BLOCK), :]) \
                + bias_ref[1, 0, p["f2"]]
            s_refs[2][u] = _nt_dot(qkv3_ref[0, 0, 0, q, :], qkv3_ref[0, 1, 0, q, :]) \
                + bias_ref[2, 0, 1, :, 0:BLOCK]

    def put(o_dst, l_dst, idx, o, lse):
        o_dst[idx, :] = o
        l_dst[idx, :] = jnp.broadcast_to(lse, (BLOCK, HEAD_DIM))

    def finish(it, s_refs):
        for u in range(per):
            p = plan(it, u)
            q = rows(p["q"], BLOCK)
            o, lse = _softmax_pv(s_refs[0][u], qkv1_ref[0, 2, 0, rows(p["k1"], 2 * BLOCK), :])
            put(o1_ref, l1_ref, q, o, lse)
            o, lse = _softmax_pv(s_refs[1][u], qkv2_ref[0, 2, 0, rows(p["k2"], 2 * BLOCK), :])
            put(o2_ref, l2_ref, pl.ds(p["out2"], BLOCK, stride=d4), o, lse)
            o, lse = _softmax_pv(s_refs[2][u], qkv3_ref[0, 2, 0, q, :])
            put(t3o_ref, t3l_ref, pl.ds(p["out3"], BLOCK, stride=d4), o, lse)

    s_a = (sa1_ref, sa2_ref, sa3_ref)
    s_b = (sb1_ref, sb2_ref, sb3_ref)
    scores(0, s_a)

    def body(i, carry):
        scores(2 * i + 1, s_b)
        finish(2 * i, s_a)
        scores(2 * i + 2, s_a)
        finish(2 * i + 1, s_b)
        return carry
    lax.fori_loop(0, n_it // 2 - 1, body, 0)
    scores(n_it - 1, s_b)
    finish(n_it - 2, s_a)
    finish(n_it - 1, s_b)

    for c in range(d4):
        for p0 in range(0, sub4, BLOCK):
            src = slice(c * sub4 + p0, c * sub4 + p0 + BLOCK)
            dst = pl.ds(d4 * p0 + c, BLOCK, stride=d4)
            o3_ref[dst, :] = t3o_ref[src, :]
            l3_ref[dst, :] = t3l_ref[src, :]

    chunk = 2 * BLOCK
    for c0 in range(0, seq, chunk):
        rws = slice(c0, c0 + chunk)
        la, lb, lc = l1_ref[rws, :], l2_ref[rws, :], l3_ref[rws, :]
        m = jnp.maximum(jnp.maximum(la, lb), lc)
        wa, wb, wc = jnp.exp(la - m), jnp.exp(lb - m), jnp.exp(lc - m)
        num = wa * o1_ref[rws, :] + wb * o2_ref[rws, :] + wc * o3_ref[rws, :]
        gate = sg_ref[0, 0, rws, :].astype(F32)
        o_ref[0, 0, rws, :] = (num * (1.0 / (wa + wb + wc)) * gate).astype(BF16)


def _attn_call(qkv1, qkv2, qkv3, bias, sg):
    bsz, _, hg, seq, _ = qkv1.shape
    qkv2 = qkv2.reshape(qkv1.shape)
    qkv3 = qkv3.reshape(qkv1.shape)
    bias = bias.reshape(N_GROUPS, hg, 2, BLOCK, 2 * BLOCK)
    in_tiles = (3 * 3 + 1) * seq * HEAD_DIM * 2 * 2 + N_GROUPS * 2 * BLOCK * 2 * BLOCK * 4 * 2
    out_tiles = seq * HEAD_DIM * 2 * 2
    per = ATTN_BLOCKS_PER_STAGE
    score_shapes = [(per, BLOCK, 2 * BLOCK), (per, BLOCK, 2 * BLOCK), (per, BLOCK, BLOCK)] * 2
    n_rowbufs = 8
    scratch = n_rowbufs * seq * HEAD_DIM * 4 + sum(p * a * b * 4 for p, a, b in score_shapes)
    qkv_spec = pl.BlockSpec((1, 3, 1, seq, HEAD_DIM), lambda b, j: (b, 0, j, 0, 0))
    return pl.pallas_call(
        functools.partial(_attn_kernel, seq=seq),
        grid=(bsz, hg),
        in_specs=[
            qkv_spec, qkv_spec, qkv_spec,
            pl.BlockSpec((N_GROUPS, 1, 2, BLOCK, 2 * BLOCK), lambda b, j: (0, j, 0, 0, 0)),
            pl.BlockSpec((1, 1, seq, HEAD_DIM), lambda b, j: (b, j, 0, 0)),
        ],
        out_specs=pl.BlockSpec((1, 1, seq, HEAD_DIM), lambda b, j: (b, j, 0, 0)),
        out_shape=jax.ShapeDtypeStruct((bsz, hg, seq, HEAD_DIM), BF16),
        scratch_shapes=[pltpu.VMEM(s, F32) for s in score_shapes]
        + [pltpu.VMEM((seq, HEAD_DIM), F32) for _ in range(n_rowbufs)],
        compiler_params=pltpu.CompilerParams(
            dimension_semantics=("arbitrary", "arbitrary"),
            vmem_limit_bytes=_vmem_limit(in_tiles + out_tiles + scratch)),
        name="dilated_attn",
    )(qkv1, qkv2, qkv3, bias, sg)


def _outproj_kernel(mod_ref, o_ref, sa_ref, ms_ref, x_ref, wao_ref, wo_ref,
                    lng_ref, lnb_ref, out_ref, *, alpha, tm):
    gate1 = 1.0 + mod_ref[0, :, 2 * D_MODEL:3 * D_MODEL]
    for r0 in range(0, tm, OUTPROJ_ROWS):
        rws = slice(r0, r0 + OUTPROJ_ROWS)
        a_in = jnp.concatenate([o_ref[0, j, rws, :] for j in range(HEADS_PER_GROUP)], axis=1)
        a_out = jnp.dot(a_in, wao_ref[...], preferred_element_type=F32)
        merged = (sa_ref[0, rws, :].astype(F32) * a_out + ms_ref[0, rws, :].astype(F32)).astype(BF16)
        y = jnp.dot(merged, wo_ref[...], preferred_element_type=F32)
        v = alpha * x_ref[0, rws, :] + gate1 * y
        mu = jnp.mean(v, axis=-1, keepdims=True)
        dv = v - mu
        var = jnp.mean(dv * dv, axis=-1, keepdims=True)
        out_ref[0, rws, :] = dv * lax.rsqrt(var + LN_EPS) * lng_ref[...] + lnb_ref[...]


def _outproj_call(mod, o, sa, ms, x, w_attn_out, w_o, ln_g, ln_b, alpha):
    bsz, seq, _ = x.shape
    tm = OUTPROJ_TM
    assert seq % tm == 0 and tm % OUTPROJ_ROWS == 0
    hg = HEADS_PER_GROUP
    tok = lambda b, i: (b, i, 0)
    const = lambda b, i: (0, 0)
    tiles = (ATTN_WIDTH * 2 + 2 * D_MODEL * 2 + 2 * D_MODEL * 4) * tm * 2
    weights = (ATTN_WIDTH + D_MODEL) * D_MODEL * 2 * 2
    temps = 4 * tm * D_MODEL * 4
    return pl.pallas_call(
        functools.partial(_outproj_kernel, alpha=alpha, tm=tm),
        grid=(bsz, seq // tm),
        in_specs=[
            pl.BlockSpec((1, 1, 3 * D_MODEL), lambda b, i: (b, 0, 0)),
            pl.BlockSpec((1, hg, tm, HEAD_DIM), lambda b, i: (b, 0, i, 0)),
            pl.BlockSpec((1, tm, D_MODEL), tok),
            pl.BlockSpec((1, tm, D_MODEL), tok),
            pl.BlockSpec((1, tm, D_MODEL), tok),
            pl.BlockSpec((ATTN_WIDTH, D_MODEL), const),
            pl.BlockSpec((D_MODEL, D_MODEL), const),
            pl.BlockSpec((1, D_MODEL), const),
            pl.BlockSpec((1, D_MODEL), const),
        ],
        out_specs=pl.BlockSpec((1, tm, D_MODEL), tok),
        out_shape=jax.ShapeDtypeStruct((bsz, seq, D_MODEL), F32),
        compiler_params=pltpu.CompilerParams(
            dimension_semantics=("arbitrary", "arbitrary"),
            vmem_limit_bytes=_vmem_limit(tiles + weights + temps)),
        name="out_proj",
    )(mod, o, sa, ms, x, w_attn_out, w_o, ln_g, ln_b)


def kernel(x, c, w_ada, b_ada, w_in, conv_w, conv_b, rel_bias, w_attn_out, w_conv_out, w_o, ln_g, ln_b):
    bsz = x.shape[0]
    depth = w_in.shape[0]
    alpha = (2.0 * depth) ** 0.25
    bias = _bias_call(rel_bias)
    for layer in range(depth):
        mod = _ada_call(c, w_ada[layer], b_ada[layer]).reshape(bsz, 1, 3 * D_MODEL)
        qkv1, qkv2, qkv3, sg, sa, ms = _inproj_call(
            mod, x, w_in[layer].astype(BF16), w_conv_out[layer].astype(BF16),
            conv_w[layer], conv_b[layer][None, :])
        o = _attn_call(qkv1, qkv2, qkv3, bias, sg)
        x = _outproj_call(mod, o, sa, ms, x, w_attn_out[layer].astype(BF16),
                          w_o[layer].astype(BF16), ln_g[layer][None, :], ln_b[layer][None, :], alpha)
    return x
```

```python
import functools
import math

import numpy as np
import jax
import jax.numpy as jnp
from jax import lax
from jax.experimental import pallas as pl
from jax.experimental.pallas import tpu as pltpu

F32 = jnp.float32
BF16 = jnp.bfloat16

D_MODEL = 1024
HEAD_DIM = 128
HEADS_PER_GROUP = 4
DILATED_GROUPS = ((128, 1), (512, 4), (2048, 16))
N_GROUPS = len(DILATED_GROUPS)
N_ATTN_HEADS = N_GROUPS * HEADS_PER_GROUP
QKV_WIDTH = N_ATTN_HEADS * HEAD_DIM
ATTN_WIDTH = HEADS_PER_GROUP * HEAD_DIM
CONV_WIDTH = D_MODEL
CONV_K = 3
N_BUCKETS = 32
MAX_EXACT = 16
MAX_DISTANCE = 2048
BLOCK = 128
LN_EPS = 1e-5
NEG_INF = -1e30
LOG2E = math.log2(math.e)
Q_SCALE = HEAD_DIM ** -0.5 * LOG2E

COL_Q = 0
COL_K = QKV_WIDTH
COL_V = 2 * QKV_WIDTH
COL_GATTN = 3 * QKV_WIDTH
COL_U = COL_GATTN + ATTN_WIDTH
COL_B = COL_U + CONV_WIDTH
COL_C = COL_B + CONV_WIDTH
COL_GCONV = COL_C + CONV_WIDTH
COL_MATTN = COL_GCONV + CONV_WIDTH
COL_MCONV = COL_MATTN + D_MODEL
N_COLS = COL_MCONV + D_MODEL

V7X_VMEM_BYTES = 64 * 1024 * 1024
SUBLANES = 8
INPROJ_TM = 512
OUTPROJ_TM = 1024
OUTPROJ_ROWS = 256
CONV_CHUNK = 256
ATTN_BLOCKS_PER_STAGE = 2
MERGE_ROWS = 32


def _vmem_limit(estimate_bytes):
    return int(min(V7X_VMEM_BYTES - (4 << 20), estimate_bytes * 5 // 4 + (4 << 20)))


def _bucket_table():
    a = np.arange(BLOCK)[:, None]
    b = np.arange(2 * BLOCK)[None, :]
    steps = a + BLOCK - b
    out = []
    for window, dilation in DILATED_GROUPS:
        n_steps = window // dilation
        valid = (steps >= 0) & (steps <= n_steps)
        dist = np.maximum(steps, 0) * dilation
        n = np.maximum(dist, 1).astype(np.float32)
        large = MAX_EXACT + (np.log(n / np.float32(MAX_EXACT)) / np.float32(math.log(MAX_DISTANCE / MAX_EXACT))
                             * np.float32(N_BUCKETS - MAX_EXACT)).astype(np.int32)
        large = np.minimum(large, N_BUCKETS - 1)
        bucket = np.where(dist < MAX_EXACT, dist, large)
        out.append(np.where(valid, bucket, -1).astype(np.int32))
    return np.stack(out)


_BUCKETS = _bucket_table()


def _sigmoid(v):
    return 0.5 * jnp.tanh(0.5 * v) + 0.5


def _ada_kernel(c_ref, w_ref, b_ref, o_ref):
    c = c_ref[...]
    o_ref[...] = jnp.dot(c * _sigmoid(c), w_ref[...], preferred_element_type=F32) + b_ref[...]


def _ada_call(c, w_ada, b_ada):
    bsz = c.shape[0]
    return pl.pallas_call(
        _ada_kernel,
        out_shape=jax.ShapeDtypeStruct((bsz, 3 * D_MODEL), F32),
        name="ada_mod",
    )(c, w_ada, b_ada[None, :])


def _bias_kernel(rb_ref, bucket_ref, o_ref):
    for g in range(N_GROUPS):
        bk = bucket_ref[g]
        for j in range(HEADS_PER_GROUP):
            h = g * HEADS_PER_GROUP + j
            acc = jnp.full(bk.shape, NEG_INF, F32)
            for k in range(N_BUCKETS):
                acc = jnp.where(bk == k, rb_ref[k, h] * LOG2E, acc)
            o_ref[h, 0] = acc
            o_ref[h, 1] = jnp.concatenate(
                [acc[:, BLOCK:], jnp.full((BLOCK, BLOCK), NEG_INF, F32)], axis=1)


def _bias_call(rel_bias):
    return pl.pallas_call(
        _bias_kernel,
        out_shape=jax.ShapeDtypeStruct((N_ATTN_HEADS, 2, BLOCK, 2 * BLOCK), F32),
        in_specs=[pl.BlockSpec(memory_space=pltpu.SMEM), pl.BlockSpec(memory_space=pltpu.VMEM)],
        out_specs=pl.BlockSpec(memory_space=pltpu.VMEM),
        name="bias_tables",
    )(rel_bias, jnp.asarray(_BUCKETS))


def _inproj_kernel(mod_ref, x_ref, w_ref, wco_ref, cw_ref, cb_ref,
                   qkv1_ref, qkv2_ref, qkv3_ref, sg_ref, sa_ref, ms_ref,
                   hs_ref, hb_ref, zbuf_ref, spre_ref, *, tm):
    i = pl.program_id(1)

    @pl.when(i == 0)
    def _():
        zbuf_ref[0:SUBLANES, :] = jnp.zeros((SUBLANES, CONV_WIDTH), F32)

    shift = mod_ref[0, :, 0:D_MODEL]
    scale1 = 1.0 + mod_ref[0, :, D_MODEL:2 * D_MODEL]
    h = x_ref[0] * scale1 + shift
    hb_ref[0] = h.astype(BF16)

    n_slab = D_MODEL // HEAD_DIM
    for s in range(n_slab):
        hs_ref[s] = h[:, s * HEAD_DIM:(s + 1) * HEAD_DIM]

    def by_residue(d):
        return jnp.concatenate(
            [jnp.concatenate([hs_ref[s, pl.ds(r, tm // d, stride=d), :] for s in range(n_slab)], axis=1)
             for r in range(d)], axis=0).astype(BF16)
    d4, d16 = DILATED_GROUPS[1][1], DILATED_GROUPS[2][1]
    hb_ref[1] = by_residue(d4)
    hb_ref[2] = by_residue(d16)

    def proj(g, c0, width):
        return jnp.dot(hb_ref[g], w_ref[:, c0:c0 + width], preferred_element_type=F32)

    def qkv(g, store):
        for t, col in enumerate((COL_Q, COL_K, COL_V)):
            r = proj(g, col + g * ATTN_WIDTH, ATTN_WIDTH)
            if t == 0:
                r = r * Q_SCALE
            r = r.astype(BF16)
            for j in range(HEADS_PER_GROUP):
                store(t, j, r[:, j * HEAD_DIM:(j + 1) * HEAD_DIM])

    def store1(t, j, v):
        qkv1_ref[0, t, j] = v
    qkv(0, store1)

    def store_sub(ref, d):
        def store(t, j, v):
            rows = tm // d
            for r in range(d):
                ref[0, t, j, r] = v[r * rows:(r + 1) * rows]
        return store
    qkv(1, store_sub(qkv2_ref, d4))
    qkv(2, store_sub(qkv3_ref, d16))

    ga = proj(0, COL_GATTN, ATTN_WIDTH)
    sgv = (ga * _sigmoid(ga)).astype(BF16)
    for j in range(HEADS_PER_GROUP):
        sg_ref[0, j] = sgv[:, j * HEAD_DIM:(j + 1) * HEAD_DIM]

    for c0 in range(0, D_MODEL, CONV_CHUNK):
        ma = proj(0, COL_MATTN + c0, CONV_CHUNK)
        sa_ref[0, :, c0:c0 + CONV_CHUNK] = _sigmoid(ma).astype(BF16)

    for c0 in range(0, CONV_WIDTH, CONV_CHUNK):
        cs = slice(c0, c0 + CONV_CHUNK)
        u = proj(0, COL_U + c0, CONV_CHUNK)
        cg = proj(0, COL_C + c0, CONV_CHUNK)
        z = cg * u
        zbuf_ref[SUBLANES:SUBLANES + tm, cs] = z
        zm1 = zbuf_ref[SUBLANES - 1:SUBLANES - 1 + tm, cs]
        zm2 = zbuf_ref[SUBLANES - 2:SUBLANES - 2 + tm, cs]
        y = cw_ref[0:1, cs] * zm2 + cw_ref[1:2, cs] * zm1 + cw_ref[2:3, cs] * z + cb_ref[:, cs]
        bg = proj(0, COL_B + c0, CONV_CHUNK)
        gc = proj(0, COL_GCONV + c0, CONV_CHUNK)
        spre_ref[:, cs] = (bg * y * (gc * _sigmoid(gc))).astype(BF16)
    zbuf_ref[0:SUBLANES, :] = zbuf_ref[tm:tm + SUBLANES, :]

    for c0 in range(0, D_MODEL, CONV_CHUNK):
        cs = slice(c0, c0 + CONV_CHUNK)
        so = jnp.dot(spre_ref[...], wco_ref[:, cs], preferred_element_type=F32)
        mc = proj(0, COL_MCONV + c0, CONV_CHUNK)
        ms_ref[0, :, cs] = (_sigmoid(mc) * so).astype(BF16)


def _inproj_call(mod, x, w_in, w_conv_out, conv_w, conv_b):
    bsz, seq, _ = x.shape
    tm = INPROJ_TM
    n_t = seq // tm
    d4, d16 = DILATED_GROUPS[1][1], DILATED_GROUPS[2][1]
    sub4, sub16 = seq // d4, seq // d16
    assert seq % tm == 0 and tm % (d16 * 16) == 0
    hg = HEADS_PER_GROUP

    in_specs = [
        pl.BlockSpec((1, 1, 3 * D_MODEL), lambda b, i: (b, 0, 0)),
        pl.BlockSpec((1, tm, D_MODEL), lambda b, i: (b, i, 0)),
        pl.BlockSpec((D_MODEL, N_COLS), lambda b, i: (0, 0), pipeline_mode=pl.Buffered(1)),
        pl.BlockSpec((CONV_WIDTH, D_MODEL), lambda b, i: (0, 0), pipeline_mode=pl.Buffered(1)),
        pl.BlockSpec((CONV_K, CONV_WIDTH), lambda b, i: (0, 0)),
        pl.BlockSpec((1, CONV_WIDTH), lambda b, i: (0, 0)),
    ]
    out_shape = [
        jax.ShapeDtypeStruct((bsz, 3, hg, seq, HEAD_DIM), BF16),
        jax.ShapeDtypeStruct((bsz, 3, hg, d4, sub4, HEAD_DIM), BF16),
        jax.ShapeDtypeStruct((bsz, 3, hg, d16, sub16, HEAD_DIM), BF16),
        jax.ShapeDtypeStruct((bsz, hg, seq, HEAD_DIM), BF16),
        jax.ShapeDtypeStruct((bsz, seq, D_MODEL), BF16),
        jax.ShapeDtypeStruct((bsz, seq, D_MODEL), BF16),
    ]
    out_specs = [
        pl.BlockSpec((1, 3, hg, tm, HEAD_DIM), lambda b, i: (b, 0, 0, i, 0)),
        pl.BlockSpec((1, 3, hg, d4, tm // d4, HEAD_DIM), lambda b, i: (b, 0, 0, 0, i, 0)),
        pl.BlockSpec((1, 3, hg, d16, tm // d16, HEAD_DIM), lambda b, i: (b, 0, 0, 0, i, 0)),
        pl.BlockSpec((1, hg, tm, HEAD_DIM), lambda b, i: (b, 0, i, 0)),
        pl.BlockSpec((1, tm, D_MODEL), lambda b, i: (b, i, 0)),
        pl.BlockSpec((1, tm, D_MODEL), lambda b, i: (b, i, 0)),
    ]
    weights = D_MODEL * N_COLS * 2 + CONV_WIDTH * D_MODEL * 2
    x_tiles = tm * D_MODEL * 4 * 2
    out_tiles = (3 * 3 * ATTN_WIDTH + ATTN_WIDTH + 2 * D_MODEL) * tm * 2 * 2
    scratch = (tm * D_MODEL * 4 + N_GROUPS * tm * D_MODEL * 2 + (tm + SUBLANES) * CONV_WIDTH * 4
               + tm * CONV_WIDTH * 2)
    return pl.pallas_call(
        functools.partial(_inproj_kernel, tm=tm),
        grid=(bsz, n_t),
        in_specs=in_specs,
        out_specs=out_specs,
        out_shape=out_shape,
        scratch_shapes=[pltpu.VMEM((D_MODEL // HEAD_DIM, tm, HEAD_DIM), F32),
                        pltpu.VMEM((N_GROUPS, tm, D_MODEL), BF16),
                        pltpu.VMEM((tm + SUBLANES, CONV_WIDTH), F32),
                        pltpu.VMEM((tm, CONV_WIDTH), BF16)],
        compiler_params=pltpu.CompilerParams(
            dimension_semantics=("arbitrary", "arbitrary"),
            vmem_limit_bytes=_vmem_limit(weights + x_tiles + out_tiles + scratch)),
        name="in_proj",
    )(mod, x, w_in, w_conv_out, conv_w, conv_b)


def _nt_dot(q, k):
    return lax.dot_general(q, k, (((1,), (1,)), ((), ())), preferred_element_type=F32)


def _softmax_pv(s, v):
    m = jnp.max(s, axis=1, keepdims=True)
    p = jnp.exp2(s - m)
    v_ext = jnp.concatenate([v, jnp.ones((v.shape[0], HEAD_DIM), BF16)], axis=1)
    acc = jnp.dot(p.astype(BF16), v_ext, preferred_element_type=F32)
    return acc[:, :HEAD_DIM], m, acc[:, HEAD_DIM:]


def _attn_kernel(qkv1_ref, qkv2_ref, qkv3_ref, bias_ref, sg_ref, o_ref,
                 sa1_ref, sa2_ref, sa3_ref, sb1_ref, sb2_ref, sb3_ref, *row_refs, seq):
    res1, res2, res3, stage3 = (row_refs[3 * g:3 * g + 3] for g in range(N_GROUPS + 1))
    step = pl.program_id(0)
    par = jnp.bitwise_and(step, 1)
    opar = 1 - par

    @pl.when(step == 0)
    def _():
        for r in res1 + res2 + res3:
            r[1] = jnp.ones((seq, HEAD_DIM), F32)

    d4, d16 = DILATED_GROUPS[1][1], DILATED_GROUPS[2][1]
    per = ATTN_BLOCKS_PER_STAGE
    n_it = seq // BLOCK // per
    blk4 = seq // d4 // BLOCK
    assert seq // d16 == BLOCK and blk4 & (blk4 - 1) == 0 and n_it % 2 == 0 and n_it * per * BLOCK == seq
    assert d16 == d4 * d4 and d4 == blk4
    shift4 = blk4.bit_length() - 1
    sub4 = seq // d4

    def rows(start, size, align=BLOCK):
        if isinstance(start, int):
            return pl.ds(start, size)
        return pl.ds(pl.multiple_of(start, align), size)

    def plan(it, u):
        it = it * per + u
        if isinstance(it, int):
            f1, nb, r4 = int(it == 0), it & (blk4 - 1), it >> shift4
            f2 = int(nb == 0)
        else:
            f1 = (it == 0).astype(jnp.int32)
            nb = jnp.bitwise_and(it, blk4 - 1)
            f2 = (nb == 0).astype(jnp.int32)
            r4 = lax.shift_right_logical(it, shift4)
        return dict(q=it * BLOCK, k1=(it - 1 + f1) * BLOCK, f1=f1, k2=(it - 1 + f2) * BLOCK, f2=f2,
                    out2=nb * (BLOCK * d4) + r4, out3=nb * sub4 + r4)

    def scores(it, s_refs):
        for u in range(per):
            p = plan(it, u)
            q = rows(p["q"], BLOCK)
            s_refs[0][u] = _nt_dot(qkv1_ref[0, 0, 0, q, :], qkv1_ref[0, 1, 0, rows(p["k1"], 2 * BLOCK), :]) \
                + bias_ref[0, 0, p["f1"]]
            s_refs[1][u] = _nt_dot(qkv2_ref[0, 0, 0, q, :], qkv2_ref[0, 1, 0, rows(p["k2"], 2 * BLOCK), :]) \
                + bias_ref[1, 0, p["f2"]]
            s_refs[2][u] = _nt_dot(qkv3_ref[0, 0, 0, q, :], qkv3_ref[0, 1, 0, q, :]) \
                + bias_ref[2, 0, 1, :, 0:BLOCK]

    def put(dst, lead, idx, acc, m, l):
        dst[0][(*lead, idx, slice(None))] = acc
        dst[1][(*lead, idx, slice(None))] = jnp.broadcast_to(m, (BLOCK, HEAD_DIM))
        dst[2][(*lead, idx, slice(None))] = l

    def finish(it, s_refs):
        for u in range(per):
            p = plan(it, u)
            q = rows(p["q"], BLOCK)
            put(res1, (par,), q, *_softmax_pv(s_refs[0][u], qkv1_ref[0, 2, 0, rows(p["k1"], 2 * BLOCK), :]))
            put(res2, (par,), pl.ds(p["out2"], BLOCK, stride=d4),
                *_softmax_pv(s_refs[1][u], qkv2_ref[0, 2, 0, rows(p["k2"], 2 * BLOCK), :]))
            put(stage3, (), pl.ds(p["out3"], BLOCK, stride=d4),
                *_softmax_pv(s_refs[2][u], qkv3_ref[0, 2, 0, q, :]))

    def merge(base, n_rows):
        for c0 in range(0, n_rows, MERGE_ROWS):
            rws = rows(base + c0, MERGE_ROWS, MERGE_ROWS)
            ms = [r[1][opar, rws, :] for r in (res1, res2, res3)]
            m = jnp.maximum(jnp.maximum(ms[0], ms[1]), ms[2])
            es = [jnp.exp2(mg - m) for mg in ms]
            num = sum(e * r[0][opar, rws, :] for e, r in zip(es, (res1, res2, res3)))
            den = sum(e * r[2][opar, rws, :] for e, r in zip(es, (res1, res2, res3)))
            o_ref[0, 0, rws, :] = (num * (1.0 / den)).astype(BF16) * sg_ref[0, 0, rws, :]

    sec = seq // n_it
    s_a = (sa1_ref, sa2_ref, sa3_ref)
    s_b = (sb1_ref, sb2_ref, sb3_ref)
    merge(0, sec // 2)
    scores(0, s_a)

    def body(i, carry):
        base = sec // 2 + 2 * i * sec
        merge(base, sec)
        scores(2 * i + 1, s_b)
        finish(2 * i, s_a)
        merge(base + sec, sec)
        scores(2 * i + 2, s_a)
        finish(2 * i + 1, s_b)
        return carry
    lax.fori_loop(0, n_it // 2 - 1, body, 0)
    tail = sec // 2 + (n_it - 2) * sec
    merge(tail, seq - tail)
    scores(n_it - 1, s_b)
    finish(n_it - 2, s_a)
    finish(n_it - 1, s_b)

    for c in range(d4):
        for p0 in range(0, sub4, BLOCK):
            src = slice(c * sub4 + p0, c * sub4 + p0 + BLOCK)
            dst = pl.ds(d4 * p0 + c, BLOCK, stride=d4)
            for k in range(3):
                res3[k][par, dst, :] = stage3[k][src, :]


def _attn_call(qkv1, qkv2, qkv3, bias, sg):
    bsz, _, hg, seq, _ = qkv1.shape
    qkv2 = qkv2.reshape(qkv1.shape)
    qkv3 = qkv3.reshape(qkv1.shape)
    bias = bias.reshape(N_GROUPS, hg, 2, BLOCK, 2 * BLOCK)
    n_slots = bsz * hg

    def cur(s):
        c = jnp.minimum(s, n_slots - 1)
        return c // hg, c % hg

    def prev(s):
        p = jnp.maximum(s - 1, 0)
        return p // hg, p % hg

    def qkv_map(s):
        b, j = cur(s)
        return (b, 0, j, 0, 0)

    def bias_map(s):
        return (0, cur(s)[1], 0, 0, 0)

    def lag_map(s):
        b, j = prev(s)
        return (b, j, 0, 0)

    in_tiles = (3 * 3 + 1) * seq * HEAD_DIM * 2 * 2 + N_GROUPS * 2 * BLOCK * 2 * BLOCK * 4 * 2
    out_tiles = seq * HEAD_DIM * 2 * 2
    per = ATTN_BLOCKS_PER_STAGE
    score_shapes = [(per, BLOCK, 2 * BLOCK), (per, BLOCK, 2 * BLOCK), (per, BLOCK, BLOCK)] * 2
    row_shapes = [(2, seq, HEAD_DIM)] * (3 * N_GROUPS) + [(seq, HEAD_DIM)] * 3
    scratch = sum(math.prod(sh) * 4 for sh in score_shapes + row_shapes)
    qkv_spec = pl.BlockSpec((1, 3, 1, seq, HEAD_DIM), qkv_map)
    return pl.pallas_call(
        functools.partial(_attn_kernel, seq=seq),
        grid=(n_slots + 1,),
        in_specs=[
            qkv_spec, qkv_spec, qkv_spec,
            pl.BlockSpec((N_GROUPS, 1, 2, BLOCK, 2 * BLOCK), bias_map),
            pl.BlockSpec((1, 1, seq, HEAD_DIM), lag_map),
        ],
        out_specs=pl.BlockSpec((1, 1, seq, HEAD_DIM), lag_map),
        out_shape=jax.ShapeDtypeStruct((bsz, hg, seq, HEAD_DIM), BF16),
        scratch_shapes=[pltpu.VMEM(sh, F32) for sh in score_shapes + row_shapes],
        compiler_params=pltpu.CompilerParams(
            dimension_semantics=("arbitrary",),
            vmem_limit_bytes=_vmem_limit(in_tiles + out_tiles + scratch)),
        name="dilated_attn",
    )(qkv1, qkv2, qkv3, bias, sg)


def _outproj_kernel(mod_ref, o_ref, sa_ref, ms_ref, x_ref, wao_ref, wo_ref,
                    lng_ref, lnb_ref, out_ref, *, alpha, tm):
    gate1 = 1.0 + mod_ref[0, :, 2 * D_MODEL:3 * D_MODEL]
    for r0 in range(0, tm, OUTPROJ_ROWS):
        rws = slice(r0, r0 + OUTPROJ_ROWS)
        a_in = jnp.concatenate([o_ref[0, j, rws, :] for j in range(HEADS_PER_GROUP)], axis=1)
        a_out = jnp.dot(a_in, wao_ref[...], preferred_element_type=F32)
        merged = (sa_ref[0, rws, :].astype(F32) * a_out + ms_ref[0, rws, :].astype(F32)).astype(BF16)
        y = jnp.dot(merged, wo_ref[...], preferred_element_type=F32)
        v = alpha * x_ref[0, rws, :] + gate1 * y
        mu = jnp.mean(v, axis=-1, keepdims=True)
        dv = v - mu
        var = jnp.mean(dv * dv, axis=-1, keepdims=True)
        out_ref[0, rws, :] = dv * lax.rsqrt(var + LN_EPS) * lng_ref[...] + lnb_ref[...]


def _outproj_call(mod, o, sa, ms, x, w_attn_out, w_o, ln_g, ln_b, alpha):
    bsz, seq, _ = x.shape
    tm = OUTPROJ_TM
    assert seq % tm == 0 and tm % OUTPROJ_ROWS == 0
    hg = HEADS_PER_GROUP
    tok = lambda b, i: (b, i, 0)
    const = lambda b, i: (0, 0)
    tiles = (ATTN_WIDTH * 2 + 2 * D_MODEL * 2 + 2 * D_MODEL * 4) * tm * 2
    weights = (ATTN_WIDTH + D_MODEL) * D_MODEL * 2 * 2
    temps = 4 * tm * D_MODEL * 4
    return pl.pallas_call(
        functools.partial(_outproj_kernel, alpha=alpha, tm=tm),
        grid=(bsz, seq // tm),
        in_specs=[
            pl.BlockSpec((1, 1, 3 * D_MODEL), lambda b, i: (b, 0, 0)),
            pl.BlockSpec((1, hg, tm, HEAD_DIM), lambda b, i: (b, 0, i, 0)),
            pl.BlockSpec((1, tm, D_MODEL), tok),
            pl.BlockSpec((1, tm, D_MODEL), tok),
            pl.BlockSpec((1, tm, D_MODEL), tok),
            pl.BlockSpec((ATTN_WIDTH, D_MODEL), const),
            pl.BlockSpec((D_MODEL, D_MODEL), const),
            pl.BlockSpec((1, D_MODEL), const),
            pl.BlockSpec((1, D_MODEL), const),
        ],
        out_specs=pl.BlockSpec((1, tm, D_MODEL), tok),
        out_shape=jax.ShapeDtypeStruct((bsz, seq, D_MODEL), F32),
        compiler_params=pltpu.CompilerParams(
            dimension_semantics=("arbitrary", "arbitrary"),
            vmem_limit_bytes=_vmem_limit(tiles + weights + temps)),
        name="out_proj",
    )(mod, o, sa, ms, x, w_attn_out, w_o, ln_g, ln_b)


def kernel(x, c, w_ada, b_ada, w_in, conv_w, conv_b, rel_bias, w_attn_out, w_conv_out, w_o, ln_g, ln_b):
    bsz = x.shape[0]
    depth = w_in.shape[0]
    alpha = (2.0 * depth) ** 0.25
    bias = _bias_call(rel_bias)
    for layer in range(depth):
        mod = _ada_call(c, w_ada[layer], b_ada[layer]).reshape(bsz, 1, 3 * D_MODEL)
        qkv1, qkv2, qkv3, sg, sa, ms = _inproj_call(
            mod, x, w_in[layer].astype(BF16), w_conv_out[layer].astype(BF16),
            conv_w[layer], conv_b[layer][None, :])
        o = _attn_call(qkv1, qkv2, qkv3, bias, sg)
        x = _outproj_call(mod, o, sa, ms, x, w_attn_out[layer].astype(BF16),
                          w_o[layer].astype(BF16), ln_g[layer][None, :], ln_b[layer][None, :], alpha)
    return x
```

```python
import functools
import math

import numpy as np
import jax
import jax.numpy as jnp
from jax import lax
from jax.experimental import pallas as pl
from jax.experimental.pallas import tpu as pltpu

F32 = jnp.float32
BF16 = jnp.bfloat16

D_MODEL = 1024
HEAD_DIM = 128
HEADS_PER_GROUP = 4
DILATED_GROUPS = ((128, 1), (512, 4), (2048, 16))
N_GROUPS = len(DILATED_GROUPS)
N_ATTN_HEADS = N_GROUPS * HEADS_PER_GROUP
QKV_WIDTH = N_ATTN_HEADS * HEAD_DIM
ATTN_WIDTH = HEADS_PER_GROUP * HEAD_DIM
CONV_WIDTH = D_MODEL
CONV_K = 3
N_BUCKETS = 32
MAX_EXACT = 16
MAX_DISTANCE = 2048
BLOCK = 128
LN_EPS = 1e-5
NEG_INF = -1e30
LOG2E = math.log2(math.e)
Q_SCALE = HEAD_DIM ** -0.5 * LOG2E

COL_Q = 0
COL_K = QKV_WIDTH
COL_V = 2 * QKV_WIDTH
COL_GATTN = 3 * QKV_WIDTH
COL_U = COL_GATTN + ATTN_WIDTH
COL_B = COL_U + CONV_WIDTH
COL_C = COL_B + CONV_WIDTH
COL_GCONV = COL_C + CONV_WIDTH
COL_MATTN = COL_GCONV + CONV_WIDTH
COL_MCONV = COL_MATTN + D_MODEL
N_COLS = COL_MCONV + D_MODEL

V7X_VMEM_BYTES = 64 * 1024 * 1024
SUBLANES = 8
INPROJ_TM = 512
OUTPROJ_TM = 1024
OUTPROJ_ROWS = 256
CONV_CHUNK = 256
ATTN_BLOCKS_PER_STAGE = 2
MERGE_ROWS = 32


def _vmem_limit(estimate_bytes):
    return int(min(V7X_VMEM_BYTES - (4 << 20), estimate_bytes * 5 // 4 + (4 << 20)))


def _bucket_table():
    a = np.arange(BLOCK)[:, None]
    b = np.arange(2 * BLOCK)[None, :]
    steps = a + BLOCK - b
    out = []
    for window, dilation in DILATED_GROUPS:
        n_steps = window // dilation
        valid = (steps >= 0) & (steps <= n_steps)
        dist = np.maximum(steps, 0) * dilation
        n = np.maximum(dist, 1).astype(np.float32)
        large = MAX_EXACT + (np.log(n / np.float32(MAX_EXACT)) / np.float32(math.log(MAX_DISTANCE / MAX_EXACT))
                             * np.float32(N_BUCKETS - MAX_EXACT)).astype(np.int32)
        large = np.minimum(large, N_BUCKETS - 1)
        bucket = np.where(dist < MAX_EXACT, dist, large)
        out.append(np.where(valid, bucket, -1).astype(np.int32))
    return np.stack(out)


_BUCKETS = _bucket_table()


def _sigmoid(v):
    return 0.5 * jnp.tanh(0.5 * v) + 0.5


def _ada_kernel(c_ref, w_ref, b_ref, o_ref):
    c = c_ref[...]
    o_ref[...] = jnp.dot(c * _sigmoid(c), w_ref[...], preferred_element_type=F32) + b_ref[...]


def _ada_call(c, w_ada, b_ada):
    bsz = c.shape[0]
    return pl.pallas_call(
        _ada_kernel,
        out_shape=jax.ShapeDtypeStruct((bsz, 3 * D_MODEL), F32),
        name="ada_mod",
    )(c, w_ada, b_ada[None, :])


def _bias_kernel(rb_ref, bucket_ref, o_ref):
    for g in range(N_GROUPS):
        bk = bucket_ref[g]
        for j in range(HEADS_PER_GROUP):
            h = g * HEADS_PER_GROUP + j
            acc = jnp.full(bk.shape, NEG_INF, F32)
            for k in range(N_BUCKETS):
                acc = jnp.where(bk == k, rb_ref[k, h] * LOG2E, acc)
            o_ref[h, 0] = acc
            o_ref[h, 1] = jnp.concatenate(
                [acc[:, BLOCK:], jnp.full((BLOCK, BLOCK), NEG_INF, F32)], axis=1)


def _bias_call(rel_bias):
    return pl.pallas_call(
        _bias_kernel,
        out_shape=jax.ShapeDtypeStruct((N_ATTN_HEADS, 2, BLOCK, 2 * BLOCK), F32),
        in_specs=[pl.BlockSpec(memory_space=pltpu.SMEM), pl.BlockSpec(memory_space=pltpu.VMEM)],
        out_specs=pl.BlockSpec(memory_space=pltpu.VMEM),
        name="bias_tables",
    )(rel_bias, jnp.asarray(_BUCKETS))


def _inproj_kernel(mod_ref, x_ref, w_ref, cw_ref, cb_ref,
                   qkv1_ref, qkv2_ref, qkv3_ref, sg_ref, spre_ref,
                   hs_ref, hb_ref, zbuf_ref, *, tm):
    i = pl.program_id(1)

    @pl.when(i == 0)
    def _():
        zbuf_ref[0:SUBLANES, :] = jnp.zeros((SUBLANES, CONV_WIDTH), F32)

    shift = mod_ref[0, :, 0:D_MODEL]
    scale1 = 1.0 + mod_ref[0, :, D_MODEL:2 * D_MODEL]
    h = x_ref[0] * scale1 + shift
    hb_ref[0] = h.astype(BF16)

    n_slab = D_MODEL // HEAD_DIM
    for s in range(n_slab):
        hs_ref[s] = h[:, s * HEAD_DIM:(s + 1) * HEAD_DIM]

    def by_residue(d):
        return jnp.concatenate(
            [jnp.concatenate([hs_ref[s, pl.ds(r, tm // d, stride=d), :] for s in range(n_slab)], axis=1)
             for r in range(d)], axis=0).astype(BF16)
    d4, d16 = DILATED_GROUPS[1][1], DILATED_GROUPS[2][1]
    hb_ref[1] = by_residue(d4)
    hb_ref[2] = by_residue(d16)

    def proj(g, c0, width):
        return jnp.dot(hb_ref[g], w_ref[:, c0:c0 + width], preferred_element_type=F32)

    def qkv(g, store):
        for t, col in enumerate((COL_Q, COL_K, COL_V)):
            r = proj(g, col + g * ATTN_WIDTH, ATTN_WIDTH)
            if t == 0:
                r = r * Q_SCALE
            r = r.astype(BF16)
            for j in range(HEADS_PER_GROUP):
                store(t, j, r[:, j * HEAD_DIM:(j + 1) * HEAD_DIM])

    def store1(t, j, v):
        qkv1_ref[0, t, j] = v
    qkv(0, store1)

    def store_sub(ref, d):
        def store(t, j, v):
            rows = tm // d
            for r in range(d):
                ref[0, t, j, r] = v[r * rows:(r + 1) * rows]
        return store
    qkv(1, store_sub(qkv2_ref, d4))
    qkv(2, store_sub(qkv3_ref, d16))

    ga = proj(0, COL_GATTN, ATTN_WIDTH)
    sgv = (ga * _sigmoid(ga)).astype(BF16)
    for j in range(HEADS_PER_GROUP):
        sg_ref[0, j] = sgv[:, j * HEAD_DIM:(j + 1) * HEAD_DIM]

    for c0 in range(0, CONV_WIDTH, CONV_CHUNK):
        cs = slice(c0, c0 + CONV_CHUNK)
        u = proj(0, COL_U + c0, CONV_CHUNK)
        cg = proj(0, COL_C + c0, CONV_CHUNK)
        z = cg * u
        zbuf_ref[SUBLANES:SUBLANES + tm, cs] = z
        zm1 = zbuf_ref[SUBLANES - 1:SUBLANES - 1 + tm, cs]
        zm2 = zbuf_ref[SUBLANES - 2:SUBLANES - 2 + tm, cs]
        y = cw_ref[0:1, cs] * zm2 + cw_ref[1:2, cs] * zm1 + cw_ref[2:3, cs] * z + cb_ref[:, cs]
        bg = proj(0, COL_B + c0, CONV_CHUNK)
        gc = proj(0, COL_GCONV + c0, CONV_CHUNK)
        spre_ref[0, :, cs] = (bg * y * (gc * _sigmoid(gc))).astype(BF16)
    zbuf_ref[0:SUBLANES, :] = zbuf_ref[tm:tm + SUBLANES, :]


def _inproj_call(mod, x, w_in, conv_w, conv_b):
    bsz, seq, _ = x.shape
    tm = INPROJ_TM
    n_t = seq // tm
    d4, d16 = DILATED_GROUPS[1][1], DILATED_GROUPS[2][1]
    sub4, sub16 = seq // d4, seq // d16
    assert seq % tm == 0 and tm % (d16 * 16) == 0
    hg = HEADS_PER_GROUP

    in_specs = [
        pl.BlockSpec((1, 1, 3 * D_MODEL), lambda b, i: (b, 0, 0)),
        pl.BlockSpec((1, tm, D_MODEL), lambda b, i: (b, i, 0)),
        pl.BlockSpec((D_MODEL, COL_MATTN), lambda b, i: (0, 0), pipeline_mode=pl.Buffered(1)),
        pl.BlockSpec((CONV_K, CONV_WIDTH), lambda b, i: (0, 0)),
        pl.BlockSpec((1, CONV_WIDTH), lambda b, i: (0, 0)),
    ]
    out_shape = [
        jax.ShapeDtypeStruct((bsz, 3, hg, seq, HEAD_DIM), BF16),
        jax.ShapeDtypeStruct((bsz, 3, hg, d4, sub4, HEAD_DIM), BF16),
        jax.ShapeDtypeStruct((bsz, 3, hg, d16, sub16, HEAD_DIM), BF16),
        jax.ShapeDtypeStruct((bsz, hg, seq, HEAD_DIM), BF16),
        jax.ShapeDtypeStruct((bsz, seq, CONV_WIDTH), BF16),
    ]
    out_specs = [
        pl.BlockSpec((1, 3, hg, tm, HEAD_DIM), lambda b, i: (b, 0, 0, i, 0)),
        pl.BlockSpec((1, 3, hg, d4, tm // d4, HEAD_DIM), lambda b, i: (b, 0, 0, 0, i, 0)),
        pl.BlockSpec((1, 3, hg, d16, tm // d16, HEAD_DIM), lambda b, i: (b, 0, 0, 0, i, 0)),
        pl.BlockSpec((1, hg, tm, HEAD_DIM), lambda b, i: (b, 0, i, 0)),
        pl.BlockSpec((1, tm, CONV_WIDTH), lambda b, i: (b, i, 0)),
    ]
    weights = D_MODEL * COL_MATTN * 2
    x_tiles = tm * D_MODEL * 4 * 2
    out_tiles = (3 * 3 * ATTN_WIDTH + ATTN_WIDTH + CONV_WIDTH) * tm * 2 * 2
    scratch = tm * D_MODEL * 4 + N_GROUPS * tm * D_MODEL * 2 + (tm + SUBLANES) * CONV_WIDTH * 4
    return pl.pallas_call(
        functools.partial(_inproj_kernel, tm=tm),
        grid=(bsz, n_t),
        in_specs=in_specs,
        out_specs=out_specs,
        out_shape=out_shape,
        scratch_shapes=[pltpu.VMEM((D_MODEL // HEAD_DIM, tm, HEAD_DIM), F32),
                        pltpu.VMEM((N_GROUPS, tm, D_MODEL), BF16),
                        pltpu.VMEM((tm + SUBLANES, CONV_WIDTH), F32)],
        compiler_params=pltpu.CompilerParams(
            dimension_semantics=("arbitrary", "arbitrary"),
            vmem_limit_bytes=_vmem_limit(weights + x_tiles + out_tiles + scratch)),
        name="in_proj",
    )(mod, x, w_in, conv_w, conv_b)


def _nt_dot(q, k):
    return lax.dot_general(q, k, (((1,), (1,)), ((), ())), preferred_element_type=F32)


def _softmax_pv(s, v):
    m = jnp.max(s, axis=1, keepdims=True)
    p = jnp.exp2(s - m)
    v_ext = jnp.concatenate([v, jnp.ones((v.shape[0], HEAD_DIM), BF16)], axis=1)
    acc = jnp.dot(p.astype(BF16), v_ext, preferred_element_type=F32)
    return acc[:, :HEAD_DIM], m, acc[:, HEAD_DIM:]


def _attn_kernel(qkv1_ref, qkv2_ref, qkv3_ref, bias_ref, sg_ref, o_ref,
                 sa1_ref, sa2_ref, sa3_ref, sb1_ref, sb2_ref, sb3_ref, *row_refs, seq):
    res1, res2, res3, stage3 = (row_refs[3 * g:3 * g + 3] for g in range(N_GROUPS + 1))
    step = pl.program_id(0)
    par = jnp.bitwise_and(step, 1)
    opar = 1 - par

    @pl.when(step == 0)
    def _():
        for r in res1 + res2 + res3:
            r[1] = jnp.ones((seq, HEAD_DIM), F32)

    d4, d16 = DILATED_GROUPS[1][1], DILATED_GROUPS[2][1]
    per = ATTN_BLOCKS_PER_STAGE
    n_it = seq // BLOCK // per
    blk4 = seq // d4 // BLOCK
    assert seq // d16 == BLOCK and blk4 & (blk4 - 1) == 0 and n_it % 2 == 0 and n_it * per * BLOCK == seq
    assert d16 == d4 * d4 and d4 == blk4
    shift4 = blk4.bit_length() - 1
    sub4 = seq // d4

    def rows(start, size, align=BLOCK):
        if isinstance(start, int):
            return pl.ds(start, size)
        return pl.ds(pl.multiple_of(start, align), size)

    def plan(it, u):
        it = it * per + u
        if isinstance(it, int):
            f1, nb, r4 = int(it == 0), it & (blk4 - 1), it >> shift4
            f2 = int(nb == 0)
        else:
            f1 = (it == 0).astype(jnp.int32)
            nb = jnp.bitwise_and(it, blk4 - 1)
            f2 = (nb == 0).astype(jnp.int32)
            r4 = lax.shift_right_logical(it, shift4)
        return dict(q=it * BLOCK, k1=(it - 1 + f1) * BLOCK, f1=f1, k2=(it - 1 + f2) * BLOCK, f2=f2,
                    out2=nb * (BLOCK * d4) + r4, out3=nb * sub4 + r4)

    def scores(it, s_refs):
        for u in range(per):
            p = plan(it, u)
            q = rows(p["q"], BLOCK)
            s_refs[0][u] = _nt_dot(qkv1_ref[0, 0, 0, q, :], qkv1_ref[0, 1, 0, rows(p["k1"], 2 * BLOCK), :]) \
                + bias_ref[0, 0, p["f1"]]
            s_refs[1][u] = _nt_dot(qkv2_ref[0, 0, 0, q, :], qkv2_ref[0, 1, 0, rows(p["k2"], 2 * BLOCK), :]) \
                + bias_ref[1, 0, p["f2"]]
            s_refs[2][u] = _nt_dot(qkv3_ref[0, 0, 0, q, :], qkv3_ref[0, 1, 0, q, :]) \
                + bias_ref[2, 0, 1, :, 0:BLOCK]

    def put(dst, lead, idx, acc, m, l):
        dst[0][(*lead, idx, slice(None))] = acc
        dst[1][(*lead, idx, slice(None))] = jnp.broadcast_to(m, (BLOCK, HEAD_DIM))
        dst[2][(*lead, idx, slice(None))] = l

    def finish(it, s_refs):
        for u in range(per):
            p = plan(it, u)
            q = rows(p["q"], BLOCK)
            put(res1, (par,), q, *_softmax_pv(s_refs[0][u], qkv1_ref[0, 2, 0, rows(p["k1"], 2 * BLOCK), :]))
            put(res2, (par,), pl.ds(p["out2"], BLOCK, stride=d4),
                *_softmax_pv(s_refs[1][u], qkv2_ref[0, 2, 0, rows(p["k2"], 2 * BLOCK), :]))
            put(stage3, (), pl.ds(p["out3"], BLOCK, stride=d4),
                *_softmax_pv(s_refs[2][u], qkv3_ref[0, 2, 0, q, :]))

    def merge(base, n_rows):
        for c0 in range(0, n_rows, MERGE_ROWS):
            rws = rows(base + c0, MERGE_ROWS, MERGE_ROWS)
            ms = [r[1][opar, rws, :] for r in (res1, res2, res3)]
            m = jnp.maximum(jnp.maximum(ms[0], ms[1]), ms[2])
            es = [jnp.exp2(mg - m) for mg in ms]
            num = sum(e * r[0][opar, rws, :] for e, r in zip(es, (res1, res2, res3)))
            den = sum(e * r[2][opar, rws, :] for e, r in zip(es, (res1, res2, res3)))
            o_ref[0, 0, rws, :] = (num * (1.0 / den)).astype(BF16) * sg_ref[0, 0, rws, :]

    sec = seq // n_it
    s_a = (sa1_ref, sa2_ref, sa3_ref)
    s_b = (sb1_ref, sb2_ref, sb3_ref)
    merge(0, sec // 2)
    scores(0, s_a)

    def body(i, carry):
        base = sec // 2 + 2 * i * sec
        merge(base, sec)
        scores(2 * i + 1, s_b)
        finish(2 * i, s_a)
        merge(base + sec, sec)
        scores(2 * i + 2, s_a)
        finish(2 * i + 1, s_b)
        return carry
    lax.fori_loop(0, n_it // 2 - 1, body, 0)
    tail = sec // 2 + (n_it - 2) * sec
    merge(tail, seq - tail)
    scores(n_it - 1, s_b)
    finish(n_it - 2, s_a)
    finish(n_it - 1, s_b)

    for c in range(d4):
        for p0 in range(0, sub4, BLOCK):
            src = slice(c * sub4 + p0, c * sub4 + p0 + BLOCK)
            dst = pl.ds(d4 * p0 + c, BLOCK, stride=d4)
            for k in range(3):
                res3[k][par, dst, :] = stage3[k][src, :]


def _attn_call(qkv1, qkv2, qkv3, bias, sg):
    bsz, _, hg, seq, _ = qkv1.shape
    qkv2 = qkv2.reshape(qkv1.shape)
    qkv3 = qkv3.reshape(qkv1.shape)
    bias = bias.reshape(N_GROUPS, hg, 2, BLOCK, 2 * BLOCK)
    n_slots = bsz * hg

    def cur(s):
        c = jnp.minimum(s, n_slots - 1)
        return c // hg, c % hg

    def prev(s):
        p = jnp.maximum(s - 1, 0)
        return p // hg, p % hg

    def qkv_map(s):
        b, j = cur(s)
        return (b, 0, j, 0, 0)

    def bias_map(s):
        return (0, cur(s)[1], 0, 0, 0)

    def lag_map(s):
        b, j = prev(s)
        return (b, j, 0, 0)

    in_tiles = (3 * 3 + 1) * seq * HEAD_DIM * 2 * 2 + N_GROUPS * 2 * BLOCK * 2 * BLOCK * 4 * 2
    out_tiles = seq * HEAD_DIM * 2 * 2
    per = ATTN_BLOCKS_PER_STAGE
    score_shapes = [(per, BLOCK, 2 * BLOCK), (per, BLOCK, 2 * BLOCK), (per, BLOCK, BLOCK)] * 2
    row_shapes = [(2, seq, HEAD_DIM)] * (3 * N_GROUPS) + [(seq, HEAD_DIM)] * 3
    scratch = sum(math.prod(sh) * 4 for sh in score_shapes + row_shapes)
    qkv_spec = pl.BlockSpec((1, 3, 1, seq, HEAD_DIM), qkv_map)
    return pl.pallas_call(
        functools.partial(_attn_kernel, seq=seq),
        grid=(n_slots + 1,),
        in_specs=[
            qkv_spec, qkv_spec, qkv_spec,
            pl.BlockSpec((N_GROUPS, 1, 2, BLOCK, 2 * BLOCK), bias_map),
            pl.BlockSpec((1, 1, seq, HEAD_DIM), lag_map),
        ],
        out_specs=pl.BlockSpec((1, 1, seq, HEAD_DIM), lag_map),
        out_shape=jax.ShapeDtypeStruct((bsz, hg, seq, HEAD_DIM), BF16),
        scratch_shapes=[pltpu.VMEM(sh, F32) for sh in score_shapes + row_shapes],
        compiler_params=pltpu.CompilerParams(
            dimension_semantics=("arbitrary",),
            vmem_limit_bytes=_vmem_limit(in_tiles + out_tiles + scratch)),
        name="dilated_attn",
    )(qkv1, qkv2, qkv3, bias, sg)


def _outproj_kernel(mod_ref, o_ref, spre_ref, x_ref, wm_ref, wco_ref, wao_ref, wo_ref,
                    lng_ref, lnb_ref, out_ref, *, alpha, tm):
    shift = mod_ref[0, :, 0:D_MODEL]
    scale1 = 1.0 + mod_ref[0, :, D_MODEL:2 * D_MODEL]
    gate1 = 1.0 + mod_ref[0, :, 2 * D_MODEL:3 * D_MODEL]
    for r0 in range(0, tm, OUTPROJ_ROWS):
        rws = slice(r0, r0 + OUTPROJ_ROWS)
        xr = x_ref[0, rws, :]
        h = (xr * scale1 + shift).astype(BF16)
        g_attn = _sigmoid(jnp.dot(h, wm_ref[:, 0:D_MODEL], preferred_element_type=F32))
        g_conv = _sigmoid(jnp.dot(h, wm_ref[:, D_MODEL:2 * D_MODEL], preferred_element_type=F32))
        a_in = jnp.concatenate([o_ref[0, j, rws, :] for j in range(HEADS_PER_GROUP)], axis=1)
        a_out = jnp.dot(a_in, wao_ref[...], preferred_element_type=F32)
        s_out = jnp.dot(spre_ref[0, rws, :], wco_ref[...], preferred_element_type=F32)
        merged = (g_attn * a_out + g_conv * s_out).astype(BF16)
        y = jnp.dot(merged, wo_ref[...], preferred_element_type=F32)
        v = alpha * xr + gate1 * y
        mu = jnp.mean(v, axis=-1, keepdims=True)
        dv = v - mu
        var = jnp.mean(dv * dv, axis=-1, keepdims=True)
        out_ref[0, rws, :] = dv * lax.rsqrt(var + LN_EPS) * lng_ref[...] + lnb_ref[...]


def _outproj_call(mod, o, spre, x, w_m, w_conv_out, w_attn_out, w_o, ln_g, ln_b, alpha):
    bsz, seq, _ = x.shape
    tm = OUTPROJ_TM
    assert seq % tm == 0 and tm % OUTPROJ_ROWS == 0
    hg = HEADS_PER_GROUP
    tok = lambda b, i: (b, i, 0)
    const = lambda b, i: (0, 0)
    resident = dict(pipeline_mode=pl.Buffered(1))
    tiles = (ATTN_WIDTH * 2 + CONV_WIDTH * 2 + 2 * D_MODEL * 4) * tm * 2
    weights = (2 * D_MODEL + CONV_WIDTH + ATTN_WIDTH + D_MODEL) * D_MODEL * 2
    temps = 8 * OUTPROJ_ROWS * D_MODEL * 4
    return pl.pallas_call(
        functools.partial(_outproj_kernel, alpha=alpha, tm=tm),
        grid=(bsz, seq // tm),
        in_specs=[
            pl.BlockSpec((1, 1, 3 * D_MODEL), lambda b, i: (b, 0, 0)),
            pl.BlockSpec((1, hg, tm, HEAD_DIM), lambda b, i: (b, 0, i, 0)),
            pl.BlockSpec((1, tm, CONV_WIDTH), tok),
            pl.BlockSpec((1, tm, D_MODEL), tok),
            pl.BlockSpec((D_MODEL, 2 * D_MODEL), const, **resident),
            pl.BlockSpec((CONV_WIDTH, D_MODEL), const, **resident),
            pl.BlockSpec((ATTN_WIDTH, D_MODEL), const, **resident),
            pl.BlockSpec((D_MODEL, D_MODEL), const, **resident),
            pl.BlockSpec((1, D_MODEL), const),
            pl.BlockSpec((1, D_MODEL), const),
        ],
        out_specs=pl.BlockSpec((1, tm, D_MODEL), tok),
        out_shape=jax.ShapeDtypeStruct((bsz, seq, D_MODEL), F32),
        compiler_params=pltpu.CompilerParams(
            dimension_semantics=("arbitrary", "arbitrary"),
            vmem_limit_bytes=_vmem_limit(tiles + weights + temps)),
        name="out_proj",
    )(mod, o, spre, x, w_m, w_conv_out, w_attn_out, w_o, ln_g, ln_b)


def kernel(x, c, w_ada, b_ada, w_in, conv_w, conv_b, rel_bias, w_attn_out, w_conv_out, w_o, ln_g, ln_b):
    bsz = x.shape[0]
    depth = w_in.shape[0]
    alpha = (2.0 * depth) ** 0.25
    bias = _bias_call(rel_bias)
    for layer in range(depth):
        mod = _ada_call(c, w_ada[layer], b_ada[layer]).reshape(bsz, 1, 3 * D_MODEL)
        w_in_b = w_in[layer].astype(BF16)
        qkv1, qkv2, qkv3, sg, spre = _inproj_call(
            mod, x, w_in_b[:, :COL_MATTN], conv_w[layer], conv_b[layer][None, :])
        o = _attn_call(qkv1, qkv2, qkv3, bias, sg)
        x = _outproj_call(mod, o, spre, x, w_in_b[:, COL_MATTN:], w_conv_out[layer].astype(BF16),
                          w_attn_out[layer].astype(BF16), w_o[layer].astype(BF16),
                          ln_g[layer][None, :], ln_b[layer][None, :], alpha)
    return x
```

```python
import functools
import math

import numpy as np
import jax
import jax.numpy as jnp
from jax import lax
from jax.experimental import pallas as pl
from jax.experimental.pallas import tpu as pltpu

F32 = jnp.float32
BF16 = jnp.bfloat16

D_MODEL = 1024
HEAD_DIM = 128
HEADS_PER_GROUP = 4
DILATED_GROUPS = ((128, 1), (512, 4), (2048, 16))
N_GROUPS = len(DILATED_GROUPS)
N_ATTN_HEADS = N_GROUPS * HEADS_PER_GROUP
QKV_WIDTH = N_ATTN_HEADS * HEAD_DIM
ATTN_WIDTH = HEADS_PER_GROUP * HEAD_DIM
CONV_WIDTH = D_MODEL
CONV_K = 3
N_BUCKETS = 32
MAX_EXACT = 16
MAX_DISTANCE = 2048
BLOCK = 128
LN_EPS = 1e-5
NEG_INF = -1e30
LOG2E = math.log2(math.e)
Q_SCALE = HEAD_DIM ** -0.5 * LOG2E

COL_Q = 0
COL_K = QKV_WIDTH
COL_V = 2 * QKV_WIDTH
COL_GATTN = 3 * QKV_WIDTH
COL_U = COL_GATTN + ATTN_WIDTH
COL_B = COL_U + CONV_WIDTH
COL_C = COL_B + CONV_WIDTH
COL_GCONV = COL_C + CONV_WIDTH
COL_MATTN = COL_GCONV + CONV_WIDTH
COL_MCONV = COL_MATTN + D_MODEL
N_COLS = COL_MCONV + D_MODEL

V7X_VMEM_BYTES = 64 * 1024 * 1024
SUBLANES = 8
INPROJ_TM = 512
OUTPROJ_TM = 1024
OUTPROJ_ROWS = 256
CONV_CHUNK = 256
ATTN_BLOCKS_PER_STAGE = 2
MERGE_ROWS = 32


def _vmem_limit(estimate_bytes):
    return int(min(V7X_VMEM_BYTES - (4 << 20), estimate_bytes * 5 // 4 + (4 << 20)))


def _bucket_table():
    a = np.arange(BLOCK)[:, None]
    b = np.arange(2 * BLOCK)[None, :]
    steps = a + BLOCK - b
    out = []
    for window, dilation in DILATED_GROUPS:
        n_steps = window // dilation
        valid = (steps >= 0) & (steps <= n_steps)
        dist = np.maximum(steps, 0) * dilation
        n = np.maximum(dist, 1).astype(np.float32)
        large = MAX_EXACT + (np.log(n / np.float32(MAX_EXACT)) / np.float32(math.log(MAX_DISTANCE / MAX_EXACT))
                             * np.float32(N_BUCKETS - MAX_EXACT)).astype(np.int32)
        large = np.minimum(large, N_BUCKETS - 1)
        bucket = np.where(dist < MAX_EXACT, dist, large)
        out.append(np.where(valid, bucket, -1).astype(np.int32))
    return np.stack(out)


_BUCKETS = _bucket_table()


def _sigmoid(v):
    return 0.5 * jnp.tanh(0.5 * v) + 0.5


def _ada_kernel(c_ref, w_ref, b_ref, o_ref):
    c = c_ref[...]
    o_ref[...] = jnp.dot(c * _sigmoid(c), w_ref[...], preferred_element_type=F32) + b_ref[...]


def _ada_call(c, w_ada, b_ada):
    bsz = c.shape[0]
    return pl.pallas_call(
        _ada_kernel,
        out_shape=jax.ShapeDtypeStruct((bsz, 3 * D_MODEL), F32),
        name="ada_mod",
    )(c, w_ada, b_ada[None, :])


def _bias_kernel(rb_ref, bucket_ref, o_ref):
    for g in range(N_GROUPS):
        bk = bucket_ref[g]
        for j in range(HEADS_PER_GROUP):
            h = g * HEADS_PER_GROUP + j
            acc = jnp.full(bk.shape, NEG_INF, F32)
            for k in range(N_BUCKETS):
                acc = jnp.where(bk == k, rb_ref[k, h] * LOG2E, acc)
            o_ref[h, 0] = acc
            o_ref[h, 1] = jnp.concatenate(
                [acc[:, BLOCK:], jnp.full((BLOCK, BLOCK), NEG_INF, F32)], axis=1)


def _bias_call(rel_bias):
    return pl.pallas_call(
        _bias_kernel,
        out_shape=jax.ShapeDtypeStruct((N_ATTN_HEADS, 2, BLOCK, 2 * BLOCK), F32),
        in_specs=[pl.BlockSpec(memory_space=pltpu.SMEM), pl.BlockSpec(memory_space=pltpu.VMEM)],
        out_specs=pl.BlockSpec(memory_space=pltpu.VMEM),
        name="bias_tables",
    )(rel_bias, jnp.asarray(_BUCKETS))


def _inproj_kernel(mod_ref, x_ref, w_ref, cw_ref, cb_ref,
                   qkv1_ref, qkv2_ref, qkv3_ref, sg_ref, spre_ref,
                   hs_ref, hb_ref, zbuf_ref, *, tm):
    i = pl.program_id(1)

    @pl.when(i == 0)
    def _():
        zbuf_ref[0:SUBLANES, :] = jnp.zeros((SUBLANES, CONV_WIDTH), F32)

    shift = mod_ref[0, :, 0:D_MODEL]
    scale1 = 1.0 + mod_ref[0, :, D_MODEL:2 * D_MODEL]
    h = x_ref[0] * scale1 + shift
    hb_ref[0] = h.astype(BF16)

    n_slab = D_MODEL // HEAD_DIM
    for s in range(n_slab):
        hs_ref[s] = h[:, s * HEAD_DIM:(s + 1) * HEAD_DIM]

    def by_residue(d):
        return jnp.concatenate(
            [jnp.concatenate([hs_ref[s, pl.ds(r, tm // d, stride=d), :] for s in range(n_slab)], axis=1)
             for r in range(d)], axis=0).astype(BF16)
    d4, d16 = DILATED_GROUPS[1][1], DILATED_GROUPS[2][1]
    hb_ref[1] = by_residue(d4)
    hb_ref[2] = by_residue(d16)

    def proj(g, c0, width):
        return jnp.dot(hb_ref[g], w_ref[:, c0:c0 + width], preferred_element_type=F32)

    def qkv(g, store):
        for t, col in enumerate((COL_Q, COL_K, COL_V)):
            r = proj(g, col + g * ATTN_WIDTH, ATTN_WIDTH)
            if t == 0:
                r = r * Q_SCALE
            r = r.astype(BF16)
            for j in range(HEADS_PER_GROUP):
                store(t, j, r[:, j * HEAD_DIM:(j + 1) * HEAD_DIM])

    def store1(t, j, v):
        qkv1_ref[0, t, j] = v
    qkv(0, store1)

    def store_sub(ref, d):
        def store(t, j, v):
            rows = tm // d
            for r in range(d):
                ref[0, t, j, r] = v[r * rows:(r + 1) * rows]
        return store
    qkv(1, store_sub(qkv2_ref, d4))
    qkv(2, store_sub(qkv3_ref, d16))

    ga = proj(0, COL_GATTN, ATTN_WIDTH)
    sgv = (ga * _sigmoid(ga)).astype(BF16)
    for j in range(HEADS_PER_GROUP):
        sg_ref[0, j] = sgv[:, j * HEAD_DIM:(j + 1) * HEAD_DIM]

    for c0 in range(0, CONV_WIDTH, CONV_CHUNK):
        cs = slice(c0, c0 + CONV_CHUNK)
        u = proj(0, COL_U + c0, CONV_CHUNK)
        cg = proj(0, COL_C + c0, CONV_CHUNK)
        z = cg * u
        zbuf_ref[SUBLANES:SUBLANES + tm, cs] = z
        zm1 = zbuf_ref[SUBLANES - 1:SUBLANES - 1 + tm, cs]
        zm2 = zbuf_ref[SUBLANES - 2:SUBLANES - 2 + tm, cs]
        y = cw_ref[0:1, cs] * zm2 + cw_ref[1:2, cs] * zm1 + cw_ref[2:3, cs] * z + cb_ref[:, cs]
        bg = proj(0, COL_B + c0, CONV_CHUNK)
        gc = proj(0, COL_GCONV + c0, CONV_CHUNK)
        spre_ref[0, :, cs] = (bg * y * (gc * _sigmoid(gc))).astype(BF16)
    zbuf_ref[0:SUBLANES, :] = zbuf_ref[tm:tm + SUBLANES, :]


def _inproj_call(mod, x, w_in, conv_w, conv_b):
    bsz, seq, _ = x.shape
    tm = INPROJ_TM
    n_t = seq // tm
    d4, d16 = DILATED_GROUPS[1][1], DILATED_GROUPS[2][1]
    sub4, sub16 = seq // d4, seq // d16
    assert seq % tm == 0 and tm % (d16 * 16) == 0
    hg = HEADS_PER_GROUP

    in_specs = [
        pl.BlockSpec((1, 1, 3 * D_MODEL), lambda b, i: (b, 0, 0)),
        pl.BlockSpec((1, tm, D_MODEL), lambda b, i: (b, i, 0)),
        pl.BlockSpec((D_MODEL, COL_MATTN), lambda b, i: (0, 0), pipeline_mode=pl.Buffered(1)),
        pl.BlockSpec((CONV_K, CONV_WIDTH), lambda b, i: (0, 0)),
        pl.BlockSpec((1, CONV_WIDTH), lambda b, i: (0, 0)),
    ]
    out_shape = [
        jax.ShapeDtypeStruct((bsz, 3, hg, seq, HEAD_DIM), BF16),
        jax.ShapeDtypeStruct((bsz, 3, hg, d4, sub4, HEAD_DIM), BF16),
        jax.ShapeDtypeStruct((bsz, 3, hg, d16, sub16, HEAD_DIM), BF16),
        jax.ShapeDtypeStruct((bsz, hg, seq, HEAD_DIM), BF16),
        jax.ShapeDtypeStruct((bsz, seq, CONV_WIDTH), BF16),
    ]
    out_specs = [
        pl.BlockSpec((1, 3, hg, tm, HEAD_DIM), lambda b, i: (b, 0, 0, i, 0)),
        pl.BlockSpec((1, 3, hg, d4, tm // d4, HEAD_DIM), lambda b, i: (b, 0, 0, 0, i, 0)),
        pl.BlockSpec((1, 3, hg, d16, tm // d16, HEAD_DIM), lambda b, i: (b, 0, 0, 0, i, 0)),
        pl.BlockSpec((1, hg, tm, HEAD_DIM), lambda b, i: (b, 0, i, 0)),
        pl.BlockSpec((1, tm, CONV_WIDTH), lambda b, i: (b, i, 0)),
    ]
    weights = D_MODEL * COL_MATTN * 2
    x_tiles = tm * D_MODEL * 4 * 2
    out_tiles = (3 * 3 * ATTN_WIDTH + ATTN_WIDTH + CONV_WIDTH) * tm * 2 * 2
    scratch = tm * D_MODEL * 4 + N_GROUPS * tm * D_MODEL * 2 + (tm + SUBLANES) * CONV_WIDTH * 4
    return pl.pallas_call(
        functools.partial(_inproj_kernel, tm=tm),
        grid=(bsz, n_t),
        in_specs=in_specs,
        out_specs=out_specs,
        out_shape=out_shape,
        scratch_shapes=[pltpu.VMEM((D_MODEL // HEAD_DIM, tm, HEAD_DIM), F32),
                        pltpu.VMEM((N_GROUPS, tm, D_MODEL), BF16),
                        pltpu.VMEM((tm + SUBLANES, CONV_WIDTH), F32)],
        compiler_params=pltpu.CompilerParams(
            dimension_semantics=("arbitrary", "arbitrary"),
            vmem_limit_bytes=_vmem_limit(weights + x_tiles + out_tiles + scratch)),
        name="in_proj",
    )(mod, x, w_in, conv_w, conv_b)


def _nt_dot(q, k):
    return lax.dot_general(q, k, (((1,), (1,)), ((), ())), preferred_element_type=F32)


def _softmax_pv(s, v):
    m = jnp.max(s, axis=1, keepdims=True)
    p = jnp.exp2(s - m)
    v_ext = jnp.concatenate([v, jnp.ones((v.shape[0], HEAD_DIM), BF16)], axis=1)
    acc = jnp.dot(p.astype(BF16), v_ext, preferred_element_type=F32)
    return acc[:, :HEAD_DIM], m, acc[:, HEAD_DIM:]


def _attn_kernel(qkv1_ref, qkv2_ref, qkv3_ref, bias_ref, sg_ref, o_ref,
                 sa1_ref, sa2_ref, sa3_ref, sb1_ref, sb2_ref, sb3_ref, *row_refs, seq):
    res1, res2, res3, stage3 = (row_refs[3 * g:3 * g + 3] for g in range(N_GROUPS + 1))
    step = pl.program_id(0)
    par = jnp.bitwise_and(step, 1)
    opar = 1 - par

    @pl.when(step == 0)
    def _():
        for r in res1 + res2 + res3:
            r[1] = jnp.ones((seq, HEAD_DIM), F32)

    d4, d16 = DILATED_GROUPS[1][1], DILATED_GROUPS[2][1]
    per = ATTN_BLOCKS_PER_STAGE
    n_it = seq // BLOCK // per
    blk4 = seq // d4 // BLOCK
    assert seq // d16 == BLOCK and blk4 & (blk4 - 1) == 0 and n_it % 2 == 0 and n_it * per * BLOCK == seq
    assert d16 == d4 * d4 and d4 == blk4
    shift4 = blk4.bit_length() - 1
    sub4 = seq // d4

    def rows(start, size, align=BLOCK):
        if isinstance(start, int):
            return pl.ds(start, size)
        return pl.ds(pl.multiple_of(start, align), size)

    def plan(it, u):
        it = it * per + u
        if isinstance(it, int):
            f1, nb, r4 = int(it == 0), it & (blk4 - 1), it >> shift4
            f2 = int(nb == 0)
        else:
            f1 = (it == 0).astype(jnp.int32)
            nb = jnp.bitwise_and(it, blk4 - 1)
            f2 = (nb == 0).astype(jnp.int32)
            r4 = lax.shift_right_logical(it, shift4)
        return dict(q=it * BLOCK, k1=(it - 1 + f1) * BLOCK, f1=f1, k2=(it - 1 + f2) * BLOCK, f2=f2,
                    out2=nb * (BLOCK * d4) + r4, out3=nb * sub4 + r4)

    def scores(it, s_refs):
        for u in range(per):
            p = plan(it, u)
            q = rows(p["q"], BLOCK)
            s_refs[0][u] = _nt_dot(qkv1_ref[0, 0, 0, q, :], qkv1_ref[0, 1, 0, rows(p["k1"], 2 * BLOCK), :]) \
                + bias_ref[0, 0, p["f1"]]
            s_refs[1][u] = _nt_dot(qkv2_ref[0, 0, 0, q, :], qkv2_ref[0, 1, 0, rows(p["k2"], 2 * BLOCK), :]) \
                + bias_ref[1, 0, p["f2"]]
            s_refs[2][u] = _nt_dot(qkv3_ref[0, 0, 0, q, :], qkv3_ref[0, 1, 0, q, :]) \
                + bias_ref[2, 0, 1, :, 0:BLOCK]

    def put(dst, lead, idx, acc, m, l):
        dst[0][(*lead, idx, slice(None))] = acc
        dst[1][(*lead, idx, slice(None))] = jnp.broadcast_to(m, (BLOCK, HEAD_DIM))
        dst[2][(*lead, idx, slice(None))] = l

    def finish(it, s_refs):
        for u in range(per):
            p = plan(it, u)
            q = rows(p["q"], BLOCK)
            put(res1, (par,), q, *_softmax_pv(s_refs[0][u], qkv1_ref[0, 2, 0, rows(p["k1"], 2 * BLOCK), :]))
            put(res2, (par,), pl.ds(p["out2"], BLOCK, stride=d4),
                *_softmax_pv(s_refs[1][u], qkv2_ref[0, 2, 0, rows(p["k2"], 2 * BLOCK), :]))
            put(stage3, (), pl.ds(p["out3"], BLOCK, stride=d4),
                *_softmax_pv(s_refs[2][u], qkv3_ref[0, 2, 0, q, :]))

    def merge(base, n_rows):
        for c0 in range(0, n_rows, MERGE_ROWS):
            rws = rows(base + c0, MERGE_ROWS, MERGE_ROWS)
            ms = [r[1][opar, rws, :] for r in (res1, res2, res3)]
            m = jnp.maximum(jnp.maximum(ms[0], ms[1]), ms[2])
            es = [jnp.exp2(mg - m) for mg in ms]
            num = sum(e * r[0][opar, rws, :] for e, r in zip(es, (res1, res2, res3)))
            den = sum(e * r[2][opar, rws, :] for e, r in zip(es, (res1, res2, res3)))
            o_ref[0, 0, rws, :] = (num * (1.0 / den)).astype(BF16) * sg_ref[0, 0, rws, :]

    sec = seq // n_it
    s_a = (sa1_ref, sa2_ref, sa3_ref)
    s_b = (sb1_ref, sb2_ref, sb3_ref)
    merge(0, sec // 2)
    scores(0, s_a)

    def body(i, carry):
        base = sec // 2 + 2 * i * sec
        merge(base, sec)
        scores(2 * i + 1, s_b)
        finish(2 * i, s_a)
        merge(base + sec, sec)
        scores(2 * i + 2, s_a)
        finish(2 * i + 1, s_b)
        return carry
    lax.fori_loop(0, n_it // 2 - 1, body, 0)
    tail = sec // 2 + (n_it - 2) * sec
    merge(tail, seq - tail)
    scores(n_it - 1, s_b)
    finish(n_it - 2, s_a)
    finish(n_it - 1, s_b)

    for c in range(d4):
        for p0 in range(0, sub4, BLOCK):
            src = slice(c * sub4 + p0, c * sub4 + p0 + BLOCK)
            dst = pl.ds(d4 * p0 + c, BLOCK, stride=d4)
            for k in range(3):
                res3[k][par, dst, :] = stage3[k][src, :]


def _attn_call(qkv1, qkv2, qkv3, bias, sg):
    bsz, _, hg, seq, _ = qkv1.shape
    qkv2 = qkv2.reshape(qkv1.shape)
    qkv3 = qkv3.reshape(qkv1.shape)
    bias = bias.reshape(N_GROUPS, hg, 2, BLOCK, 2 * BLOCK)
    n_slots = bsz * hg

    def cur(s):
        c = jnp.minimum(s, n_slots - 1)
        return c // hg, c % hg

    def prev(s):
        p = jnp.maximum(s - 1, 0)
        return p // hg, p % hg

    def qkv_map(s):
        b, j = cur(s)
        return (b, 0, j, 0, 0)

    def bias_map(s):
        return (0, cur(s)[1], 0, 0, 0)

    def lag_map(s):
        b, j = prev(s)
        return (b, j, 0, 0)

    in_tiles = (3 * 3 + 1) * seq * HEAD_DIM * 2 * 2 + N_GROUPS * 2 * BLOCK * 2 * BLOCK * 4 * 2
    out_tiles = seq * HEAD_DIM * 2 * 2
    per = ATTN_BLOCKS_PER_STAGE
    score_shapes = [(per, BLOCK, 2 * BLOCK), (per, BLOCK, 2 * BLOCK), (per, BLOCK, BLOCK)] * 2
    row_shapes = [(2, seq, HEAD_DIM)] * (3 * N_GROUPS) + [(seq, HEAD_DIM)] * 3
    scratch = sum(math.prod(sh) * 4 for sh in score_shapes + row_shapes)
    qkv_spec = pl.BlockSpec((1, 3, 1, seq, HEAD_DIM), qkv_map)
    return pl.pallas_call(
        functools.partial(_attn_kernel, seq=seq),
        grid=(n_slots + 1,),
        in_specs=[
            qkv_spec, qkv_spec, qkv_spec,
            pl.BlockSpec((N_GROUPS, 1, 2, BLOCK, 2 * BLOCK), bias_map),
            pl.BlockSpec((1, 1, seq, HEAD_DIM), lag_map),
        ],
        out_specs=pl.BlockSpec((1, 1, seq, HEAD_DIM), lag_map),
        out_shape=jax.ShapeDtypeStruct((bsz, hg, seq, HEAD_DIM), BF16),
        scratch_shapes=[pltpu.VMEM(sh, F32) for sh in score_shapes + row_shapes],
        compiler_params=pltpu.CompilerParams(
            dimension_semantics=("arbitrary",),
            vmem_limit_bytes=_vmem_limit(in_tiles + out_tiles + scratch)),
        name="dilated_attn",
    )(qkv1, qkv2, qkv3, bias, sg)


def _outproj_kernel(mod_ref, o_ref, spre_ref, x_ref, wma_ref, wmc_ref, wco_ref, wao_ref, wo_ref,
                    lng_ref, lnb_ref, out_ref, *, alpha, tm):
    shift = mod_ref[0, :, 0:D_MODEL]
    scale1 = 1.0 + mod_ref[0, :, D_MODEL:2 * D_MODEL]
    gate1 = 1.0 + mod_ref[0, :, 2 * D_MODEL:3 * D_MODEL]
    for r0 in range(0, tm, OUTPROJ_ROWS):
        rws = slice(r0, r0 + OUTPROJ_ROWS)
        xr = x_ref[0, rws, :]
        h = (xr * scale1 + shift).astype(BF16)
        g_attn = _sigmoid(jnp.dot(h, wma_ref[...], preferred_element_type=F32))
        g_conv = _sigmoid(jnp.dot(h, wmc_ref[...], preferred_element_type=F32))
        a_in = jnp.concatenate([o_ref[0, j, rws, :] for j in range(HEADS_PER_GROUP)], axis=1)
        a_out = jnp.dot(a_in, wao_ref[...], preferred_element_type=F32)
        s_out = jnp.dot(spre_ref[0, rws, :], wco_ref[...], preferred_element_type=F32)
        merged = (g_attn * a_out + g_conv * s_out).astype(BF16)
        y = jnp.dot(merged, wo_ref[...], preferred_element_type=F32)
        v = alpha * xr + gate1 * y
        mu = jnp.mean(v, axis=-1, keepdims=True)
        dv = v - mu
        var = jnp.mean(dv * dv, axis=-1, keepdims=True)
        out_ref[0, rws, :] = dv * lax.rsqrt(var + LN_EPS) * lng_ref[...] + lnb_ref[...]


def _outproj_call(mod, o, spre, x, w_in, w_conv_out, w_attn_out, w_o, ln_g, ln_b, alpha):
    bsz, seq, _ = x.shape
    tm = OUTPROJ_TM
    assert seq % tm == 0 and tm % OUTPROJ_ROWS == 0
    assert COL_MATTN % D_MODEL == 0 and COL_MCONV % D_MODEL == 0
    hg = HEADS_PER_GROUP
    tok = lambda b, i: (b, i, 0)
    const = lambda b, i: (0, 0)
    resident = dict(pipeline_mode=pl.Buffered(1))
    tiles = (ATTN_WIDTH * 2 + CONV_WIDTH * 2 + 2 * D_MODEL * 4) * tm * 2
    weights = (2 * D_MODEL + CONV_WIDTH + ATTN_WIDTH + D_MODEL) * D_MODEL * 2
    temps = 8 * OUTPROJ_ROWS * D_MODEL * 4
    return pl.pallas_call(
        functools.partial(_outproj_kernel, alpha=alpha, tm=tm),
        grid=(bsz, seq // tm),
        in_specs=[
            pl.BlockSpec((1, 1, 3 * D_MODEL), lambda b, i: (b, 0, 0)),
            pl.BlockSpec((1, hg, tm, HEAD_DIM), lambda b, i: (b, 0, i, 0)),
            pl.BlockSpec((1, tm, CONV_WIDTH), tok),
            pl.BlockSpec((1, tm, D_MODEL), tok),
            pl.BlockSpec((D_MODEL, D_MODEL), lambda b, i: (0, COL_MATTN // D_MODEL), **resident),
            pl.BlockSpec((D_MODEL, D_MODEL), lambda b, i: (0, COL_MCONV // D_MODEL), **resident),
            pl.BlockSpec((CONV_WIDTH, D_MODEL), const, **resident),
            pl.BlockSpec((ATTN_WIDTH, D_MODEL), const, **resident),
            pl.BlockSpec((D_MODEL, D_MODEL), const, **resident),
            pl.BlockSpec((1, D_MODEL), const),
            pl.BlockSpec((1, D_MODEL), const),
        ],
        out_specs=pl.BlockSpec((1, tm, D_MODEL), tok),
        out_shape=jax.ShapeDtypeStruct((bsz, seq, D_MODEL), F32),
        compiler_params=pltpu.CompilerParams(
            dimension_semantics=("arbitrary", "arbitrary"),
            vmem_limit_bytes=_vmem_limit(tiles + weights + temps)),
        name="out_proj",
    )(mod, o, spre, x, w_in, w_in, w_conv_out, w_attn_out, w_o, ln_g, ln_b)


def kernel(x, c, w_ada, b_ada, w_in, conv_w, conv_b, rel_bias, w_attn_out, w_conv_out, w_o, ln_g, ln_b):
    bsz = x.shape[0]
    depth = w_in.shape[0]
    alpha = (2.0 * depth) ** 0.25
    bias = _bias_call(rel_bias)
    for layer in range(depth):
        mod = _ada_call(c, w_ada[layer], b_ada[layer]).reshape(bsz, 1, 3 * D_MODEL)
        w_in_b = w_in[layer].astype(BF16)
        qkv1, qkv2, qkv3, sg, spre = _inproj_call(mod, x, w_in_b, conv_w[layer], conv_b[layer][None, :])
        o = _attn_call(qkv1, qkv2, qkv3, bias, sg)
        x = _outproj_call(mod, o, spre, x, w_in_b, w_conv_out[layer].astype(BF16),
                          w_attn_out[layer].astype(BF16), w_o[layer].astype(BF16),
                          ln_g[layer][None, :], ln_b[layer][None, :], alpha)
    return x
```

```python
import functools
import math

import numpy as np
import jax
import jax.numpy as jnp
from jax import lax
from jax.experimental import pallas as pl
from jax.experimental.pallas import tpu as pltpu

F32 = jnp.float32
BF16 = jnp.bfloat16

D_MODEL = 1024
HEAD_DIM = 128
HEADS_PER_GROUP = 4
DILATED_GROUPS = ((128, 1), (512, 4), (2048, 16))
N_GROUPS = len(DILATED_GROUPS)
N_ATTN_HEADS = N_GROUPS * HEADS_PER_GROUP
QKV_WIDTH = N_ATTN_HEADS * HEAD_DIM
ATTN_WIDTH = HEADS_PER_GROUP * HEAD_DIM
CONV_WIDTH = D_MODEL
CONV_K = 3
N_BUCKETS = 32
MAX_EXACT = 16
MAX_DISTANCE = 2048
BLOCK = 128
LN_EPS = 1e-5
NEG_INF = -1e30
LOG2E = math.log2(math.e)
Q_SCALE = HEAD_DIM ** -0.5 * LOG2E

COL_Q = 0
COL_K = QKV_WIDTH
COL_V = 2 * QKV_WIDTH
COL_GATTN = 3 * QKV_WIDTH
COL_U = COL_GATTN + ATTN_WIDTH
COL_B = COL_U + CONV_WIDTH
COL_C = COL_B + CONV_WIDTH
COL_GCONV = COL_C + CONV_WIDTH
COL_MATTN = COL_GCONV + CONV_WIDTH
COL_MCONV = COL_MATTN + D_MODEL
N_COLS = COL_MCONV + D_MODEL

V7X_VMEM_BYTES = 64 * 1024 * 1024
SUBLANES = 8
INPROJ_TM = 512
OUTPROJ_TM = 1024
OUTPROJ_ROWS = 256
CONV_CHUNK = 256
ATTN_BLOCKS_PER_STAGE = 2
MERGE_ROWS = 32


def _vmem_limit(estimate_bytes):
    return int(min(V7X_VMEM_BYTES - (4 << 20), estimate_bytes * 5 // 4 + (4 << 20)))


def _bucket_table():
    a = np.arange(BLOCK)[:, None]
    b = np.arange(2 * BLOCK)[None, :]
    steps = a + BLOCK - b
    out = []
    for window, dilation in DILATED_GROUPS:
        n_steps = window // dilation
        valid = (steps >= 0) & (steps <= n_steps)
        dist = np.maximum(steps, 0) * dilation
        n = np.maximum(dist, 1).astype(np.float32)
        large = MAX_EXACT + (np.log(n / np.float32(MAX_EXACT)) / np.float32(math.log(MAX_DISTANCE / MAX_EXACT))
                             * np.float32(N_BUCKETS - MAX_EXACT)).astype(np.int32)
        large = np.minimum(large, N_BUCKETS - 1)
        bucket = np.where(dist < MAX_EXACT, dist, large)
        out.append(np.where(valid, bucket, -1).astype(np.int32))
    return np.stack(out)


_BUCKETS = _bucket_table()


def _sigmoid(v):
    return 0.5 * jnp.tanh(0.5 * v) + 0.5


def _ada_kernel(c_ref, w_ref, b_ref, o_ref):
    c = c_ref[...]
    o_ref[...] = jnp.dot(c * _sigmoid(c), w_ref[...], preferred_element_type=F32) + b_ref[...]


def _ada_call(c, w_ada, b_ada):
    bsz = c.shape[0]
    return pl.pallas_call(
        _ada_kernel,
        out_shape=jax.ShapeDtypeStruct((bsz, 3 * D_MODEL), F32),
        name="ada_mod",
    )(c, w_ada, b_ada[None, :])


def _bias_kernel(rb_ref, bucket_ref, o_ref):
    for g in range(N_GROUPS):
        bk = bucket_ref[g]
        for j in range(HEADS_PER_GROUP):
            h = g * HEADS_PER_GROUP + j
            acc = jnp.full(bk.shape, NEG_INF, F32)
            for k in range(N_BUCKETS):
                acc = jnp.where(bk == k, rb_ref[k, h] * LOG2E, acc)
            o_ref[h, 0] = acc
            o_ref[h, 1] = jnp.concatenate(
                [acc[:, BLOCK:], jnp.full((BLOCK, BLOCK), NEG_INF, F32)], axis=1)


def _bias_call(rel_bias):
    return pl.pallas_call(
        _bias_kernel,
        out_shape=jax.ShapeDtypeStruct((N_ATTN_HEADS, 2, BLOCK, 2 * BLOCK), F32),
        in_specs=[pl.BlockSpec(memory_space=pltpu.SMEM), pl.BlockSpec(memory_space=pltpu.VMEM)],
        out_specs=pl.BlockSpec(memory_space=pltpu.VMEM),
        name="bias_tables",
    )(rel_bias, jnp.asarray(_BUCKETS))


def _inproj_kernel(mod_ref, x_ref, w_ref, cw_ref, cb_ref,
                   qkv1_ref, qkv2_ref, qkv3_ref, sg_ref, spre_ref,
                   hs_ref, hb_ref, zbuf_ref, *, tm):
    i = pl.program_id(1)

    @pl.when(i == 0)
    def _():
        zbuf_ref[0:SUBLANES, :] = jnp.zeros((SUBLANES, CONV_WIDTH), F32)

    shift = mod_ref[0, :, 0:D_MODEL]
    scale1 = 1.0 + mod_ref[0, :, D_MODEL:2 * D_MODEL]
    h = x_ref[0] * scale1 + shift
    hb_ref[0] = h.astype(BF16)

    n_slab = D_MODEL // HEAD_DIM
    for s in range(n_slab):
        hs_ref[s] = h[:, s * HEAD_DIM:(s + 1) * HEAD_DIM]

    def by_residue(d):
        return jnp.concatenate(
            [jnp.concatenate([hs_ref[s, pl.ds(r, tm // d, stride=d), :] for s in range(n_slab)], axis=1)
             for r in range(d)], axis=0).astype(BF16)
    d4, d16 = DILATED_GROUPS[1][1], DILATED_GROUPS[2][1]
    hb_ref[1] = by_residue(d4)
    hb_ref[2] = by_residue(d16)

    def proj(g, c0, width):
        return jnp.dot(hb_ref[g], w_ref[:, c0:c0 + width], preferred_element_type=F32)

    def qkv(g, store):
        for t, col in enumerate((COL_Q, COL_K, COL_V)):
            r = proj(g, col + g * ATTN_WIDTH, ATTN_WIDTH)
            if t == 0:
                r = r * Q_SCALE
            r = r.astype(BF16)
            for j in range(HEADS_PER_GROUP):
                store(t, j, r[:, j * HEAD_DIM:(j + 1) * HEAD_DIM])

    def store1(t, j, v):
        qkv1_ref[0, t, j] = v
    qkv(0, store1)

    def store_sub(ref, d):
        def store(t, j, v):
            rows = tm // d
            for r in range(d):
                ref[0, t, j, r] = v[r * rows:(r + 1) * rows]
        return store
    qkv(1, store_sub(qkv2_ref, d4))
    qkv(2, store_sub(qkv3_ref, d16))

    ga = proj(0, COL_GATTN, ATTN_WIDTH)
    sgv = (ga * _sigmoid(ga)).astype(BF16)
    for j in range(HEADS_PER_GROUP):
        sg_ref[0, j] = sgv[:, j * HEAD_DIM:(j + 1) * HEAD_DIM]

    for c0 in range(0, CONV_WIDTH, CONV_CHUNK):
        cs = slice(c0, c0 + CONV_CHUNK)
        u = proj(0, COL_U + c0, CONV_CHUNK)
        cg = proj(0, COL_C + c0, CONV_CHUNK)
        z = cg * u
        zbuf_ref[SUBLANES:SUBLANES + tm, cs] = z
        zm1 = zbuf_ref[SUBLANES - 1:SUBLANES - 1 + tm, cs]
        zm2 = zbuf_ref[SUBLANES - 2:SUBLANES - 2 + tm, cs]
        y = cw_ref[0:1, cs] * zm2 + cw_ref[1:2, cs] * zm1 + cw_ref[2:3, cs] * z + cb_ref[:, cs]
        bg = proj(0, COL_B + c0, CONV_CHUNK)
        gc = proj(0, COL_GCONV + c0, CONV_CHUNK)
        spre_ref[0, :, cs] = (bg * y * (gc * _sigmoid(gc))).astype(BF16)
    zbuf_ref[0:SUBLANES, :] = zbuf_ref[tm:tm + SUBLANES, :]


def _inproj_call(mod, x, w_in, conv_w, conv_b):
    bsz, seq, _ = x.shape
    tm = INPROJ_TM
    n_t = seq // tm
    d4, d16 = DILATED_GROUPS[1][1], DILATED_GROUPS[2][1]
    sub4, sub16 = seq // d4, seq // d16
    assert seq % tm == 0 and tm % (d16 * 16) == 0
    hg = HEADS_PER_GROUP

    in_specs = [
        pl.BlockSpec((1, 1, 3 * D_MODEL), lambda b, i: (b, 0, 0)),
        pl.BlockSpec((1, tm, D_MODEL), lambda b, i: (b, i, 0)),
        pl.BlockSpec((D_MODEL, COL_MATTN), lambda b, i: (0, 0), pipeline_mode=pl.Buffered(1)),
        pl.BlockSpec((CONV_K, CONV_WIDTH), lambda b, i: (0, 0)),
        pl.BlockSpec((1, CONV_WIDTH), lambda b, i: (0, 0)),
    ]
    out_shape = [
        jax.ShapeDtypeStruct((bsz, 3, hg, seq, HEAD_DIM), BF16),
        jax.ShapeDtypeStruct((bsz, 3, hg, d4, sub4, HEAD_DIM), BF16),
        jax.ShapeDtypeStruct((bsz, 3, hg, d16, sub16, HEAD_DIM), BF16),
        jax.ShapeDtypeStruct((bsz, hg, seq, HEAD_DIM), BF16),
        jax.ShapeDtypeStruct((bsz, seq, CONV_WIDTH), BF16),
    ]
    out_specs = [
        pl.BlockSpec((1, 3, hg, tm, HEAD_DIM), lambda b, i: (b, 0, 0, i, 0)),
        pl.BlockSpec((1, 3, hg, d4, tm // d4, HEAD_DIM), lambda b, i: (b, 0, 0, 0, i, 0)),
        pl.BlockSpec((1, 3, hg, d16, tm // d16, HEAD_DIM), lambda b, i: (b, 0, 0, 0, i, 0)),
        pl.BlockSpec((1, hg, tm, HEAD_DIM), lambda b, i: (b, 0, i, 0)),
        pl.BlockSpec((1, tm, CONV_WIDTH), lambda b, i: (b, i, 0)),
    ]
    weights = D_MODEL * COL_MATTN * 2
    x_tiles = tm * D_MODEL * 4 * 2
    out_tiles = (3 * 3 * ATTN_WIDTH + ATTN_WIDTH + CONV_WIDTH) * tm * 2 * 2
    scratch = tm * D_MODEL * 4 + N_GROUPS * tm * D_MODEL * 2 + (tm + SUBLANES) * CONV_WIDTH * 4
    return pl.pallas_call(
        functools.partial(_inproj_kernel, tm=tm),
        grid=(bsz, n_t),
        in_specs=in_specs,
        out_specs=out_specs,
        out_shape=out_shape,
        scratch_shapes=[pltpu.VMEM((D_MODEL // HEAD_DIM, tm, HEAD_DIM), F32),
                        pltpu.VMEM((N_GROUPS, tm, D_MODEL), BF16),
                        pltpu.VMEM((tm + SUBLANES, CONV_WIDTH), F32)],
        compiler_params=pltpu.CompilerParams(
            dimension_semantics=("arbitrary", "arbitrary"),
            vmem_limit_bytes=_vmem_limit(weights + x_tiles + out_tiles + scratch)),
        name="in_proj",
    )(mod, x, w_in, conv_w, conv_b)


def _nt_dot(q, k):
    return lax.dot_general(q, k, (((1,), (1,)), ((), ())), preferred_element_type=F32)


def _softmax_pv(s, v):
    m = jnp.max(s, axis=1, keepdims=True)
    p = jnp.exp2(s - m)
    v_ext = jnp.concatenate([v, jnp.ones((v.shape[0], HEAD_DIM), BF16)], axis=1)
    acc = jnp.dot(p.astype(BF16), v_ext, preferred_element_type=F32)
    return acc[:, :HEAD_DIM], m, acc[:, HEAD_DIM:]


def _attn_kernel(qkv1_ref, qkv2_ref, qkv3_ref, bias_ref, sg_ref, o_ref,
                 sa1_ref, sa2_ref, sa3_ref, sb1_ref, sb2_ref, sb3_ref, *row_refs, seq):
    res1, res2, res3, stage3 = (row_refs[3 * g:3 * g + 3] for g in range(N_GROUPS + 1))
    step = pl.program_id(0)
    par = jnp.bitwise_and(step, 1)
    opar = 1 - par

    @pl.when(step == 0)
    def _():
        for r in res1 + res2 + res3:
            r[1] = jnp.ones((seq, HEAD_DIM), F32)

    d4, d16 = DILATED_GROUPS[1][1], DILATED_GROUPS[2][1]
    per = ATTN_BLOCKS_PER_STAGE
    n_it = seq // BLOCK // per
    blk4 = seq // d4 // BLOCK
    assert seq // d16 == BLOCK and blk4 & (blk4 - 1) == 0 and n_it % 2 == 0 and n_it * per * BLOCK == seq
    assert d16 == d4 * d4 and d4 == blk4
    shift4 = blk4.bit_length() - 1
    sub4 = seq // d4

    def rows(start, size, align=BLOCK):
        if isinstance(start, int):
            return pl.ds(start, size)
        return pl.ds(pl.multiple_of(start, align), size)

    def plan(it, u):
        it = it * per + u
        if isinstance(it, int):
            f1, nb, r4 = int(it == 0), it & (blk4 - 1), it >> shift4
            f2 = int(nb == 0)
        else:
            f1 = (it == 0).astype(jnp.int32)
            nb = jnp.bitwise_and(it, blk4 - 1)
            f2 = (nb == 0).astype(jnp.int32)
            r4 = lax.shift_right_logical(it, shift4)
        return dict(q=it * BLOCK, k1=(it - 1 + f1) * BLOCK, f1=f1, k2=(it - 1 + f2) * BLOCK, f2=f2,
                    out2=nb * (BLOCK * d4) + r4, out3=nb * sub4 + r4)

    def score_block(it, u, g, s_refs):
        p = plan(it, u)
        q = rows(p["q"], BLOCK)
        if g == 0:
            s_refs[0][u] = _nt_dot(qkv1_ref[0, 0, 0, q, :], qkv1_ref[0, 1, 0, rows(p["k1"], 2 * BLOCK), :]) \
                + bias_ref[0, 0, p["f1"]]
        elif g == 1:
            s_refs[1][u] = _nt_dot(qkv2_ref[0, 0, 0, q, :], qkv2_ref[0, 1, 0, rows(p["k2"], 2 * BLOCK), :]) \
                + bias_ref[1, 0, p["f2"]]
        else:
            s_refs[2][u] = _nt_dot(qkv3_ref[0, 0, 0, q, :], qkv3_ref[0, 1, 0, q, :]) \
                + bias_ref[2, 0, 1, :, 0:BLOCK]

    def put(dst, lead, idx, acc, m, l):
        dst[0][(*lead, idx, slice(None))] = acc
        dst[1][(*lead, idx, slice(None))] = jnp.broadcast_to(m, (BLOCK, HEAD_DIM))
        dst[2][(*lead, idx, slice(None))] = l

    def finish_block(it, u, g, s_refs):
        p = plan(it, u)
        q = rows(p["q"], BLOCK)
        if g == 0:
            put(res1, (par,), q, *_softmax_pv(s_refs[0][u], qkv1_ref[0, 2, 0, rows(p["k1"], 2 * BLOCK), :]))
        elif g == 1:
            put(res2, (par,), pl.ds(p["out2"], BLOCK, stride=d4),
                *_softmax_pv(s_refs[1][u], qkv2_ref[0, 2, 0, rows(p["k2"], 2 * BLOCK), :]))
        else:
            put(stage3, (), pl.ds(p["out3"], BLOCK, stride=d4),
                *_softmax_pv(s_refs[2][u], qkv3_ref[0, 2, 0, q, :]))

    def merge(base, n_rows):
        for c0 in range(0, n_rows, MERGE_ROWS):
            rws = rows(base + c0, MERGE_ROWS, MERGE_ROWS)
            ms = [r[1][opar, rws, :] for r in (res1, res2, res3)]
            m = jnp.maximum(jnp.maximum(ms[0], ms[1]), ms[2])
            es = [jnp.exp2(mg - m) for mg in ms]
            num = sum(e * r[0][opar, rws, :] for e, r in zip(es, (res1, res2, res3)))
            den = sum(e * r[2][opar, rws, :] for e, r in zip(es, (res1, res2, res3)))
            o_ref[0, 0, rws, :] = (num * (1.0 / den)).astype(BF16) * sg_ref[0, 0, rws, :]

    def section(it_scores, s_dst, it_finish, s_src, merge_base, merge_rows):
        n_p = N_GROUPS * per
        cuts = [MERGE_ROWS * ((merge_rows // MERGE_ROWS) * k // n_p) for k in range(n_p + 1)]
        for u in range(per):
            for g in range(N_GROUPS):
                k = N_GROUPS * u + g
                merge(merge_base + cuts[k], cuts[k + 1] - cuts[k])
                if it_scores is not None:
                    score_block(it_scores, u, g, s_dst)
                if it_finish is not None:
                    finish_block(it_finish, u, g, s_src)

    sec = seq // n_it
    s_a = (sa1_ref, sa2_ref, sa3_ref)
    s_b = (sb1_ref, sb2_ref, sb3_ref)
    section(0, s_a, None, None, 0, sec // 2)

    def body(i, carry):
        base = sec // 2 + 2 * i * sec
        section(2 * i + 1, s_b, 2 * i, s_a, base, sec)
        section(2 * i + 2, s_a, 2 * i + 1, s_b, base + sec, sec)
        return carry
    lax.fori_loop(0, n_it // 2 - 1, body, 0)
    tail = sec // 2 + (n_it - 2) * sec
    section(n_it - 1, s_b, n_it - 2, s_a, tail, sec)
    section(None, None, n_it - 1, s_b, tail + sec, seq - tail - sec)

    for c in range(d4):
        for p0 in range(0, sub4, BLOCK):
            src = slice(c * sub4 + p0, c * sub4 + p0 + BLOCK)
            dst = pl.ds(d4 * p0 + c, BLOCK, stride=d4)
            for k in range(3):
                res3[k][par, dst, :] = stage3[k][src, :]


def _attn_call(qkv1, qkv2, qkv3, bias, sg):
    bsz, _, hg, seq, _ = qkv1.shape
    qkv2 = qkv2.reshape(qkv1.shape)
    qkv3 = qkv3.reshape(qkv1.shape)
    bias = bias.reshape(N_GROUPS, hg, 2, BLOCK, 2 * BLOCK)
    n_slots = bsz * hg

    def cur(s):
        c = jnp.minimum(s, n_slots - 1)
        return c // hg, c % hg

    def prev(s):
        p = jnp.maximum(s - 1, 0)
        return p // hg, p % hg

    def qkv_map(s):
        b, j = cur(s)
        return (b, 0, j, 0, 0)

    def bias_map(s):
        return (0, cur(s)[1], 0, 0, 0)

    def lag_map(s):
        b, j = prev(s)
        return (b, j, 0, 0)

    in_tiles = (3 * 3 + 1) * seq * HEAD_DIM * 2 * 2 + N_GROUPS * 2 * BLOCK * 2 * BLOCK * 4 * 2
    out_tiles = seq * HEAD_DIM * 2 * 2
    per = ATTN_BLOCKS_PER_STAGE
    score_shapes = [(per, BLOCK, 2 * BLOCK), (per, BLOCK, 2 * BLOCK), (per, BLOCK, BLOCK)] * 2
    row_shapes = [(2, seq, HEAD_DIM)] * (3 * N_GROUPS) + [(seq, HEAD_DIM)] * 3
    scratch = sum(math.prod(sh) * 4 for sh in score_shapes + row_shapes)
    qkv_spec = pl.BlockSpec((1, 3, 1, seq, HEAD_DIM), qkv_map)
    return pl.pallas_call(
        functools.partial(_attn_kernel, seq=seq),
        grid=(n_slots + 1,),
        in_specs=[
            qkv_spec, qkv_spec, qkv_spec,
            pl.BlockSpec((N_GROUPS, 1, 2, BLOCK, 2 * BLOCK), bias_map),
            pl.BlockSpec((1, 1, seq, HEAD_DIM), lag_map),
        ],
        out_specs=pl.BlockSpec((1, 1, seq, HEAD_DIM), lag_map),
        out_shape=jax.ShapeDtypeStruct((bsz, hg, seq, HEAD_DIM), BF16),
        scratch_shapes=[pltpu.VMEM(sh, F32) for sh in score_shapes + row_shapes],
        compiler_params=pltpu.CompilerParams(
            dimension_semantics=("arbitrary",),
            vmem_limit_bytes=_vmem_limit(in_tiles + out_tiles + scratch)),
        name="dilated_attn",
    )(qkv1, qkv2, qkv3, bias, sg)


def _outproj_kernel(mod_ref, o_ref, spre_ref, x_ref, wma_ref, wmc_ref, wco_ref, wao_ref, wo_ref,
                    lng_ref, lnb_ref, out_ref, *, alpha, tm):
    shift = mod_ref[0, :, 0:D_MODEL]
    scale1 = 1.0 + mod_ref[0, :, D_MODEL:2 * D_MODEL]
    gate1 = 1.0 + mod_ref[0, :, 2 * D_MODEL:3 * D_MODEL]
    for r0 in range(0, tm, OUTPROJ_ROWS):
        rws = slice(r0, r0 + OUTPROJ_ROWS)
        xr = x_ref[0, rws, :]
        h = (xr * scale1 + shift).astype(BF16)
        g_attn = _sigmoid(jnp.dot(h, wma_ref[...], preferred_element_type=F32))
        g_conv = _sigmoid(jnp.dot(h, wmc_ref[...], preferred_element_type=F32))
        a_in = jnp.concatenate([o_ref[0, j, rws, :] for j in range(HEADS_PER_GROUP)], axis=1)
        a_out = jnp.dot(a_in, wao_ref[...], preferred_element_type=F32)
        s_out = jnp.dot(spre_ref[0, rws, :], wco_ref[...], preferred_element_type=F32)
        merged = (g_attn * a_out + g_conv * s_out).astype(BF16)
        y = jnp.dot(merged, wo_ref[...], preferred_element_type=F32)
        v = alpha * xr + gate1 * y
        mu = jnp.mean(v, axis=-1, keepdims=True)
        dv = v - mu
        var = jnp.mean(dv * dv, axis=-1, keepdims=True)
        out_ref[0, rws, :] = dv * lax.rsqrt(var + LN_EPS) * lng_ref[...] + lnb_ref[...]


def _outproj_call(mod, o, spre, x, w_in, w_conv_out, w_attn_out, w_o, ln_g, ln_b, alpha):
    bsz, seq, _ = x.shape
    tm = OUTPROJ_TM
    assert seq % tm == 0 and tm % OUTPROJ_ROWS == 0
    assert COL_MATTN % D_MODEL == 0 and COL_MCONV % D_MODEL == 0
    hg = HEADS_PER_GROUP
    tok = lambda b, i: (b, i, 0)
    const = lambda b, i: (0, 0)
    resident = dict(pipeline_mode=pl.Buffered(1))
    tiles = (ATTN_WIDTH * 2 + CONV_WIDTH * 2 + 2 * D_MODEL * 4) * tm * 2
    weights = (2 * D_MODEL + CONV_WIDTH + ATTN_WIDTH + D_MODEL) * D_MODEL * 2
    temps = 8 * OUTPROJ_ROWS * D_MODEL * 4
    return pl.pallas_call(
        functools.partial(_outproj_kernel, alpha=alpha, tm=tm),
        grid=(bsz, seq // tm),
        in_specs=[
            pl.BlockSpec((1, 1, 3 * D_MODEL), lambda b, i: (b, 0, 0)),
            pl.BlockSpec((1, hg, tm, HEAD_DIM), lambda b, i: (b, 0, i, 0)),
            pl.BlockSpec((1, tm, CONV_WIDTH), tok),
            pl.BlockSpec((1, tm, D_MODEL), tok),
            pl.BlockSpec((D_MODEL, D_MODEL), lambda b, i: (0, COL_MATTN // D_MODEL), **resident),
            pl.BlockSpec((D_MODEL, D_MODEL), lambda b, i: (0, COL_MCONV // D_MODEL), **resident),
            pl.BlockSpec((CONV_WIDTH, D_MODEL), const, **resident),
            pl.BlockSpec((ATTN_WIDTH, D_MODEL), const, **resident),
            pl.BlockSpec((D_MODEL, D_MODEL), const, **resident),
            pl.BlockSpec((1, D_MODEL), const),
            pl.BlockSpec((1, D_MODEL), const),
        ],
        out_specs=pl.BlockSpec((1, tm, D_MODEL), tok),
        out_shape=jax.ShapeDtypeStruct((bsz, seq, D_MODEL), F32),
        compiler_params=pltpu.CompilerParams(
            dimension_semantics=("arbitrary", "arbitrary"),
            vmem_limit_bytes=_vmem_limit(tiles + weights + temps)),
        name="out_proj",
    )(mod, o, spre, x, w_in, w_in, w_conv_out, w_attn_out, w_o, ln_g, ln_b)


def kernel(x, c, w_ada, b_ada, w_in, conv_w, conv_b, rel_bias, w_attn_out, w_conv_out, w_o, ln_g, ln_b):
    bsz = x.shape[0]
    depth = w_in.shape[0]
    alpha = (2.0 * depth) ** 0.25
    bias = _bias_call(rel_bias)
    for layer in range(depth):
        mod = _ada_call(c, w_ada[layer], b_ada[layer]).reshape(bsz, 1, 3 * D_MODEL)
        w_in_b = w_in[layer].astype(BF16)
        qkv1, qkv2, qkv3, sg, spre = _inproj_call(mod, x, w_in_b, conv_w[layer], conv_b[layer][None, :])
        o = _attn_call(qkv1, qkv2, qkv3, bias, sg)
        x = _outproj_call(mod, o, spre, x, w_in_b, w_conv_out[layer].astype(BF16),
                          w_attn_out[layer].astype(BF16), w_o[layer].astype(BF16),
                          ln_g[layer][None, :], ln_b[layer][None, :], alpha)
    return x
```

```python
import functools
import math

import numpy as np
import jax
import jax.numpy as jnp
from jax import lax
from jax.experimental import pallas as pl
from jax.experimental.pallas import tpu as pltpu

F32 = jnp.float32
BF16 = jnp.bfloat16

D_MODEL = 1024
HEAD_DIM = 128
HEADS_PER_GROUP = 4
DILATED_GROUPS = ((128, 1), (512, 4), (2048, 16))
N_GROUPS = len(DILATED_GROUPS)
N_ATTN_HEADS = N_GROUPS * HEADS_PER_GROUP
QKV_WIDTH = N_ATTN_HEADS * HEAD_DIM
ATTN_WIDTH = HEADS_PER_GROUP * HEAD_DIM
CONV_WIDTH = D_MODEL
CONV_K = 3
N_BUCKETS = 32
MAX_EXACT = 16
MAX_DISTANCE = 2048
BLOCK = 128
LN_EPS = 1e-5
NEG_INF = -1e30
LOG2E = math.log2(math.e)
Q_SCALE = HEAD_DIM ** -0.5 * LOG2E

COL_Q = 0
COL_K = QKV_WIDTH
COL_V = 2 * QKV_WIDTH
COL_GATTN = 3 * QKV_WIDTH
COL_U = COL_GATTN + ATTN_WIDTH
COL_B = COL_U + CONV_WIDTH
COL_C = COL_B + CONV_WIDTH
COL_GCONV = COL_C + CONV_WIDTH
COL_MATTN = COL_GCONV + CONV_WIDTH
COL_MCONV = COL_MATTN + D_MODEL
N_COLS = COL_MCONV + D_MODEL

V7X_VMEM_BYTES = 64 * 1024 * 1024
SUBLANES = 8
INPROJ_TM = 512
OUTPROJ_TM = 1024
OUTPROJ_ROWS = 256
CONV_CHUNK = 256
ATTN_BLOCKS_PER_STAGE = 2
MERGE_ROWS = 32


def _vmem_limit(estimate_bytes):
    return int(min(V7X_VMEM_BYTES - (4 << 20), estimate_bytes * 5 // 4 + (4 << 20)))


def _bucket_table():
    a = np.arange(BLOCK)[:, None]
    b = np.arange(2 * BLOCK)[None, :]
    steps = a + BLOCK - b
    out = []
    for window, dilation in DILATED_GROUPS:
        n_steps = window // dilation
        valid = (steps >= 0) & (steps <= n_steps)
        dist = np.maximum(steps, 0) * dilation
        n = np.maximum(dist, 1).astype(np.float32)
        large = MAX_EXACT + (np.log(n / np.float32(MAX_EXACT)) / np.float32(math.log(MAX_DISTANCE / MAX_EXACT))
                             * np.float32(N_BUCKETS - MAX_EXACT)).astype(np.int32)
        large = np.minimum(large, N_BUCKETS - 1)
        bucket = np.where(dist < MAX_EXACT, dist, large)
        out.append(np.where(valid, bucket, -1).astype(np.int32))
    return np.stack(out)


_BUCKETS = _bucket_table()


def _sigmoid(v):
    return 0.5 * jnp.tanh(0.5 * v) + 0.5


def _ada_kernel(c_ref, w_ref, b_ref, o_ref):
    c = c_ref[...]
    o_ref[...] = jnp.dot(c * _sigmoid(c), w_ref[...], preferred_element_type=F32) + b_ref[...]


def _ada_call(c, w_ada, b_ada):
    bsz = c.shape[0]
    return pl.pallas_call(
        _ada_kernel,
        out_shape=jax.ShapeDtypeStruct((bsz, 3 * D_MODEL), F32),
        name="ada_mod",
    )(c, w_ada, b_ada[None, :])


def _bias_kernel(rb_ref, bucket_ref, o_ref):
    for g in range(N_GROUPS):
        bk = bucket_ref[g]
        for j in range(HEADS_PER_GROUP):
            h = g * HEADS_PER_GROUP + j
            acc = jnp.full(bk.shape, NEG_INF, F32)
            for k in range(N_BUCKETS):
                acc = jnp.where(bk == k, rb_ref[k, h] * LOG2E, acc)
            o_ref[h, 0] = acc
            o_ref[h, 1] = jnp.concatenate(
                [acc[:, BLOCK:], jnp.full((BLOCK, BLOCK), NEG_INF, F32)], axis=1)


def _bias_call(rel_bias):
    return pl.pallas_call(
        _bias_kernel,
        out_shape=jax.ShapeDtypeStruct((N_ATTN_HEADS, 2, BLOCK, 2 * BLOCK), F32),
        in_specs=[pl.BlockSpec(memory_space=pltpu.SMEM), pl.BlockSpec(memory_space=pltpu.VMEM)],
        out_specs=pl.BlockSpec(memory_space=pltpu.VMEM),
        name="bias_tables",
    )(rel_bias, jnp.asarray(_BUCKETS))


def _inproj_kernel(mod_ref, x_ref, w_ref, cw_ref, cb_ref,
                   qkv1_ref, qkv2_ref, qkv3_ref, sg_ref, spre_ref,
                   hs_ref, hs4_ref, hb_ref, zbuf_ref, *, tm):
    i = pl.program_id(1)

    @pl.when(i == 0)
    def _():
        zbuf_ref[0:SUBLANES, :] = jnp.zeros((SUBLANES, CONV_WIDTH), F32)

    n_slab = D_MODEL // HEAD_DIM
    for s in range(n_slab):
        ls = slice(s * HEAD_DIM, (s + 1) * HEAD_DIM)
        h = x_ref[0, :, ls] * (1.0 + mod_ref[0, :, D_MODEL + s * HEAD_DIM:D_MODEL + (s + 1) * HEAD_DIM]) \
            + mod_ref[0, :, ls]
        hb_ref[0, :, ls] = h.astype(BF16)
        hs_ref[s] = h

    d4, d16 = DILATED_GROUPS[1][1], DILATED_GROUPS[2][1]
    assert d16 == d4 * d4
    n4, n16 = tm // d4, tm // d16
    for s in range(n_slab):
        ls = slice(s * HEAD_DIM, (s + 1) * HEAD_DIM)
        for r in range(d4):
            part = hs_ref[s, pl.ds(r, n4, stride=d4), :]
            hs4_ref[s, r * n4:(r + 1) * n4, :] = part
            hb_ref[1, r * n4:(r + 1) * n4, ls] = part.astype(BF16)
    for s in range(n_slab):
        ls = slice(s * HEAD_DIM, (s + 1) * HEAD_DIM)
        for r16 in range(d16):
            part = hs4_ref[s, pl.ds((r16 % d4) * n4 + r16 // d4, n16, stride=d4), :]
            hb_ref[2, r16 * n16:(r16 + 1) * n16, ls] = part.astype(BF16)

    def proj(g, c0, width):
        return jnp.dot(hb_ref[g], w_ref[:, c0:c0 + width], preferred_element_type=F32)

    def qkv(g, store):
        for t, col in enumerate((COL_Q, COL_K, COL_V)):
            r = proj(g, col + g * ATTN_WIDTH, ATTN_WIDTH)
            if t == 0:
                r = r * Q_SCALE
            r = r.astype(BF16)
            for j in range(HEADS_PER_GROUP):
                store(t, j, r[:, j * HEAD_DIM:(j + 1) * HEAD_DIM])

    def store1(t, j, v):
        qkv1_ref[0, t, j] = v
    qkv(0, store1)

    ga = proj(0, COL_GATTN, ATTN_WIDTH)
    sgv = (ga * _sigmoid(ga)).astype(BF16)
    for j in range(HEADS_PER_GROUP):
        sg_ref[0, j] = sgv[:, j * HEAD_DIM:(j + 1) * HEAD_DIM]

    for c0 in range(0, CONV_WIDTH, CONV_CHUNK):
        cs = slice(c0, c0 + CONV_CHUNK)
        u = proj(0, COL_U + c0, CONV_CHUNK)
        cg = proj(0, COL_C + c0, CONV_CHUNK)
        z = cg * u
        zbuf_ref[SUBLANES:SUBLANES + tm, cs] = z
        zm1 = zbuf_ref[SUBLANES - 1:SUBLANES - 1 + tm, cs]
        zm2 = zbuf_ref[SUBLANES - 2:SUBLANES - 2 + tm, cs]
        y = cw_ref[0:1, cs] * zm2 + cw_ref[1:2, cs] * zm1 + cw_ref[2:3, cs] * z + cb_ref[:, cs]
        bg = proj(0, COL_B + c0, CONV_CHUNK)
        gc = proj(0, COL_GCONV + c0, CONV_CHUNK)
        spre_ref[0, :, cs] = (bg * y * (gc * _sigmoid(gc))).astype(BF16)
    zbuf_ref[0:SUBLANES, :] = zbuf_ref[tm:tm + SUBLANES, :]

    def store_sub(ref, d):
        def store(t, j, v):
            rows = tm // d
            for r in range(d):
                ref[0, t, j, r] = v[r * rows:(r + 1) * rows]
        return store
    qkv(1, store_sub(qkv2_ref, d4))
    qkv(2, store_sub(qkv3_ref, d16))


def _inproj_call(mod, x, w_in, conv_w, conv_b):
    bsz, seq, _ = x.shape
    tm = INPROJ_TM
    n_t = seq // tm
    d4, d16 = DILATED_GROUPS[1][1], DILATED_GROUPS[2][1]
    sub4, sub16 = seq // d4, seq // d16
    assert seq % tm == 0 and tm % (d16 * 16) == 0
    hg = HEADS_PER_GROUP

    in_specs = [
        pl.BlockSpec((1, 1, 3 * D_MODEL), lambda b, i: (b, 0, 0)),
        pl.BlockSpec((1, tm, D_MODEL), lambda b, i: (b, i, 0)),
        pl.BlockSpec((D_MODEL, COL_MATTN), lambda b, i: (0, 0), pipeline_mode=pl.Buffered(1)),
        pl.BlockSpec((CONV_K, CONV_WIDTH), lambda b, i: (0, 0)),
        pl.BlockSpec((1, CONV_WIDTH), lambda b, i: (0, 0)),
    ]
    out_shape = [
        jax.ShapeDtypeStruct((bsz, 3, hg, seq, HEAD_DIM), BF16),
        jax.ShapeDtypeStruct((bsz, 3, hg, d4, sub4, HEAD_DIM), BF16),
        jax.ShapeDtypeStruct((bsz, 3, hg, d16, sub16, HEAD_DIM), BF16),
        jax.ShapeDtypeStruct((bsz, hg, seq, HEAD_DIM), BF16),
        jax.ShapeDtypeStruct((bsz, seq, CONV_WIDTH), BF16),
    ]
    out_specs = [
        pl.BlockSpec((1, 3, hg, tm, HEAD_DIM), lambda b, i: (b, 0, 0, i, 0)),
        pl.BlockSpec((1, 3, hg, d4, tm // d4, HEAD_DIM), lambda b, i: (b, 0, 0, 0, i, 0)),
        pl.BlockSpec((1, 3, hg, d16, tm // d16, HEAD_DIM), lambda b, i: (b, 0, 0, 0, i, 0)),
        pl.BlockSpec((1, hg, tm, HEAD_DIM), lambda b, i: (b, 0, i, 0)),
        pl.BlockSpec((1, tm, CONV_WIDTH), lambda b, i: (b, i, 0)),
    ]
    weights = D_MODEL * COL_MATTN * 2
    x_tiles = tm * D_MODEL * 4 * 2
    out_tiles = (3 * 3 * ATTN_WIDTH + ATTN_WIDTH + CONV_WIDTH) * tm * 2 * 2
    scratch = 2 * tm * D_MODEL * 4 + N_GROUPS * tm * D_MODEL * 2 + (tm + SUBLANES) * CONV_WIDTH * 4
    return pl.pallas_call(
        functools.partial(_inproj_kernel, tm=tm),
        grid=(bsz, n_t),
        in_specs=in_specs,
        out_specs=out_specs,
        out_shape=out_shape,
        scratch_shapes=[pltpu.VMEM((D_MODEL // HEAD_DIM, tm, HEAD_DIM), F32),
                        pltpu.VMEM((D_MODEL // HEAD_DIM, tm, HEAD_DIM), F32),
                        pltpu.VMEM((N_GROUPS, tm, D_MODEL), BF16),
                        pltpu.VMEM((tm + SUBLANES, CONV_WIDTH), F32)],
        compiler_params=pltpu.CompilerParams(
            dimension_semantics=("arbitrary", "arbitrary"),
            vmem_limit_bytes=_vmem_limit(weights + x_tiles + out_tiles + scratch)),
        name="in_proj",
    )(mod, x, w_in, conv_w, conv_b)


def _nt_dot(q, k):
    return lax.dot_general(q, k, (((1,), (1,)), ((), ())), preferred_element_type=F32)


def _softmax_pv(s, v):
    m = jnp.max(s, axis=1, keepdims=True)
    p = jnp.exp2(s - m)
    v_ext = jnp.concatenate([v, jnp.ones((v.shape[0], HEAD_DIM), BF16)], axis=1)
    acc = jnp.dot(p.astype(BF16), v_ext, preferred_element_type=F32)
    return acc[:, :HEAD_DIM], m, acc[:, HEAD_DIM:]


def _attn_kernel(qkv1_ref, qkv2_ref, qkv3_ref, bias_ref, sg_ref, o_ref,
                 sa1_ref, sa2_ref, sa3_ref, sb1_ref, sb2_ref, sb3_ref, *row_refs, seq):
    res1, res2, res3, stage3 = (row_refs[3 * g:3 * g + 3] for g in range(N_GROUPS + 1))
    step = pl.program_id(0)
    par = jnp.bitwise_and(step, 1)
    opar = 1 - par

    @pl.when(step == 0)
    def _():
        for r in res1 + res2 + res3:
            r[1] = jnp.ones((seq, HEAD_DIM), F32)

    d4, d16 = DILATED_GROUPS[1][1], DILATED_GROUPS[2][1]
    per = ATTN_BLOCKS_PER_STAGE
    n_it = seq // BLOCK // per
    blk4 = seq // d4 // BLOCK
    assert seq // d16 == BLOCK and blk4 & (blk4 - 1) == 0 and n_it % 2 == 0 and n_it * per * BLOCK == seq
    assert d16 == d4 * d4 and d4 == blk4
    shift4 = blk4.bit_length() - 1
    sub4 = seq // d4

    def rows(start, size, align=BLOCK):
        if isinstance(start, int):
            return pl.ds(start, size)
        return pl.ds(pl.multiple_of(start, align), size)

    def plan(it, u):
        it = it * per + u
        if isinstance(it, int):
            f1, nb, r4 = int(it == 0), it & (blk4 - 1), it >> shift4
            f2 = int(nb == 0)
        else:
            f1 = (it == 0).astype(jnp.int32)
            nb = jnp.bitwise_and(it, blk4 - 1)
            f2 = (nb == 0).astype(jnp.int32)
            r4 = lax.shift_right_logical(it, shift4)
        return dict(q=it * BLOCK, k1=(it - 1 + f1) * BLOCK, f1=f1, k2=(it - 1 + f2) * BLOCK, f2=f2,
                    out2=nb * (BLOCK * d4) + r4, out3=nb * sub4 + r4)

    def score_block(it, u, g, s_refs):
        p = plan(it, u)
        q = rows(p["q"], BLOCK)
        if g == 0:
            s_refs[0][u] = _nt_dot(qkv1_ref[0, 0, 0, q, :], qkv1_ref[0, 1, 0, rows(p["k1"], 2 * BLOCK), :]) \
                + bias_ref[0, 0, p["f1"]]
        elif g == 1:
            s_refs[1][u] = _nt_dot(qkv2_ref[0, 0, 0, q, :], qkv2_ref[0, 1, 0, rows(p["k2"], 2 * BLOCK), :]) \
                + bias_ref[1, 0, p["f2"]]
        else:
            s_refs[2][u] = _nt_dot(qkv3_ref[0, 0, 0, q, :], qkv3_ref[0, 1, 0, q, :]) \
                + bias_ref[2, 0, 1, :, 0:BLOCK]

    def put(dst, lead, idx, acc, m, l):
        dst[0][(*lead, idx, slice(None))] = acc
        dst[1][(*lead, idx, slice(None))] = jnp.broadcast_to(m, (BLOCK, HEAD_DIM))
        dst[2][(*lead, idx, slice(None))] = l

    def finish_block(it, u, g, s_refs):
        p = plan(it, u)
        q = rows(p["q"], BLOCK)
        if g == 0:
            put(res1, (par,), q, *_softmax_pv(s_refs[0][u], qkv1_ref[0, 2, 0, rows(p["k1"], 2 * BLOCK), :]))
        elif g == 1:
            put(res2, (par,), pl.ds(p["out2"], BLOCK, stride=d4),
                *_softmax_pv(s_refs[1][u], qkv2_ref[0, 2, 0, rows(p["k2"], 2 * BLOCK), :]))
        else:
            put(stage3, (), pl.ds(p["out3"], BLOCK, stride=d4),
                *_softmax_pv(s_refs[2][u], qkv3_ref[0, 2, 0, q, :]))

    def merge(base, n_rows):
        for c0 in range(0, n_rows, MERGE_ROWS):
            rws = rows(base + c0, MERGE_ROWS, MERGE_ROWS)
            ms = [r[1][opar, rws, :] for r in (res1, res2, res3)]
            m = jnp.maximum(jnp.maximum(ms[0], ms[1]), ms[2])
            es = [jnp.exp2(mg - m) for mg in ms]
            num = sum(e * r[0][opar, rws, :] for e, r in zip(es, (res1, res2, res3)))
            den = sum(e * r[2][opar, rws, :] for e, r in zip(es, (res1, res2, res3)))
            o_ref[0, 0, rws, :] = (num * (1.0 / den)).astype(BF16) * sg_ref[0, 0, rws, :]

    def section(it_scores, s_dst, it_finish, s_src, merge_base, merge_rows):
        n_p = N_GROUPS * per
        cuts = [MERGE_ROWS * ((merge_rows // MERGE_ROWS) * k // n_p) for k in range(n_p + 1)]
        for u in range(per):
            for g in range(N_GROUPS):
                k = N_GROUPS * u + g
                merge(merge_base + cuts[k], cuts[k + 1] - cuts[k])
                if it_scores is not None:
                    score_block(it_scores, u, g, s_dst)
                if it_finish is not None:
                    finish_block(it_finish, u, g, s_src)

    sec = seq // n_it
    s_a = (sa1_ref, sa2_ref, sa3_ref)
    s_b = (sb1_ref, sb2_ref, sb3_ref)
    section(0, s_a, None, None, 0, sec // 2)

    def body(i, carry):
        base = sec // 2 + 2 * i * sec
        section(2 * i + 1, s_b, 2 * i, s_a, base, sec)
        section(2 * i + 2, s_a, 2 * i + 1, s_b, base + sec, sec)
        return carry
    lax.fori_loop(0, n_it // 2 - 1, body, 0)
    tail = sec // 2 + (n_it - 2) * sec
    section(n_it - 1, s_b, n_it - 2, s_a, tail, sec)
    section(None, None, n_it - 1, s_b, tail + sec, seq - tail - sec)

    for c in range(d4):
        for p0 in range(0, sub4, BLOCK):
            src = slice(c * sub4 + p0, c * sub4 + p0 + BLOCK)
            dst = pl.ds(d4 * p0 + c, BLOCK, stride=d4)
            for k in range(3):
                res3[k][par, dst, :] = stage3[k][src, :]


def _attn_call(qkv1, qkv2, qkv3, bias, sg):
    bsz, _, hg, seq, _ = qkv1.shape
    qkv2 = qkv2.reshape(qkv1.shape)
    qkv3 = qkv3.reshape(qkv1.shape)
    bias = bias.reshape(N_GROUPS, hg, 2, BLOCK, 2 * BLOCK)
    n_slots = bsz * hg

    def cur(s):
        c = jnp.minimum(s, n_slots - 1)
        return c // hg, c % hg

    def prev(s):
        p = jnp.maximum(s - 1, 0)
        return p // hg, p % hg

    def qkv_map(s):
        b, j = cur(s)
        return (b, 0, j, 0, 0)

    def bias_map(s):
        return (0, cur(s)[1], 0, 0, 0)

    def lag_map(s):
        b, j = prev(s)
        return (b, j, 0, 0)

    in_tiles = (3 * 3 + 1) * seq * HEAD_DIM * 2 * 2 + N_GROUPS * 2 * BLOCK * 2 * BLOCK * 4 * 2
    out_tiles = seq * HEAD_DIM * 2 * 2
    per = ATTN_BLOCKS_PER_STAGE
    score_shapes = [(per, BLOCK, 2 * BLOCK), (per, BLOCK, 2 * BLOCK), (per, BLOCK, BLOCK)] * 2
    row_shapes = [(2, seq, HEAD_DIM)] * (3 * N_GROUPS) + [(seq, HEAD_DIM)] * 3
    scratch = sum(math.prod(sh) * 4 for sh in score_shapes + row_shapes)
    qkv_spec = pl.BlockSpec((1, 3, 1, seq, HEAD_DIM), qkv_map)
    return pl.pallas_call(
        functools.partial(_attn_kernel, seq=seq),
        grid=(n_slots + 1,),
        in_specs=[
            qkv_spec, qkv_spec, qkv_spec,
            pl.BlockSpec((N_GROUPS, 1, 2, BLOCK, 2 * BLOCK), bias_map),
            pl.BlockSpec((1, 1, seq, HEAD_DIM), lag_map),
        ],
        out_specs=pl.BlockSpec((1, 1, seq, HEAD_DIM), lag_map),
        out_shape=jax.ShapeDtypeStruct((bsz, hg, seq, HEAD_DIM), BF16),
        scratch_shapes=[pltpu.VMEM(sh, F32) for sh in score_shapes + row_shapes],
        compiler_params=pltpu.CompilerParams(
            dimension_semantics=("arbitrary",),
            vmem_limit_bytes=_vmem_limit(in_tiles + out_tiles + scratch)),
        name="dilated_attn",
    )(qkv1, qkv2, qkv3, bias, sg)


def _outproj_kernel(mod_ref, o_ref, spre_ref, x_ref, wma_ref, wmc_ref, wco_ref, wao_ref, wo_ref,
                    lng_ref, lnb_ref, out_ref, *, alpha, tm):
    shift = mod_ref[0, :, 0:D_MODEL]
    scale1 = 1.0 + mod_ref[0, :, D_MODEL:2 * D_MODEL]
    gate1 = 1.0 + mod_ref[0, :, 2 * D_MODEL:3 * D_MODEL]
    def branches(rws):
        h = (x_ref[0, rws, :] * scale1 + shift).astype(BF16)
        g_attn = _sigmoid(jnp.dot(h, wma_ref[...], preferred_element_type=F32))
        g_conv = _sigmoid(jnp.dot(h, wmc_ref[...], preferred_element_type=F32))
        a_in = jnp.concatenate([o_ref[0, j, rws, :] for j in range(HEADS_PER_GROUP)], axis=1)
        a_out = jnp.dot(a_in, wao_ref[...], preferred_element_type=F32)
        s_out = jnp.dot(spre_ref[0, rws, :], wco_ref[...], preferred_element_type=F32)
        return (g_attn * a_out + g_conv * s_out).astype(BF16)

    def norm(rws, y):
        v = alpha * x_ref[0, rws, :] + gate1 * y
        mu = jnp.mean(v, axis=-1, keepdims=True)
        dv = v - mu
        var = jnp.mean(dv * dv, axis=-1, keepdims=True)
        out_ref[0, rws, :] = dv * lax.rsqrt(var + LN_EPS) * lng_ref[...] + lnb_ref[...]

    chunks = [slice(r0, r0 + OUTPROJ_ROWS) for r0 in range(0, tm, OUTPROJ_ROWS)]
    merged = branches(chunks[0])
    for c, rws in enumerate(chunks):
        y = jnp.dot(merged, wo_ref[...], preferred_element_type=F32)
        if c + 1 < len(chunks):
            merged = branches(chunks[c + 1])
        norm(rws, y)


def _outproj_call(mod, o, spre, x, w_in, w_conv_out, w_attn_out, w_o, ln_g, ln_b, alpha):
    bsz, seq, _ = x.shape
    tm = OUTPROJ_TM
    assert seq % tm == 0 and tm % OUTPROJ_ROWS == 0
    assert COL_MATTN % D_MODEL == 0 and COL_MCONV % D_MODEL == 0
    hg = HEADS_PER_GROUP
    tok = lambda b, i: (b, i, 0)
    const = lambda b, i: (0, 0)
    resident = dict(pipeline_mode=pl.Buffered(1))
    tiles = (ATTN_WIDTH * 2 + CONV_WIDTH * 2 + 2 * D_MODEL * 4) * tm * 2
    weights = (2 * D_MODEL + CONV_WIDTH + ATTN_WIDTH + D_MODEL) * D_MODEL * 2
    temps = 8 * OUTPROJ_ROWS * D_MODEL * 4
    return pl.pallas_call(
        functools.partial(_outproj_kernel, alpha=alpha, tm=tm),
        grid=(bsz, seq // tm),
        in_specs=[
            pl.BlockSpec((1, 1, 3 * D_MODEL), lambda b, i: (b, 0, 0)),
            pl.BlockSpec((1, hg, tm, HEAD_DIM), lambda b, i: (b, 0, i, 0)),
            pl.BlockSpec((1, tm, CONV_WIDTH), tok),
            pl.BlockSpec((1, tm, D_MODEL), tok),
            pl.BlockSpec((D_MODEL, D_MODEL), lambda b, i: (0, COL_MATTN // D_MODEL), **resident),
            pl.BlockSpec((D_MODEL, D_MODEL), lambda b, i: (0, COL_MCONV // D_MODEL), **resident),
            pl.BlockSpec((CONV_WIDTH, D_MODEL), const, **resident),
            pl.BlockSpec((ATTN_WIDTH, D_MODEL), const, **resident),
            pl.BlockSpec((D_MODEL, D_MODEL), const, **resident),
            pl.BlockSpec((1, D_MODEL), const),
            pl.BlockSpec((1, D_MODEL), const),
        ],
        out_specs=pl.BlockSpec((1, tm, D_MODEL), tok),
        out_shape=jax.ShapeDtypeStruct((bsz, seq, D_MODEL), F32),
        compiler_params=pltpu.CompilerParams(
            dimension_semantics=("arbitrary", "arbitrary"),
            vmem_limit_bytes=_vmem_limit(tiles + weights + temps)),
        name="out_proj",
    )(mod, o, spre, x, w_in, w_in, w_conv_out, w_attn_out, w_o, ln_g, ln_b)


def kernel(x, c, w_ada, b_ada, w_in, conv_w, conv_b, rel_bias, w_attn_out, w_conv_out, w_o, ln_g, ln_b):
    bsz = x.shape[0]
    depth = w_in.shape[0]
    alpha = (2.0 * depth) ** 0.25
    bias = _bias_call(rel_bias)
    for layer in range(depth):
        mod = _ada_call(c, w_ada[layer], b_ada[layer]).reshape(bsz, 1, 3 * D_MODEL)
        w_in_b = w_in[layer].astype(BF16)
        qkv1, qkv2, qkv3, sg, spre = _inproj_call(mod, x, w_in_b, conv_w[layer], conv_b[layer][None, :])
        o = _attn_call(qkv1, qkv2, qkv3, bias, sg)
        x = _outproj_call(mod, o, spre, x, w_in_b, w_conv_out[layer].astype(BF16),
                          w_attn_out[layer].astype(BF16), w_o[layer].astype(BF16),
                          ln_g[layer][None, :], ln_b[layer][None, :], alpha)
    return x
```

```python
import functools
import math

import numpy as np
import jax
import jax.numpy as jnp
from jax import lax
from jax.experimental import pallas as pl
from jax.experimental.pallas import tpu as pltpu

F32 = jnp.float32
BF16 = jnp.bfloat16

D_MODEL = 1024
HEAD_DIM = 128
HEADS_PER_GROUP = 4
DILATED_GROUPS = ((128, 1), (512, 4), (2048, 16))
N_GROUPS = len(DILATED_GROUPS)
N_ATTN_HEADS = N_GROUPS * HEADS_PER_GROUP
QKV_WIDTH = N_ATTN_HEADS * HEAD_DIM
ATTN_WIDTH = HEADS_PER_GROUP * HEAD_DIM
CONV_WIDTH = D_MODEL
CONV_K = 3
N_BUCKETS = 32
MAX_EXACT = 16
MAX_DISTANCE = 2048
BLOCK = 128
LN_EPS = 1e-5
NEG_INF = -1e30
LOG2E = math.log2(math.e)
Q_SCALE = HEAD_DIM ** -0.5 * LOG2E

COL_Q = 0
COL_K = QKV_WIDTH
COL_V = 2 * QKV_WIDTH
COL_GATTN = 3 * QKV_WIDTH
COL_U = COL_GATTN + ATTN_WIDTH
COL_B = COL_U + CONV_WIDTH
COL_C = COL_B + CONV_WIDTH
COL_GCONV = COL_C + CONV_WIDTH
COL_MATTN = COL_GCONV + CONV_WIDTH
COL_MCONV = COL_MATTN + D_MODEL
N_COLS = COL_MCONV + D_MODEL

V7X_VMEM_BYTES = 64 * 1024 * 1024
SUBLANES = 8
INPROJ_TM = 512
OUTPROJ_TM = 1024
OUTPROJ_ROWS = 256
CONV_CHUNK = 256
ATTN_BLOCKS_PER_STAGE = 4
MERGE_ROWS = 32


def _vmem_limit(estimate_bytes):
    return int(min(V7X_VMEM_BYTES - (4 << 20), estimate_bytes * 5 // 4 + (4 << 20)))


def _bucket_table():
    a = np.arange(BLOCK)[:, None]
    b = np.arange(2 * BLOCK)[None, :]
    steps = a + BLOCK - b
    out = []
    for window, dilation in DILATED_GROUPS:
        n_steps = window // dilation
        valid = (steps >= 0) & (steps <= n_steps)
        dist = np.maximum(steps, 0) * dilation
        n = np.maximum(dist, 1).astype(np.float32)
        large = MAX_EXACT + (np.log(n / np.float32(MAX_EXACT)) / np.float32(math.log(MAX_DISTANCE / MAX_EXACT))
                             * np.float32(N_BUCKETS - MAX_EXACT)).astype(np.int32)
        large = np.minimum(large, N_BUCKETS - 1)
        bucket = np.where(dist < MAX_EXACT, dist, large)
        out.append(np.where(valid, bucket, -1).astype(np.int32))
    return np.stack(out)


_BUCKETS = _bucket_table()


def _sigmoid(v):
    return 0.5 * jnp.tanh(0.5 * v) + 0.5


def _ada_kernel(c_ref, w_ref, b_ref, o_ref):
    c = c_ref[...]
    o_ref[...] = jnp.dot(c * _sigmoid(c), w_ref[...], preferred_element_type=F32) + b_ref[...]


def _ada_call(c, w_ada, b_ada):
    bsz = c.shape[0]
    return pl.pallas_call(
        _ada_kernel,
        out_shape=jax.ShapeDtypeStruct((bsz, 3 * D_MODEL), F32),
        name="ada_mod",
    )(c, w_ada, b_ada[None, :])


def _bias_kernel(rb_ref, bucket_ref, o_ref):
    for g in range(N_GROUPS):
        bk = bucket_ref[g]
        for j in range(HEADS_PER_GROUP):
            h = g * HEADS_PER_GROUP + j
            acc = jnp.full(bk.shape, NEG_INF, F32)
            for k in range(N_BUCKETS):
                acc = jnp.where(bk == k, rb_ref[k, h] * LOG2E, acc)
            o_ref[h, 0] = acc
            o_ref[h, 1] = jnp.concatenate(
                [acc[:, BLOCK:], jnp.full((BLOCK, BLOCK), NEG_INF, F32)], axis=1)


def _bias_call(rel_bias):
    return pl.pallas_call(
        _bias_kernel,
        out_shape=jax.ShapeDtypeStruct((N_ATTN_HEADS, 2, BLOCK, 2 * BLOCK), F32),
        in_specs=[pl.BlockSpec(memory_space=pltpu.SMEM), pl.BlockSpec(memory_space=pltpu.VMEM)],
        out_specs=pl.BlockSpec(memory_space=pltpu.VMEM),
        name="bias_tables",
    )(rel_bias, jnp.asarray(_BUCKETS))


def _inproj_kernel(mod_ref, x_ref, w_ref, cw_ref, cb_ref,
                   qkv1_ref, qkv2_ref, qkv3_ref, sg_ref, spre_ref,
                   hs_ref, hs4_ref, hb_ref, zbuf_ref, *, tm):
    i = pl.program_id(1)

    @pl.when(i == 0)
    def _():
        zbuf_ref[0:SUBLANES, :] = jnp.zeros((SUBLANES, CONV_WIDTH), F32)

    n_slab = D_MODEL // HEAD_DIM
    for s in range(n_slab):
        ls = slice(s * HEAD_DIM, (s + 1) * HEAD_DIM)
        h = x_ref[0, :, ls] * (1.0 + mod_ref[0, :, D_MODEL + s * HEAD_DIM:D_MODEL + (s + 1) * HEAD_DIM]) \
            + mod_ref[0, :, ls]
        hb_ref[0, :, ls] = h.astype(BF16)
        hs_ref[s] = h

    d4, d16 = DILATED_GROUPS[1][1], DILATED_GROUPS[2][1]
    assert d16 == d4 * d4
    n4, n16 = tm // d4, tm // d16
    for s in range(n_slab):
        ls = slice(s * HEAD_DIM, (s + 1) * HEAD_DIM)
        for r in range(d4):
            part = hs_ref[s, pl.ds(r, n4, stride=d4), :]
            hs4_ref[s, r * n4:(r + 1) * n4, :] = part
            hb_ref[1, r * n4:(r + 1) * n4, ls] = part.astype(BF16)
    for s in range(n_slab):
        ls = slice(s * HEAD_DIM, (s + 1) * HEAD_DIM)
        for r16 in range(d16):
            part = hs4_ref[s, pl.ds((r16 % d4) * n4 + r16 // d4, n16, stride=d4), :]
            hb_ref[2, r16 * n16:(r16 + 1) * n16, ls] = part.astype(BF16)

    def proj(g, c0, width):
        return jnp.dot(hb_ref[g], w_ref[:, c0:c0 + width], preferred_element_type=F32)

    def qkv(g, store):
        for t, col in enumerate((COL_Q, COL_K, COL_V)):
            r = proj(g, col + g * ATTN_WIDTH, ATTN_WIDTH)
            if t == 0:
                r = r * Q_SCALE
            r = r.astype(BF16)
            for j in range(HEADS_PER_GROUP):
                store(t, j, r[:, j * HEAD_DIM:(j + 1) * HEAD_DIM])

    def store1(t, j, v):
        qkv1_ref[0, t, j] = v
    qkv(0, store1)

    ga = proj(0, COL_GATTN, ATTN_WIDTH)
    sgv = (ga * _sigmoid(ga)).astype(BF16)
    for j in range(HEADS_PER_GROUP):
        sg_ref[0, j] = sgv[:, j * HEAD_DIM:(j + 1) * HEAD_DIM]

    for c0 in range(0, CONV_WIDTH, CONV_CHUNK):
        cs = slice(c0, c0 + CONV_CHUNK)
        u = proj(0, COL_U + c0, CONV_CHUNK)
        cg = proj(0, COL_C + c0, CONV_CHUNK)
        z = cg * u
        zbuf_ref[SUBLANES:SUBLANES + tm, cs] = z
        zm1 = zbuf_ref[SUBLANES - 1:SUBLANES - 1 + tm, cs]
        zm2 = zbuf_ref[SUBLANES - 2:SUBLANES - 2 + tm, cs]
        y = cw_ref[0:1, cs] * zm2 + cw_ref[1:2, cs] * zm1 + cw_ref[2:3, cs] * z + cb_ref[:, cs]
        bg = proj(0, COL_B + c0, CONV_CHUNK)
        gc = proj(0, COL_GCONV + c0, CONV_CHUNK)
        spre_ref[0, :, cs] = (bg * y * (gc * _sigmoid(gc))).astype(BF16)
    zbuf_ref[0:SUBLANES, :] = zbuf_ref[tm:tm + SUBLANES, :]

    def store_sub(ref, d):
        def store(t, j, v):
            rows = tm // d
            for r in range(d):
                ref[0, t, j, r] = v[r * rows:(r + 1) * rows]
        return store
    qkv(1, store_sub(qkv2_ref, d4))
    qkv(2, store_sub(qkv3_ref, d16))


def _inproj_call(mod, x, w_in, conv_w, conv_b):
    bsz, seq, _ = x.shape
    tm = INPROJ_TM
    n_t = seq // tm
    d4, d16 = DILATED_GROUPS[1][1], DILATED_GROUPS[2][1]
    sub4, sub16 = seq // d4, seq // d16
    assert seq % tm == 0 and tm % (d16 * 16) == 0
    hg = HEADS_PER_GROUP

    in_specs = [
        pl.BlockSpec((1, 1, 3 * D_MODEL), lambda b, i: (b, 0, 0)),
        pl.BlockSpec((1, tm, D_MODEL), lambda b, i: (b, i, 0)),
        pl.BlockSpec((D_MODEL, COL_MATTN), lambda b, i: (0, 0), pipeline_mode=pl.Buffered(1)),
        pl.BlockSpec((CONV_K, CONV_WIDTH), lambda b, i: (0, 0)),
        pl.BlockSpec((1, CONV_WIDTH), lambda b, i: (0, 0)),
    ]
    out_shape = [
        jax.ShapeDtypeStruct((bsz, 3, hg, seq, HEAD_DIM), BF16),
        jax.ShapeDtypeStruct((bsz, 3, hg, d4, sub4, HEAD_DIM), BF16),
        jax.ShapeDtypeStruct((bsz, 3, hg, d16, sub16, HEAD_DIM), BF16),
        jax.ShapeDtypeStruct((bsz, hg, seq, HEAD_DIM), BF16),
        jax.ShapeDtypeStruct((bsz, seq, CONV_WIDTH), BF16),
    ]
    out_specs = [
        pl.BlockSpec((1, 3, hg, tm, HEAD_DIM), lambda b, i: (b, 0, 0, i, 0)),
        pl.BlockSpec((1, 3, hg, d4, tm // d4, HEAD_DIM), lambda b, i: (b, 0, 0, 0, i, 0)),
        pl.BlockSpec((1, 3, hg, d16, tm // d16, HEAD_DIM), lambda b, i: (b, 0, 0, 0, i, 0)),
        pl.BlockSpec((1, hg, tm, HEAD_DIM), lambda b, i: (b, 0, i, 0)),
        pl.BlockSpec((1, tm, CONV_WIDTH), lambda b, i: (b, i, 0)),
    ]
    weights = D_MODEL * COL_MATTN * 2
    x_tiles = tm * D_MODEL * 4 * 2
    out_tiles = (3 * 3 * ATTN_WIDTH + ATTN_WIDTH + CONV_WIDTH) * tm * 2 * 2
    scratch = 2 * tm * D_MODEL * 4 + N_GROUPS * tm * D_MODEL * 2 + (tm + SUBLANES) * CONV_WIDTH * 4
    return pl.pallas_call(
        functools.partial(_inproj_kernel, tm=tm),
        grid=(bsz, n_t),
        in_specs=in_specs,
        out_specs=out_specs,
        out_shape=out_shape,
        scratch_shapes=[pltpu.VMEM((D_MODEL // HEAD_DIM, tm, HEAD_DIM), F32),
                        pltpu.VMEM((D_MODEL // HEAD_DIM, tm, HEAD_DIM), F32),
                        pltpu.VMEM((N_GROUPS, tm, D_MODEL), BF16),
                        pltpu.VMEM((tm + SUBLANES, CONV_WIDTH), F32)],
        compiler_params=pltpu.CompilerParams(
            dimension_semantics=("arbitrary", "arbitrary"),
            vmem_limit_bytes=_vmem_limit(weights + x_tiles + out_tiles + scratch)),
        name="in_proj",
    )(mod, x, w_in, conv_w, conv_b)


def _nt_dot(q, k):
    return lax.dot_general(q, k, (((1,), (1,)), ((), ())), preferred_element_type=F32)


def _softmax_pv(s, v):
    m = jnp.max(s, axis=1, keepdims=True)
    p = jnp.exp2(s - m)
    v_ext = jnp.concatenate([v, jnp.ones((v.shape[0], HEAD_DIM), BF16)], axis=1)
    acc = jnp.dot(p.astype(BF16), v_ext, preferred_element_type=F32)
    return acc[:, :HEAD_DIM], m, acc[:, HEAD_DIM:]


def _attn_kernel(qkv1_ref, qkv2_ref, qkv3_ref, bias_ref, sg_ref, o_ref,
                 sa1_ref, sa2_ref, sa3_ref, sb1_ref, sb2_ref, sb3_ref, *row_refs, seq):
    res1, res2, res3, stage3 = (row_refs[3 * g:3 * g + 3] for g in range(N_GROUPS + 1))
    step = pl.program_id(0)
    par = jnp.bitwise_and(step, 1)
    opar = 1 - par

    @pl.when(step == 0)
    def _():
        for r in res1 + res2 + res3:
            r[1] = jnp.ones((seq, HEAD_DIM), F32)

    d4, d16 = DILATED_GROUPS[1][1], DILATED_GROUPS[2][1]
    per = ATTN_BLOCKS_PER_STAGE
    n_it = seq // BLOCK // per
    blk4 = seq // d4 // BLOCK
    assert seq // d16 == BLOCK and blk4 & (blk4 - 1) == 0 and n_it % 2 == 0 and n_it * per * BLOCK == seq
    assert d16 == d4 * d4 and d4 == blk4
    shift4 = blk4.bit_length() - 1
    sub4 = seq // d4

    def rows(start, size, align=BLOCK):
        if isinstance(start, int):
            return pl.ds(start, size)
        return pl.ds(pl.multiple_of(start, align), size)

    def plan(it, u):
        it = it * per + u
        if isinstance(it, int):
            f1, nb, r4 = int(it == 0), it & (blk4 - 1), it >> shift4
            f2 = int(nb == 0)
        else:
            f1 = (it == 0).astype(jnp.int32)
            nb = jnp.bitwise_and(it, blk4 - 1)
            f2 = (nb == 0).astype(jnp.int32)
            r4 = lax.shift_right_logical(it, shift4)
        return dict(q=it * BLOCK, k1=(it - 1 + f1) * BLOCK, f1=f1, k2=(it - 1 + f2) * BLOCK, f2=f2,
                    out2=nb * (BLOCK * d4) + r4, out3=nb * sub4 + r4)

    def score_block(it, u, g, s_refs):
        p = plan(it, u)
        q = rows(p["q"], BLOCK)
        if g == 0:
            s_refs[0][u] = _nt_dot(qkv1_ref[0, 0, 0, q, :], qkv1_ref[0, 1, 0, rows(p["k1"], 2 * BLOCK), :]) \
                + bias_ref[0, 0, p["f1"]]
        elif g == 1:
            s_refs[1][u] = _nt_dot(qkv2_ref[0, 0, 0, q, :], qkv2_ref[0, 1, 0, rows(p["k2"], 2 * BLOCK), :]) \
                + bias_ref[1, 0, p["f2"]]
        else:
            s_refs[2][u] = _nt_dot(qkv3_ref[0, 0, 0, q, :], qkv3_ref[0, 1, 0, q, :]) \
                + bias_ref[2, 0, 1, :, 0:BLOCK]

    def put(dst, lead, idx, acc, m, l):
        dst[0][(*lead, idx, slice(None))] = acc
        dst[1][(*lead, idx, slice(None))] = jnp.broadcast_to(m, (BLOCK, HEAD_DIM))
        dst[2][(*lead, idx, slice(None))] = l

    def finish_block(it, u, g, s_refs):
        p = plan(it, u)
        q = rows(p["q"], BLOCK)
        if g == 0:
            put(res1, (par,), q, *_softmax_pv(s_refs[0][u], qkv1_ref[0, 2, 0, rows(p["k1"], 2 * BLOCK), :]))
        elif g == 1:
            put(res2, (par,), pl.ds(p["out2"], BLOCK, stride=d4),
                *_softmax_pv(s_refs[1][u], qkv2_ref[0, 2, 0, rows(p["k2"], 2 * BLOCK), :]))
        else:
            put(stage3, (), pl.ds(p["out3"], BLOCK, stride=d4),
                *_softmax_pv(s_refs[2][u], qkv3_ref[0, 2, 0, q, :]))

    def merge(base, n_rows):
        for c0 in range(0, n_rows, MERGE_ROWS):
            rws = rows(base + c0, MERGE_ROWS, MERGE_ROWS)
            ms = [r[1][opar, rws, :] for r in (res1, res2, res3)]
            m = jnp.maximum(jnp.maximum(ms[0], ms[1]), ms[2])
            es = [jnp.exp2(mg - m) for mg in ms]
            num = sum(e * r[0][opar, rws, :] for e, r in zip(es, (res1, res2, res3)))
            den = sum(e * r[2][opar, rws, :] for e, r in zip(es, (res1, res2, res3)))
            o_ref[0, 0, rws, :] = (num * (1.0 / den)).astype(BF16) * sg_ref[0, 0, rws, :]

    def section(it_scores, s_dst, it_finish, s_src, merge_base, merge_rows):
        n_p = N_GROUPS * per
        cuts = [MERGE_ROWS * ((merge_rows // MERGE_ROWS) * k // n_p) for k in range(n_p + 1)]
        for u in range(per):
            for g in range(N_GROUPS):
                k = N_GROUPS * u + g
                merge(merge_base + cuts[k], cuts[k + 1] - cuts[k])
                if it_scores is not None:
                    score_block(it_scores, u, g, s_dst)
                if it_finish is not None:
                    finish_block(it_finish, u, g, s_src)

    sec = seq // n_it
    s_a = (sa1_ref, sa2_ref, sa3_ref)
    s_b = (sb1_ref, sb2_ref, sb3_ref)
    section(0, s_a, None, None, 0, sec // 2)

    def body(i, carry):
        base = sec // 2 + 2 * i * sec
        section(2 * i + 1, s_b, 2 * i, s_a, base, sec)
        section(2 * i + 2, s_a, 2 * i + 1, s_b, base + sec, sec)
        return carry
    lax.fori_loop(0, n_it // 2 - 1, body, 0)
    tail = sec // 2 + (n_it - 2) * sec
    section(n_it - 1, s_b, n_it - 2, s_a, tail, sec)
    section(None, None, n_it - 1, s_b, tail + sec, seq - tail - sec)

    for c in range(d4):
        for p0 in range(0, sub4, BLOCK):
            src = slice(c * sub4 + p0, c * sub4 + p0 + BLOCK)
            dst = pl.ds(d4 * p0 + c, BLOCK, stride=d4)
            for k in range(3):
                res3[k][par, dst, :] = stage3[k][src, :]


def _attn_call(qkv1, qkv2, qkv3, bias, sg):
    bsz, _, hg, seq, _ = qkv1.shape
    qkv2 = qkv2.reshape(qkv1.shape)
    qkv3 = qkv3.reshape(qkv1.shape)
    bias = bias.reshape(N_GROUPS, hg, 2, BLOCK, 2 * BLOCK)
    n_slots = bsz * hg

    def cur(s):
        c = jnp.minimum(s, n_slots - 1)
        return c // hg, c % hg

    def prev(s):
        p = jnp.maximum(s - 1, 0)
        return p // hg, p % hg

    def qkv_map(s):
        b, j = cur(s)
        return (b, 0, j, 0, 0)

    def bias_map(s):
        return (0, cur(s)[1], 0, 0, 0)

    def lag_map(s):
        b, j = prev(s)
        return (b, j, 0, 0)

    in_tiles = (3 * 3 + 1) * seq * HEAD_DIM * 2 * 2 + N_GROUPS * 2 * BLOCK * 2 * BLOCK * 4 * 2
    out_tiles = seq * HEAD_DIM * 2 * 2
    per = ATTN_BLOCKS_PER_STAGE
    score_shapes = [(per, BLOCK, 2 * BLOCK), (per, BLOCK, 2 * BLOCK), (per, BLOCK, BLOCK)] * 2
    row_shapes = [(2, seq, HEAD_DIM)] * (3 * N_GROUPS) + [(seq, HEAD_DIM)] * 3
    scratch = sum(math.prod(sh) * 4 for sh in score_shapes + row_shapes)
    qkv_spec = pl.BlockSpec((1, 3, 1, seq, HEAD_DIM), qkv_map)
    return pl.pallas_call(
        functools.partial(_attn_kernel, seq=seq),
        grid=(n_slots + 1,),
        in_specs=[
            qkv_spec, qkv_spec, qkv_spec,
            pl.BlockSpec((N_GROUPS, 1, 2, BLOCK, 2 * BLOCK), bias_map),
            pl.BlockSpec((1, 1, seq, HEAD_DIM), lag_map),
        ],
        out_specs=pl.BlockSpec((1, 1, seq, HEAD_DIM), lag_map),
        out_shape=jax.ShapeDtypeStruct((bsz, hg, seq, HEAD_DIM), BF16),
        scratch_shapes=[pltpu.VMEM(sh, F32) for sh in score_shapes + row_shapes],
        compiler_params=pltpu.CompilerParams(
            dimension_semantics=("arbitrary",),
            vmem_limit_bytes=_vmem_limit(in_tiles + out_tiles + scratch)),
        name="dilated_attn",
    )(qkv1, qkv2, qkv3, bias, sg)


def _outproj_kernel(mod_ref, o_ref, spre_ref, x_ref, wma_ref, wmc_ref, wco_ref, wao_ref, wo_ref,
                    lng_ref, lnb_ref, out_ref, *, alpha, tm):
    shift = mod_ref[0, :, 0:D_MODEL]
    scale1 = 1.0 + mod_ref[0, :, D_MODEL:2 * D_MODEL]
    gate1 = 1.0 + mod_ref[0, :, 2 * D_MODEL:3 * D_MODEL]
    def branches(rws):
        h = (x_ref[0, rws, :] * scale1 + shift).astype(BF16)
        g_attn = _sigmoid(jnp.dot(h, wma_ref[...], preferred_element_type=F32))
        g_conv = _sigmoid(jnp.dot(h, wmc_ref[...], preferred_element_type=F32))
        a_in = jnp.concatenate([o_ref[0, j, rws, :] for j in range(HEADS_PER_GROUP)], axis=1)
        a_out = jnp.dot(a_in, wao_ref[...], preferred_element_type=F32)
        s_out = jnp.dot(spre_ref[0, rws, :], wco_ref[...], preferred_element_type=F32)
        return (g_attn * a_out + g_conv * s_out).astype(BF16)

    def norm(rws, y):
        v = alpha * x_ref[0, rws, :] + gate1 * y
        mu = jnp.mean(v, axis=-1, keepdims=True)
        dv = v - mu
        var = jnp.mean(dv * dv, axis=-1, keepdims=True)
        out_ref[0, rws, :] = dv * lax.rsqrt(var + LN_EPS) * lng_ref[...] + lnb_ref[...]

    chunks = [slice(r0, r0 + OUTPROJ_ROWS) for r0 in range(0, tm, OUTPROJ_ROWS)]
    merged = branches(chunks[0])
    for c, rws in enumerate(chunks):
        y = jnp.dot(merged, wo_ref[...], preferred_element_type=F32)
        if c + 1 < len(chunks):
            merged = branches(chunks[c + 1])
        norm(rws, y)


def _outproj_call(mod, o, spre, x, w_in, w_conv_out, w_attn_out, w_o, ln_g, ln_b, alpha):
    bsz, seq, _ = x.shape
    tm = OUTPROJ_TM
    assert seq % tm == 0 and tm % OUTPROJ_ROWS == 0
    assert COL_MATTN % D_MODEL == 0 and COL_MCONV % D_MODEL == 0
    hg = HEADS_PER_GROUP
    tok = lambda b, i: (b, i, 0)
    const = lambda b, i: (0, 0)
    resident = dict(pipeline_mode=pl.Buffered(1))
    tiles = (ATTN_WIDTH * 2 + CONV_WIDTH * 2 + 2 * D_MODEL * 4) * tm * 2
    weights = (2 * D_MODEL + CONV_WIDTH + ATTN_WIDTH + D_MODEL) * D_MODEL * 2
    temps = 8 * OUTPROJ_ROWS * D_MODEL * 4
    return pl.pallas_call(
        functools.partial(_outproj_kernel, alpha=alpha, tm=tm),
        grid=(bsz, seq // tm),
        in_specs=[
            pl.BlockSpec((1, 1, 3 * D_MODEL), lambda b, i: (b, 0, 0)),
            pl.BlockSpec((1, hg, tm, HEAD_DIM), lambda b, i: (b, 0, i, 0)),
            pl.BlockSpec((1, tm, CONV_WIDTH), tok),
            pl.BlockSpec((1, tm, D_MODEL), tok),
            pl.BlockSpec((D_MODEL, D_MODEL), lambda b, i: (0, COL_MATTN // D_MODEL), **resident),
            pl.BlockSpec((D_MODEL, D_MODEL), lambda b, i: (0, COL_MCONV // D_MODEL), **resident),
            pl.BlockSpec((CONV_WIDTH, D_MODEL), const, **resident),
            pl.BlockSpec((ATTN_WIDTH, D_MODEL), const, **resident),
            pl.BlockSpec((D_MODEL, D_MODEL), const, **resident),
            pl.BlockSpec((1, D_MODEL), const),
            pl.BlockSpec((1, D_MODEL), const),
        ],
        out_specs=pl.BlockSpec((1, tm, D_MODEL), tok),
        out_shape=jax.ShapeDtypeStruct((bsz, seq, D_MODEL), F32),
        compiler_params=pltpu.CompilerParams(
            dimension_semantics=("arbitrary", "arbitrary"),
            vmem_limit_bytes=_vmem_limit(tiles + weights + temps)),
        name="out_proj",
    )(mod, o, spre, x, w_in, w_in, w_conv_out, w_attn_out, w_o, ln_g, ln_b)


def kernel(x, c, w_ada, b_ada, w_in, conv_w, conv_b, rel_bias, w_attn_out, w_conv_out, w_o, ln_g, ln_b):
    bsz = x.shape[0]
    depth = w_in.shape[0]
    alpha = (2.0 * depth) ** 0.25
    bias = _bias_call(rel_bias)
    for layer in range(depth):
        mod = _ada_call(c, w_ada[layer], b_ada[layer]).reshape(bsz, 1, 3 * D_MODEL)
        w_in_b = w_in[layer].astype(BF16)
        qkv1, qkv2, qkv3, sg, spre = _inproj_call(mod, x, w_in_b, conv_w[layer], conv_b[layer][None, :])
        o = _attn_call(qkv1, qkv2, qkv3, bias, sg)
        x = _outproj_call(mod, o, spre, x, w_in_b, w_conv_out[layer].astype(BF16),
                          w_attn_out[layer].astype(BF16), w_o[layer].astype(BF16),
                          ln_g[layer][None, :], ln_b[layer][None, :], alpha)
    return x
```

```python
import functools
import math

import numpy as np
import jax
import jax.numpy as jnp
from jax import lax
from jax.experimental import pallas as pl
from jax.experimental.pallas import tpu as pltpu

F32 = jnp.float32
BF16 = jnp.bfloat16

D_MODEL = 1024
HEAD_DIM = 128
HEADS_PER_GROUP = 4
DILATED_GROUPS = ((128, 1), (512, 4), (2048, 16))
N_GROUPS = len(DILATED_GROUPS)
N_ATTN_HEADS = N_GROUPS * HEADS_PER_GROUP
QKV_WIDTH = N_ATTN_HEADS * HEAD_DIM
ATTN_WIDTH = HEADS_PER_GROUP * HEAD_DIM
CONV_WIDTH = D_MODEL
CONV_K = 3
N_BUCKETS = 32
MAX_EXACT = 16
MAX_DISTANCE = 2048
BLOCK = 128
LN_EPS = 1e-5
NEG_INF = -1e30
LOG2E = math.log2(math.e)
Q_SCALE = HEAD_DIM ** -0.5 * LOG2E

COL_Q = 0
COL_K = QKV_WIDTH
COL_V = 2 * QKV_WIDTH
COL_GATTN = 3 * QKV_WIDTH
COL_U = COL_GATTN + ATTN_WIDTH
COL_B = COL_U + CONV_WIDTH
COL_C = COL_B + CONV_WIDTH
COL_GCONV = COL_C + CONV_WIDTH
COL_MATTN = COL_GCONV + CONV_WIDTH
COL_MCONV = COL_MATTN + D_MODEL
N_COLS = COL_MCONV + D_MODEL

V7X_VMEM_BYTES = 64 * 1024 * 1024
SUBLANES = 8
INPROJ_TM = 512
OUTPROJ_TM = 1024
OUTPROJ_ROWS = 256
CONV_CHUNK = 256
ATTN_BLOCKS_PER_STAGE = 2
MERGE_ROWS = 32


def _vmem_limit(estimate_bytes):
    return int(min(V7X_VMEM_BYTES - (4 << 20), estimate_bytes * 5 // 4 + (4 << 20)))


def _bucket_table():
    a = np.arange(BLOCK)[:, None]
    b = np.arange(2 * BLOCK)[None, :]
    steps = a + BLOCK - b
    out = []
    for window, dilation in DILATED_GROUPS:
        n_steps = window // dilation
        valid = (steps >= 0) & (steps <= n_steps)
        dist = np.maximum(steps, 0) * dilation
        n = np.maximum(dist, 1).astype(np.float32)
        large = MAX_EXACT + (np.log(n / np.float32(MAX_EXACT)) / np.float32(math.log(MAX_DISTANCE / MAX_EXACT))
                             * np.float32(N_BUCKETS - MAX_EXACT)).astype(np.int32)
        large = np.minimum(large, N_BUCKETS - 1)
        bucket = np.where(dist < MAX_EXACT, dist, large)
        out.append(np.where(valid, bucket, -1).astype(np.int32))
    return np.stack(out)


_BUCKETS = _bucket_table()


def _sigmoid(v):
    return 0.5 * jnp.tanh(0.5 * v) + 0.5


def _ada_kernel(c_ref, w_ref, b_ref, o_ref):
    c = c_ref[...]
    o_ref[...] = jnp.dot(c * _sigmoid(c), w_ref[...], preferred_element_type=F32) + b_ref[...]


def _ada_call(c, w_ada, b_ada):
    bsz = c.shape[0]
    return pl.pallas_call(
        _ada_kernel,
        out_shape=jax.ShapeDtypeStruct((bsz, 3 * D_MODEL), F32),
        name="ada_mod",
    )(c, w_ada, b_ada[None, :])


def _bias_kernel(rb_ref, bucket_ref, o_ref):
    for g in range(N_GROUPS):
        bk = bucket_ref[g]
        for j in range(HEADS_PER_GROUP):
            h = g * HEADS_PER_GROUP + j
            acc = jnp.full(bk.shape, NEG_INF, F32)
            for k in range(N_BUCKETS):
                acc = jnp.where(bk == k, rb_ref[k, h] * LOG2E, acc)
            o_ref[h, 0] = acc
            o_ref[h, 1] = jnp.concatenate(
                [acc[:, BLOCK:], jnp.full((BLOCK, BLOCK), NEG_INF, F32)], axis=1)


def _bias_call(rel_bias):
    return pl.pallas_call(
        _bias_kernel,
        out_shape=jax.ShapeDtypeStruct((N_ATTN_HEADS, 2, BLOCK, 2 * BLOCK), F32),
        in_specs=[pl.BlockSpec(memory_space=pltpu.SMEM), pl.BlockSpec(memory_space=pltpu.VMEM)],
        out_specs=pl.BlockSpec(memory_space=pltpu.VMEM),
        name="bias_tables",
    )(rel_bias, jnp.asarray(_BUCKETS))


def _inproj_kernel(mod_ref, x_ref, w_ref, cw_ref, cb_ref,
                   qkv1_ref, qkv2_ref, qkv3_ref, sg_ref, spre_ref,
                   hs_ref, hs4_ref, hb_ref, zbuf_ref, *, tm):
    i = pl.program_id(1)

    @pl.when(i == 0)
    def _():
        zbuf_ref[0:SUBLANES, :] = jnp.zeros((SUBLANES, CONV_WIDTH), F32)

    n_slab = D_MODEL // HEAD_DIM
    for s in range(n_slab):
        ls = slice(s * HEAD_DIM, (s + 1) * HEAD_DIM)
        h = x_ref[0, :, ls] * (1.0 + mod_ref[0, :, D_MODEL + s * HEAD_DIM:D_MODEL + (s + 1) * HEAD_DIM]) \
            + mod_ref[0, :, ls]
        hb_ref[0, :, ls] = h.astype(BF16)
        hs_ref[s] = h

    d4, d16 = DILATED_GROUPS[1][1], DILATED_GROUPS[2][1]
    assert d16 == d4 * d4
    n4, n16 = tm // d4, tm // d16
    for s in range(n_slab):
        ls = slice(s * HEAD_DIM, (s + 1) * HEAD_DIM)
        for r in range(d4):
            part = hs_ref[s, pl.ds(r, n4, stride=d4), :]
            hs4_ref[s, r * n4:(r + 1) * n4, :] = part
            hb_ref[1, r * n4:(r + 1) * n4, ls] = part.astype(BF16)
    for s in range(n_slab):
        ls = slice(s * HEAD_DIM, (s + 1) * HEAD_DIM)
        for r16 in range(d16):
            part = hs4_ref[s, pl.ds((r16 % d4) * n4 + r16 // d4, n16, stride=d4), :]
            hb_ref[2, r16 * n16:(r16 + 1) * n16, ls] = part.astype(BF16)

    def proj(g, c0, width):
        return jnp.dot(hb_ref[g], w_ref[:, c0:c0 + width], preferred_element_type=F32)

    def qkv(g, store):
        for t, col in enumerate((COL_Q, COL_K, COL_V)):
            r = proj(g, col + g * ATTN_WIDTH, ATTN_WIDTH)
            if t == 0:
                r = r * Q_SCALE
            r = r.astype(BF16)
            for j in range(HEADS_PER_GROUP):
                store(t, j, r[:, j * HEAD_DIM:(j + 1) * HEAD_DIM])

    def store1(t, j, v):
        qkv1_ref[0, t, j] = v
    qkv(0, store1)

    ga = proj(0, COL_GATTN, ATTN_WIDTH)
    sgv = (ga * _sigmoid(ga)).astype(BF16)
    for j in range(HEADS_PER_GROUP):
        sg_ref[0, j] = sgv[:, j * HEAD_DIM:(j + 1) * HEAD_DIM]

    for c0 in range(0, CONV_WIDTH, CONV_CHUNK):
        cs = slice(c0, c0 + CONV_CHUNK)
        u = proj(0, COL_U + c0, CONV_CHUNK)
        cg = proj(0, COL_C + c0, CONV_CHUNK)
        z = cg * u
        zbuf_ref[SUBLANES:SUBLANES + tm, cs] = z
        zm1 = zbuf_ref[SUBLANES - 1:SUBLANES - 1 + tm, cs]
        zm2 = zbuf_ref[SUBLANES - 2:SUBLANES - 2 + tm, cs]
        y = cw_ref[0:1, cs] * zm2 + cw_ref[1:2, cs] * zm1 + cw_ref[2:3, cs] * z + cb_ref[:, cs]
        bg = proj(0, COL_B + c0, CONV_CHUNK)
        gc = proj(0, COL_GCONV + c0, CONV_CHUNK)
        spre_ref[0, :, cs] = (bg * y * (gc * _sigmoid(gc))).astype(BF16)
    zbuf_ref[0:SUBLANES, :] = zbuf_ref[tm:tm + SUBLANES, :]

    def store_sub(ref, d):
        def store(t, j, v):
            rows = tm // d
            for r in range(d):
                ref[0, t, j, r] = v[r * rows:(r + 1) * rows]
        return store
    qkv(1, store_sub(qkv2_ref, d4))
    qkv(2, store_sub(qkv3_ref, d16))


def _inproj_call(mod, x, w_in, conv_w, conv_b):
    bsz, seq, _ = x.shape
    tm = INPROJ_TM
    n_t = seq // tm
    d4, d16 = DILATED_GROUPS[1][1], DILATED_GROUPS[2][1]
    sub4, sub16 = seq // d4, seq // d16
    assert seq % tm == 0 and tm % (d16 * 16) == 0
    hg = HEADS_PER_GROUP

    in_specs = [
        pl.BlockSpec((1, 1, 3 * D_MODEL), lambda b, i: (b, 0, 0)),
        pl.BlockSpec((1, tm, D_MODEL), lambda b, i: (b, i, 0)),
        pl.BlockSpec((D_MODEL, COL_MATTN), lambda b, i: (0, 0), pipeline_mode=pl.Buffered(1)),
        pl.BlockSpec((CONV_K, CONV_WIDTH), lambda b, i: (0, 0)),
        pl.BlockSpec((1, CONV_WIDTH), lambda b, i: (0, 0)),
    ]
    out_shape = [
        jax.ShapeDtypeStruct((bsz, 3, hg, seq, HEAD_DIM), BF16),
        jax.ShapeDtypeStruct((bsz, 3, hg, d4, sub4, HEAD_DIM), BF16),
        jax.ShapeDtypeStruct((bsz, 3, hg, d16, sub16, HEAD_DIM), BF16),
        jax.ShapeDtypeStruct((bsz, hg, seq, HEAD_DIM), BF16),
        jax.ShapeDtypeStruct((bsz, seq, CONV_WIDTH), BF16),
    ]
    out_specs = [
        pl.BlockSpec((1, 3, hg, tm, HEAD_DIM), lambda b, i: (b, 0, 0, i, 0)),
        pl.BlockSpec((1, 3, hg, d4, tm // d4, HEAD_DIM), lambda b, i: (b, 0, 0, 0, i, 0)),
        pl.BlockSpec((1, 3, hg, d16, tm // d16, HEAD_DIM), lambda b, i: (b, 0, 0, 0, i, 0)),
        pl.BlockSpec((1, hg, tm, HEAD_DIM), lambda b, i: (b, 0, i, 0)),
        pl.BlockSpec((1, tm, CONV_WIDTH), lambda b, i: (b, i, 0)),
    ]
    weights = D_MODEL * COL_MATTN * 2
    x_tiles = tm * D_MODEL * 4 * 2
    out_tiles = (3 * 3 * ATTN_WIDTH + ATTN_WIDTH + CONV_WIDTH) * tm * 2 * 2
    scratch = 2 * tm * D_MODEL * 4 + N_GROUPS * tm * D_MODEL * 2 + (tm + SUBLANES) * CONV_WIDTH * 4
    return pl.pallas_call(
        functools.partial(_inproj_kernel, tm=tm),
        grid=(bsz, n_t),
        in_specs=in_specs,
        out_specs=out_specs,
        out_shape=out_shape,
        scratch_shapes=[pltpu.VMEM((D_MODEL // HEAD_DIM, tm, HEAD_DIM), F32),
                        pltpu.VMEM((D_MODEL // HEAD_DIM, tm, HEAD_DIM), F32),
                        pltpu.VMEM((N_GROUPS, tm, D_MODEL), BF16),
                        pltpu.VMEM((tm + SUBLANES, CONV_WIDTH), F32)],
        compiler_params=pltpu.CompilerParams(
            dimension_semantics=("arbitrary", "arbitrary"),
            vmem_limit_bytes=_vmem_limit(weights + x_tiles + out_tiles + scratch)),
        name="in_proj",
    )(mod, x, w_in, conv_w, conv_b)


def _nt_dot(q, k):
    return lax.dot_general(q, k, (((1,), (1,)), ((), ())), preferred_element_type=F32)


def _softmax_pv(s, v):
    m = jnp.max(s, axis=1, keepdims=True)
    p = jnp.exp2(s - m)
    v_ext = jnp.concatenate([v, jnp.ones((v.shape[0], HEAD_DIM), BF16)], axis=1)
    acc = jnp.dot(p.astype(BF16), v_ext, preferred_element_type=F32)
    return acc[:, :HEAD_DIM], m, acc[:, HEAD_DIM:]


def _attn_kernel(qkv1_ref, qkv2_ref, qkv3_ref, bias_ref, sg_ref, o_ref,
                 sa1_ref, sa2_ref, sa3_ref, sb1_ref, sb2_ref, sb3_ref, *row_refs, seq):
    res1, res2, res3, stage3 = (row_refs[3 * g:3 * g + 3] for g in range(N_GROUPS + 1))
    step = pl.program_id(0)
    par = jnp.bitwise_and(step, 1)
    opar = 1 - par

    @pl.when(step == 0)
    def _():
        for r in res1 + res2 + res3:
            r[1] = jnp.ones((seq, HEAD_DIM), F32)

    d4, d16 = DILATED_GROUPS[1][1], DILATED_GROUPS[2][1]
    per = ATTN_BLOCKS_PER_STAGE
    n_it = seq // BLOCK // per
    blk4 = seq // d4 // BLOCK
    assert seq // d16 == BLOCK and blk4 & (blk4 - 1) == 0 and n_it % 2 == 0 and n_it * per * BLOCK == seq
    assert d16 == d4 * d4 and d4 == blk4
    shift4 = blk4.bit_length() - 1
    sub4 = seq // d4

    def rows(start, size, align=BLOCK):
        if isinstance(start, int):
            return pl.ds(start, size)
        return pl.ds(pl.multiple_of(start, align), size)

    def plan(it, u):
        it = it * per + u
        if isinstance(it, int):
            f1, nb, r4 = int(it == 0), it & (blk4 - 1), it >> shift4
            f2 = int(nb == 0)
        else:
            f1 = (it == 0).astype(jnp.int32)
            nb = jnp.bitwise_and(it, blk4 - 1)
            f2 = (nb == 0).astype(jnp.int32)
            r4 = lax.shift_right_logical(it, shift4)
        return dict(q=it * BLOCK, k1=(it - 1 + f1) * BLOCK, f1=f1, k2=(it - 1 + f2) * BLOCK, f2=f2,
                    out2=nb * (BLOCK * d4) + r4, out3=nb * sub4 + r4)

    def score_block(it, u, g, s_refs):
        p = plan(it, u)
        q = rows(p["q"], BLOCK)
        if g == 0:
            s_refs[0][u] = _nt_dot(qkv1_ref[0, 0, 0, q, :], qkv1_ref[0, 1, 0, rows(p["k1"], 2 * BLOCK), :]) \
                + bias_ref[0, 0, p["f1"]]
        elif g == 1:
            s_refs[1][u] = _nt_dot(qkv2_ref[0, 0, 0, q, :], qkv2_ref[0, 1, 0, rows(p["k2"], 2 * BLOCK), :]) \
                + bias_ref[1, 0, p["f2"]]
        else:
            s_refs[2][u] = _nt_dot(qkv3_ref[0, 0, 0, q, :], qkv3_ref[0, 1, 0, q, :]) \
                + bias_ref[2, 0, 1, :, 0:BLOCK]

    def put(dst, lead, idx, acc, m, l):
        dst[0][(*lead, idx, slice(None))] = acc
        dst[1][(*lead, idx, slice(None))] = jnp.broadcast_to(m, (BLOCK, HEAD_DIM))
        dst[2][(*lead, idx, slice(None))] = l

    def finish_block(it, u, g, s_refs):
        p = plan(it, u)
        q = rows(p["q"], BLOCK)
        if g == 0:
            put(res1, (par,), q, *_softmax_pv(s_refs[0][u], qkv1_ref[0, 2, 0, rows(p["k1"], 2 * BLOCK), :]))
        elif g == 1:
            put(res2, (par,), pl.ds(p["out2"], BLOCK, stride=d4),
                *_softmax_pv(s_refs[1][u], qkv2_ref[0, 2, 0, rows(p["k2"], 2 * BLOCK), :]))
        else:
            put(stage3, (), pl.ds(p["out3"], BLOCK, stride=d4),
                *_softmax_pv(s_refs[2][u], qkv3_ref[0, 2, 0, q, :]))

    def merge(base, n_rows):
        for c0 in range(0, n_rows, MERGE_ROWS):
            rws = rows(base + c0, MERGE_ROWS, MERGE_ROWS)
            ms = [r[1][opar, rws, :] for r in (res1, res2, res3)]
            m = jnp.maximum(jnp.maximum(ms[0], ms[1]), ms[2])
            es = [jnp.exp2(mg - m) for mg in ms]
            num = sum(e * r[0][opar, rws, :] for e, r in zip(es, (res1, res2, res3)))
            den = sum(e * r[2][opar, rws, :] for e, r in zip(es, (res1, res2, res3)))
            o_ref[0, 0, rws, :] = (num * (1.0 / den)).astype(BF16) * sg_ref[0, 0, rws, :]

    def section(it_scores, s_dst, it_finish, s_src, merge_base, merge_rows):
        n_p = N_GROUPS * per
        cuts = [MERGE_ROWS * ((merge_rows // MERGE_ROWS) * k // n_p) for k in range(n_p + 1)]
        for u in range(per):
            for g in range(N_GROUPS):
                k = N_GROUPS * u + g
                merge(merge_base + cuts[k], cuts[k + 1] - cuts[k])
                if it_scores is not None:
                    score_block(it_scores, u, g, s_dst)
                if it_finish is not None:
                    finish_block(it_finish, u, g, s_src)

    sec = seq // n_it
    s_a = (sa1_ref, sa2_ref, sa3_ref)
    s_b = (sb1_ref, sb2_ref, sb3_ref)
    section(0, s_a, None, None, 0, sec // 2)

    def body(i, carry):
        base = sec // 2 + 2 * i * sec
        section(2 * i + 1, s_b, 2 * i, s_a, base, sec)
        section(2 * i + 2, s_a, 2 * i + 1, s_b, base + sec, sec)
        return carry
    for i in range(n_it // 2 - 1):
        body(i, 0)
    tail = sec // 2 + (n_it - 2) * sec
    section(n_it - 1, s_b, n_it - 2, s_a, tail, sec)
    section(None, None, n_it - 1, s_b, tail + sec, seq - tail - sec)

    for c in range(d4):
        for p0 in range(0, sub4, BLOCK):
            src = slice(c * sub4 + p0, c * sub4 + p0 + BLOCK)
            dst = pl.ds(d4 * p0 + c, BLOCK, stride=d4)
            for k in range(3):
                res3[k][par, dst, :] = stage3[k][src, :]


def _attn_call(qkv1, qkv2, qkv3, bias, sg):
    bsz, _, hg, seq, _ = qkv1.shape
    qkv2 = qkv2.reshape(qkv1.shape)
    qkv3 = qkv3.reshape(qkv1.shape)
    bias = bias.reshape(N_GROUPS, hg, 2, BLOCK, 2 * BLOCK)
    n_slots = bsz * hg

    def cur(s):
        c = jnp.minimum(s, n_slots - 1)
        return c // hg, c % hg

    def prev(s):
        p = jnp.maximum(s - 1, 0)
        return p // hg, p % hg

    def qkv_map(s):
        b, j = cur(s)
        return (b, 0, j, 0, 0)

    def bias_map(s):
        return (0, cur(s)[1], 0, 0, 0)

    def lag_map(s):
        b, j = prev(s)
        return (b, j, 0, 0)

    in_tiles = (3 * 3 + 1) * seq * HEAD_DIM * 2 * 2 + N_GROUPS * 2 * BLOCK * 2 * BLOCK * 4 * 2
    out_tiles = seq * HEAD_DIM * 2 * 2
    per = ATTN_BLOCKS_PER_STAGE
    score_shapes = [(per, BLOCK, 2 * BLOCK), (per, BLOCK, 2 * BLOCK), (per, BLOCK, BLOCK)] * 2
    row_shapes = [(2, seq, HEAD_DIM)] * (3 * N_GROUPS) + [(seq, HEAD_DIM)] * 3
    scratch = sum(math.prod(sh) * 4 for sh in score_shapes + row_shapes)
    qkv_spec = pl.BlockSpec((1, 3, 1, seq, HEAD_DIM), qkv_map)
    return pl.pallas_call(
        functools.partial(_attn_kernel, seq=seq),
        grid=(n_slots + 1,),
        in_specs=[
            qkv_spec, qkv_spec, qkv_spec,
            pl.BlockSpec((N_GROUPS, 1, 2, BLOCK, 2 * BLOCK), bias_map),
            pl.BlockSpec((1, 1, seq, HEAD_DIM), lag_map),
        ],
        out_specs=pl.BlockSpec((1, 1, seq, HEAD_DIM), lag_map),
        out_shape=jax.ShapeDtypeStruct((bsz, hg, seq, HEAD_DIM), BF16),
        scratch_shapes=[pltpu.VMEM(sh, F32) for sh in score_shapes + row_shapes],
        compiler_params=pltpu.CompilerParams(
            dimension_semantics=("arbitrary",),
            vmem_limit_bytes=_vmem_limit(in_tiles + out_tiles + scratch)),
        name="dilated_attn",
    )(qkv1, qkv2, qkv3, bias, sg)


def _outproj_kernel(mod_ref, o_ref, spre_ref, x_ref, wma_ref, wmc_ref, wco_ref, wao_ref, wo_ref,
                    lng_ref, lnb_ref, out_ref, *, alpha, tm):
    shift = mod_ref[0, :, 0:D_MODEL]
    scale1 = 1.0 + mod_ref[0, :, D_MODEL:2 * D_MODEL]
    gate1 = 1.0 + mod_ref[0, :, 2 * D_MODEL:3 * D_MODEL]
    def branches(rws):
        h = (x_ref[0, rws, :] * scale1 + shift).astype(BF16)
        g_attn = _sigmoid(jnp.dot(h, wma_ref[...], preferred_element_type=F32))
        g_conv = _sigmoid(jnp.dot(h, wmc_ref[...], preferred_element_type=F32))
        a_in = jnp.concatenate([o_ref[0, j, rws, :] for j in range(HEADS_PER_GROUP)], axis=1)
        a_out = jnp.dot(a_in, wao_ref[...], preferred_element_type=F32)
        s_out = jnp.dot(spre_ref[0, rws, :], wco_ref[...], preferred_element_type=F32)
        return (g_attn * a_out + g_conv * s_out).astype(BF16)

    def norm(rws, y):
        v = alpha * x_ref[0, rws, :] + gate1 * y
        mu = jnp.mean(v, axis=-1, keepdims=True)
        dv = v - mu
        var = jnp.mean(dv * dv, axis=-1, keepdims=True)
        out_ref[0, rws, :] = dv * lax.rsqrt(var + LN_EPS) * lng_ref[...] + lnb_ref[...]

    chunks = [slice(r0, r0 + OUTPROJ_ROWS) for r0 in range(0, tm, OUTPROJ_ROWS)]
    merged = branches(chunks[0])
    for c, rws in enumerate(chunks):
        y = jnp.dot(merged, wo_ref[...], preferred_element_type=F32)
        if c + 1 < len(chunks):
            merged = branches(chunks[c + 1])
        norm(rws, y)


def _outproj_call(mod, o, spre, x, w_in, w_conv_out, w_attn_out, w_o, ln_g, ln_b, alpha):
    bsz, seq, _ = x.shape
    tm = OUTPROJ_TM
    assert seq % tm == 0 and tm % OUTPROJ_ROWS == 0
    assert COL_MATTN % D_MODEL == 0 and COL_MCONV % D_MODEL == 0
    hg = HEADS_PER_GROUP
    tok = lambda b, i: (b, i, 0)
    const = lambda b, i: (0, 0)
    resident = dict(pipeline_mode=pl.Buffered(1))
    tiles = (ATTN_WIDTH * 2 + CONV_WIDTH * 2 + 2 * D_MODEL * 4) * tm * 2
    weights = (2 * D_MODEL + CONV_WIDTH + ATTN_WIDTH + D_MODEL) * D_MODEL * 2
    temps = 8 * OUTPROJ_ROWS * D_MODEL * 4
    return pl.pallas_call(
        functools.partial(_outproj_kernel, alpha=alpha, tm=tm),
        grid=(bsz, seq // tm),
        in_specs=[
            pl.BlockSpec((1, 1, 3 * D_MODEL), lambda b, i: (b, 0, 0)),
            pl.BlockSpec((1, hg, tm, HEAD_DIM), lambda b, i: (b, 0, i, 0)),
            pl.BlockSpec((1, tm, CONV_WIDTH), tok),
            pl.BlockSpec((1, tm, D_MODEL), tok),
            pl.BlockSpec((D_MODEL, D_MODEL), lambda b, i: (0, COL_MATTN // D_MODEL), **resident),
            pl.BlockSpec((D_MODEL, D_MODEL), lambda b, i: (0, COL_MCONV // D_MODEL), **resident),
            pl.BlockSpec((CONV_WIDTH, D_MODEL), const, **resident),
            pl.BlockSpec((ATTN_WIDTH, D_MODEL), const, **resident),
            pl.BlockSpec((D_MODEL, D_MODEL), const, **resident),
            pl.BlockSpec((1, D_MODEL), const),
            pl.BlockSpec((1, D_MODEL), const),
        ],
        out_specs=pl.BlockSpec((1, tm, D_MODEL), tok),
        out_shape=jax.ShapeDtypeStruct((bsz, seq, D_MODEL), F32),
        compiler_params=pltpu.CompilerParams(
            dimension_semantics=("arbitrary", "arbitrary"),
            vmem_limit_bytes=_vmem_limit(tiles + weights + temps)),
        name="out_proj",
    )(mod, o, spre, x, w_in, w_in, w_conv_out, w_attn_out, w_o, ln_g, ln_b)


def kernel(x, c, w_ada, b_ada, w_in, conv_w, conv_b, rel_bias, w_attn_out, w_conv_out, w_o, ln_g, ln_b):
    bsz = x.shape[0]
    depth = w_in.shape[0]
    alpha = (2.0 * depth) ** 0.25
    bias = _bias_call(rel_bias)
    for layer in range(depth):
        mod = _ada_call(c, w_ada[layer], b_ada[layer]).reshape(bsz, 1, 3 * D_MODEL)
        w_in_b = w_in[layer].astype(BF16)
        qkv1, qkv2, qkv3, sg, spre = _inproj_call(mod, x, w_in_b, conv_w[layer], conv_b[layer][None, :])
        o = _attn_call(qkv1, qkv2, qkv3, bias, sg)
        x = _outproj_call(mod, o, spre, x, w_in_b, w_conv_out[layer].astype(BF16),
                          w_attn_out[layer].astype(BF16), w_o[layer].astype(BF16),
                          ln_g[layer][None, :], ln_b[layer][None, :], alpha)
    return x
```

```python
import functools
import math

import numpy as np
import jax
import jax.numpy as jnp
from jax import lax
from jax.experimental import pallas as pl
from jax.experimental.pallas import tpu as pltpu

F32 = jnp.float32
BF16 = jnp.bfloat16

D_MODEL = 1024
HEAD_DIM = 128
HEADS_PER_GROUP = 4
DILATED_GROUPS = ((128, 1), (512, 4), (2048, 16))
N_GROUPS = len(DILATED_GROUPS)
N_ATTN_HEADS = N_GROUPS * HEADS_PER_GROUP
QKV_WIDTH = N_ATTN_HEADS * HEAD_DIM
ATTN_WIDTH = HEADS_PER_GROUP * HEAD_DIM
CONV_WIDTH = D_MODEL
CONV_K = 3
N_BUCKETS = 32
MAX_EXACT = 16
MAX_DISTANCE = 2048
BLOCK = 128
LN_EPS = 1e-5
NEG_INF = -1e30
LOG2E = math.log2(math.e)
Q_SCALE = HEAD_DIM ** -0.5 * LOG2E

COL_Q = 0
COL_K = QKV_WIDTH
COL_V = 2 * QKV_WIDTH
COL_GATTN = 3 * QKV_WIDTH
COL_U = COL_GATTN + ATTN_WIDTH
COL_B = COL_U + CONV_WIDTH
COL_C = COL_B + CONV_WIDTH
COL_GCONV = COL_C + CONV_WIDTH
COL_MATTN = COL_GCONV + CONV_WIDTH
COL_MCONV = COL_MATTN + D_MODEL
N_COLS = COL_MCONV + D_MODEL

V7X_VMEM_BYTES = 64 * 1024 * 1024
SUBLANES = 8
INPROJ_TM = 512
OUTPROJ_TM = 1024
OUTPROJ_ROWS = 256
CONV_CHUNK = 256
ATTN_LAG = 2
MERGE_ROWS = 32


def _vmem_limit(estimate_bytes):
    return int(min(V7X_VMEM_BYTES - (4 << 20), estimate_bytes * 5 // 4 + (4 << 20)))


def _bucket_table():
    a = np.arange(BLOCK)[:, None]
    b = np.arange(2 * BLOCK)[None, :]
    steps = a + BLOCK - b
    out = []
    for window, dilation in DILATED_GROUPS:
        n_steps = window // dilation
        valid = (steps >= 0) & (steps <= n_steps)
        dist = np.maximum(steps, 0) * dilation
        n = np.maximum(dist, 1).astype(np.float32)
        large = MAX_EXACT + (np.log(n / np.float32(MAX_EXACT)) / np.float32(math.log(MAX_DISTANCE / MAX_EXACT))
                             * np.float32(N_BUCKETS - MAX_EXACT)).astype(np.int32)
        large = np.minimum(large, N_BUCKETS - 1)
        bucket = np.where(dist < MAX_EXACT, dist, large)
        out.append(np.where(valid, bucket, -1).astype(np.int32))
    return np.stack(out)


_BUCKETS = _bucket_table()


def _sigmoid(v):
    return 0.5 * jnp.tanh(0.5 * v) + 0.5


def _ada_kernel(c_ref, w_ref, b_ref, o_ref):
    c = c_ref[...]
    o_ref[...] = jnp.dot(c * _sigmoid(c), w_ref[...], preferred_element_type=F32) + b_ref[...]


def _ada_call(c, w_ada, b_ada):
    bsz = c.shape[0]
    return pl.pallas_call(
        _ada_kernel,
        out_shape=jax.ShapeDtypeStruct((bsz, 3 * D_MODEL), F32),
        name="ada_mod",
    )(c, w_ada, b_ada[None, :])


def _bias_kernel(rb_ref, bucket_ref, o_ref):
    for g in range(N_GROUPS):
        bk = bucket_ref[g]
        for j in range(HEADS_PER_GROUP):
            h = g * HEADS_PER_GROUP + j
            acc = jnp.full(bk.shape, NEG_INF, F32)
            for k in range(N_BUCKETS):
                acc = jnp.where(bk == k, rb_ref[k, h] * LOG2E, acc)
            o_ref[h, 0] = acc
            o_ref[h, 1] = jnp.concatenate(
                [acc[:, BLOCK:], jnp.full((BLOCK, BLOCK), NEG_INF, F32)], axis=1)


def _bias_call(rel_bias):
    return pl.pallas_call(
        _bias_kernel,
        out_shape=jax.ShapeDtypeStruct((N_ATTN_HEADS, 2, BLOCK, 2 * BLOCK), F32),
        in_specs=[pl.BlockSpec(memory_space=pltpu.SMEM), pl.BlockSpec(memory_space=pltpu.VMEM)],
        out_specs=pl.BlockSpec(memory_space=pltpu.VMEM),
        name="bias_tables",
    )(rel_bias, jnp.asarray(_BUCKETS))


def _inproj_kernel(mod_ref, x_ref, w_ref, cw_ref, cb_ref,
                   qkv1_ref, qkv2_ref, qkv3_ref, sg_ref, spre_ref,
                   hs_ref, hs4_ref, hb_ref, zbuf_ref, *, tm):
    i = pl.program_id(1)

    @pl.when(i == 0)
    def _():
        zbuf_ref[0:SUBLANES, :] = jnp.zeros((SUBLANES, CONV_WIDTH), F32)

    n_slab = D_MODEL // HEAD_DIM
    for s in range(n_slab):
        ls = slice(s * HEAD_DIM, (s + 1) * HEAD_DIM)
        h = x_ref[0, :, ls] * (1.0 + mod_ref[0, :, D_MODEL + s * HEAD_DIM:D_MODEL + (s + 1) * HEAD_DIM]) \
            + mod_ref[0, :, ls]
        hb_ref[0, :, ls] = h.astype(BF16)
        hs_ref[s] = h

    d4, d16 = DILATED_GROUPS[1][1], DILATED_GROUPS[2][1]
    assert d16 == d4 * d4
    n4, n16 = tm // d4, tm // d16
    for s in range(n_slab):
        ls = slice(s * HEAD_DIM, (s + 1) * HEAD_DIM)
        for r in range(d4):
            part = hs_ref[s, pl.ds(r, n4, stride=d4), :]
            hs4_ref[s, r * n4:(r + 1) * n4, :] = part
            hb_ref[1, r * n4:(r + 1) * n4, ls] = part.astype(BF16)
    for s in range(n_slab):
        ls = slice(s * HEAD_DIM, (s + 1) * HEAD_DIM)
        for r16 in range(d16):
            part = hs4_ref[s, pl.ds((r16 % d4) * n4 + r16 // d4, n16, stride=d4), :]
            hb_ref[2, r16 * n16:(r16 + 1) * n16, ls] = part.astype(BF16)

    def proj(g, c0, width):
        return jnp.dot(hb_ref[g], w_ref[:, c0:c0 + width], preferred_element_type=F32)

    def qkv(g, store):
        for t, col in enumerate((COL_Q, COL_K, COL_V)):
            r = proj(g, col + g * ATTN_WIDTH, ATTN_WIDTH)
            if t == 0:
                r = r * Q_SCALE
            r = r.astype(BF16)
            for j in range(HEADS_PER_GROUP):
                store(t, j, r[:, j * HEAD_DIM:(j + 1) * HEAD_DIM])

    def store1(t, j, v):
        qkv1_ref[0, t, j] = v
    qkv(0, store1)

    ga = proj(0, COL_GATTN, ATTN_WIDTH)
    sgv = (ga * _sigmoid(ga)).astype(BF16)
    for j in range(HEADS_PER_GROUP):
        sg_ref[0, j] = sgv[:, j * HEAD_DIM:(j + 1) * HEAD_DIM]

    for c0 in range(0, CONV_WIDTH, CONV_CHUNK):
        cs = slice(c0, c0 + CONV_CHUNK)
        u = proj(0, COL_U + c0, CONV_CHUNK)
        cg = proj(0, COL_C + c0, CONV_CHUNK)
        z = cg * u
        zbuf_ref[SUBLANES:SUBLANES + tm, cs] = z
        zm1 = zbuf_ref[SUBLANES - 1:SUBLANES - 1 + tm, cs]
        zm2 = zbuf_ref[SUBLANES - 2:SUBLANES - 2 + tm, cs]
        y = cw_ref[0:1, cs] * zm2 + cw_ref[1:2, cs] * zm1 + cw_ref[2:3, cs] * z + cb_ref[:, cs]
        bg = proj(0, COL_B + c0, CONV_CHUNK)
        gc = proj(0, COL_GCONV + c0, CONV_CHUNK)
        spre_ref[0, :, cs] = (bg * y * (gc * _sigmoid(gc))).astype(BF16)
    zbuf_ref[0:SUBLANES, :] = zbuf_ref[tm:tm + SUBLANES, :]

    def store_sub(ref, d):
        def store(t, j, v):
            rows = tm // d
            for r in range(d):
                ref[0, t, j, r] = v[r * rows:(r + 1) * rows]
        return store
    qkv(1, store_sub(qkv2_ref, d4))
    qkv(2, store_sub(qkv3_ref, d16))


def _inproj_call(mod, x, w_in, conv_w, conv_b):
    bsz, seq, _ = x.shape
    tm = INPROJ_TM
    n_t = seq // tm
    d4, d16 = DILATED_GROUPS[1][1], DILATED_GROUPS[2][1]
    sub4, sub16 = seq // d4, seq // d16
    assert seq % tm == 0 and tm % (d16 * 16) == 0
    hg = HEADS_PER_GROUP

    in_specs = [
        pl.BlockSpec((1, 1, 3 * D_MODEL), lambda b, i: (b, 0, 0)),
        pl.BlockSpec((1, tm, D_MODEL), lambda b, i: (b, i, 0)),
        pl.BlockSpec((D_MODEL, COL_MATTN), lambda b, i: (0, 0), pipeline_mode=pl.Buffered(1)),
        pl.BlockSpec((CONV_K, CONV_WIDTH), lambda b, i: (0, 0)),
        pl.BlockSpec((1, CONV_WIDTH), lambda b, i: (0, 0)),
    ]
    out_shape = [
        jax.ShapeDtypeStruct((bsz, 3, hg, seq, HEAD_DIM), BF16),
        jax.ShapeDtypeStruct((bsz, 3, hg, d4, sub4, HEAD_DIM), BF16),
        jax.ShapeDtypeStruct((bsz, 3, hg, d16, sub16, HEAD_DIM), BF16),
        jax.ShapeDtypeStruct((bsz, hg, seq, HEAD_DIM), BF16),
        jax.ShapeDtypeStruct((bsz, seq, CONV_WIDTH), BF16),
    ]
    out_specs = [
        pl.BlockSpec((1, 3, hg, tm, HEAD_DIM), lambda b, i: (b, 0, 0, i, 0)),
        pl.BlockSpec((1, 3, hg, d4, tm // d4, HEAD_DIM), lambda b, i: (b, 0, 0, 0, i, 0)),
        pl.BlockSpec((1, 3, hg, d16, tm // d16, HEAD_DIM), lambda b, i: (b, 0, 0, 0, i, 0)),
        pl.BlockSpec((1, hg, tm, HEAD_DIM), lambda b, i: (b, 0, i, 0)),
        pl.BlockSpec((1, tm, CONV_WIDTH), lambda b, i: (b, i, 0)),
    ]
    weights = D_MODEL * COL_MATTN * 2
    x_tiles = tm * D_MODEL * 4 * 2
    out_tiles = (3 * 3 * ATTN_WIDTH + ATTN_WIDTH + CONV_WIDTH) * tm * 2 * 2
    scratch = 2 * tm * D_MODEL * 4 + N_GROUPS * tm * D_MODEL * 2 + (tm + SUBLANES) * CONV_WIDTH * 4
    return pl.pallas_call(
        functools.partial(_inproj_kernel, tm=tm),
        grid=(bsz, n_t),
        in_specs=in_specs,
        out_specs=out_specs,
        out_shape=out_shape,
        scratch_shapes=[pltpu.VMEM((D_MODEL // HEAD_DIM, tm, HEAD_DIM), F32),
                        pltpu.VMEM((D_MODEL // HEAD_DIM, tm, HEAD_DIM), F32),
                        pltpu.VMEM((N_GROUPS, tm, D_MODEL), BF16),
                        pltpu.VMEM((tm + SUBLANES, CONV_WIDTH), F32)],
        compiler_params=pltpu.CompilerParams(
            dimension_semantics=("arbitrary", "arbitrary"),
            vmem_limit_bytes=_vmem_limit(weights + x_tiles + out_tiles + scratch)),
        name="in_proj",
    )(mod, x, w_in, conv_w, conv_b)


def _nt_dot(q, k):
    return lax.dot_general(q, k, (((1,), (1,)), ((), ())), preferred_element_type=F32)


def _softmax_pv(s, v):
    m = jnp.max(s, axis=1, keepdims=True)
    p = jnp.exp2(s - m)
    v_ext = jnp.concatenate([v, jnp.ones((v.shape[0], HEAD_DIM), BF16)], axis=1)
    acc = jnp.dot(p.astype(BF16), v_ext, preferred_element_type=F32)
    return acc[:, :HEAD_DIM], m, acc[:, HEAD_DIM:]


def _attn_kernel(qkv1_ref, qkv2_ref, qkv3_ref, bias_ref, sg_ref, o_ref, *row_refs, seq):
    res1, res2, res3, stage3 = (row_refs[3 * g:3 * g + 3] for g in range(N_GROUPS + 1))
    step = pl.program_id(0)
    par = jnp.bitwise_and(step, 1)
    opar = 1 - par

    @pl.when(step == 0)
    def _():
        for r in res1 + res2 + res3:
            r[1] = jnp.ones((seq, HEAD_DIM), F32)

    d4, d16 = DILATED_GROUPS[1][1], DILATED_GROUPS[2][1]
    n_blk = seq // BLOCK
    blk4 = seq // d4 // BLOCK
    assert seq // d16 == BLOCK and d16 == d4 * d4 and d4 == blk4
    sub4 = seq // d4

    def plan(blk):
        f1, nb, r4 = int(blk == 0), blk % blk4, blk // blk4
        f2 = int(nb == 0)
        return dict(q=pl.ds(blk * BLOCK, BLOCK), f1=f1, f2=f2,
                    k1=pl.ds((blk - 1 + f1) * BLOCK, 2 * BLOCK), k2=pl.ds((blk - 1 + f2) * BLOCK, 2 * BLOCK),
                    out2=pl.ds(nb * (BLOCK * d4) + r4, BLOCK, stride=d4),
                    out3=pl.ds(nb * sub4 + r4, BLOCK, stride=d4))

    def score_block(blk, g):
        p = plan(blk)
        if g == 0:
            return _nt_dot(qkv1_ref[0, 0, 0, p["q"], :], qkv1_ref[0, 1, 0, p["k1"], :]) + bias_ref[0, 0, p["f1"]]
        if g == 1:
            return _nt_dot(qkv2_ref[0, 0, 0, p["q"], :], qkv2_ref[0, 1, 0, p["k2"], :]) + bias_ref[1, 0, p["f2"]]
        return _nt_dot(qkv3_ref[0, 0, 0, p["q"], :], qkv3_ref[0, 1, 0, p["q"], :]) + bias_ref[2, 0, 1, :, 0:BLOCK]

    def put(dst, lead, idx, acc, m, l):
        dst[0][(*lead, idx, slice(None))] = acc
        dst[1][(*lead, idx, slice(None))] = jnp.broadcast_to(m, (BLOCK, HEAD_DIM))
        dst[2][(*lead, idx, slice(None))] = l

    def finish_block(blk, g, s):
        p = plan(blk)
        if g == 0:
            put(res1, (par,), p["q"], *_softmax_pv(s, qkv1_ref[0, 2, 0, p["k1"], :]))
        elif g == 1:
            put(res2, (par,), p["out2"], *_softmax_pv(s, qkv2_ref[0, 2, 0, p["k2"], :]))
        else:
            put(stage3, (), p["out3"], *_softmax_pv(s, qkv3_ref[0, 2, 0, p["q"], :]))

    def merge(base, n_rows):
        for c0 in range(base, base + n_rows, MERGE_ROWS):
            rws = pl.ds(c0, MERGE_ROWS)
            ms = [r[1][opar, rws, :] for r in (res1, res2, res3)]
            m = jnp.maximum(jnp.maximum(ms[0], ms[1]), ms[2])
            es = [jnp.exp2(mg - m) for mg in ms]
            num = sum(e * r[0][opar, rws, :] for e, r in zip(es, (res1, res2, res3)))
            den = sum(e * r[2][opar, rws, :] for e, r in zip(es, (res1, res2, res3)))
            o_ref[0, 0, rws, :] = (num * (1.0 / den)).astype(BF16) * sg_ref[0, 0, rws, :]

    blocks = [(blk, g) for blk in range(n_blk) for g in range(N_GROUPS)]
    n_b = len(blocks)
    cuts = [MERGE_ROWS * ((seq // MERGE_ROWS) * k // n_b) for k in range(n_b + 1)]
    pending = {k: score_block(*blocks[k]) for k in range(min(ATTN_LAG, n_b))}
    for k in range(n_b):
        merge(cuts[k], cuts[k + 1] - cuts[k])
        if k + ATTN_LAG < n_b:
            pending[k + ATTN_LAG] = score_block(*blocks[k + ATTN_LAG])
        finish_block(*blocks[k], pending.pop(k))

    for c in range(d4):
        for p0 in range(0, sub4, BLOCK):
            src = slice(c * sub4 + p0, c * sub4 + p0 + BLOCK)
            dst = pl.ds(d4 * p0 + c, BLOCK, stride=d4)
            for k in range(3):
                res3[k][par, dst, :] = stage3[k][src, :]


def _attn_call(qkv1, qkv2, qkv3, bias, sg):
    bsz, _, hg, seq, _ = qkv1.shape
    qkv2 = qkv2.reshape(qkv1.shape)
    qkv3 = qkv3.reshape(qkv1.shape)
    bias = bias.reshape(N_GROUPS, hg, 2, BLOCK, 2 * BLOCK)
    n_slots = bsz * hg

    def cur(s):
        c = jnp.minimum(s, n_slots - 1)
        return c // hg, c % hg

    def prev(s):
        p = jnp.maximum(s - 1, 0)
        return p // hg, p % hg

    def qkv_map(s):
        b, j = cur(s)
        return (b, 0, j, 0, 0)

    def bias_map(s):
        return (0, cur(s)[1], 0, 0, 0)

    def lag_map(s):
        b, j = prev(s)
        return (b, j, 0, 0)

    in_tiles = (3 * 3 + 1) * seq * HEAD_DIM * 2 * 2 + N_GROUPS * 2 * BLOCK * 2 * BLOCK * 4 * 2
    out_tiles = seq * HEAD_DIM * 2 * 2
    row_shapes = [(2, seq, HEAD_DIM)] * (3 * N_GROUPS) + [(seq, HEAD_DIM)] * 3
    scratch = sum(math.prod(sh) * 4 for sh in row_shapes)
    qkv_spec = pl.BlockSpec((1, 3, 1, seq, HEAD_DIM), qkv_map)
    return pl.pallas_call(
        functools.partial(_attn_kernel, seq=seq),
        grid=(n_slots + 1,),
        in_specs=[
            qkv_spec, qkv_spec, qkv_spec,
            pl.BlockSpec((N_GROUPS, 1, 2, BLOCK, 2 * BLOCK), bias_map),
            pl.BlockSpec((1, 1, seq, HEAD_DIM), lag_map),
        ],
        out_specs=pl.BlockSpec((1, 1, seq, HEAD_DIM), lag_map),
        out_shape=jax.ShapeDtypeStruct((bsz, hg, seq, HEAD_DIM), BF16),
        scratch_shapes=[pltpu.VMEM(sh, F32) for sh in row_shapes],
        compiler_params=pltpu.CompilerParams(
            dimension_semantics=("arbitrary",),
            vmem_limit_bytes=_vmem_limit(in_tiles + out_tiles + scratch)),
        name="dilated_attn",
    )(qkv1, qkv2, qkv3, bias, sg)


def _outproj_kernel(mod_ref, o_ref, spre_ref, x_ref, wma_ref, wmc_ref, wco_ref, wao_ref, wo_ref,
                    lng_ref, lnb_ref, out_ref, *, alpha, tm):
    shift = mod_ref[0, :, 0:D_MODEL]
    scale1 = 1.0 + mod_ref[0, :, D_MODEL:2 * D_MODEL]
    gate1 = 1.0 + mod_ref[0, :, 2 * D_MODEL:3 * D_MODEL]

    def branches(rws):
        h = (x_ref[0, rws, :] * scale1 + shift).astype(BF16)
        g_attn = _sigmoid(jnp.dot(h, wma_ref[...], preferred_element_type=F32))
        g_conv = _sigmoid(jnp.dot(h, wmc_ref[...], preferred_element_type=F32))
        a_in = jnp.concatenate([o_ref[0, j, rws, :] for j in range(HEADS_PER_GROUP)], axis=1)
        a_out = jnp.dot(a_in, wao_ref[...], preferred_element_type=F32)
        s_out = jnp.dot(spre_ref[0, rws, :], wco_ref[...], preferred_element_type=F32)
        return (g_attn * a_out + g_conv * s_out).astype(BF16)

    def norm(rws, y):
        v = alpha * x_ref[0, rws, :] + gate1 * y
        mu = jnp.mean(v, axis=-1, keepdims=True)
        dv = v - mu
        var = jnp.mean(dv * dv, axis=-1, keepdims=True)
        out_ref[0, rws, :] = dv * lax.rsqrt(var + LN_EPS) * lng_ref[...] + lnb_ref[...]

    chunks = [slice(r0, r0 + OUTPROJ_ROWS) for r0 in range(0, tm, OUTPROJ_ROWS)]
    merged = branches(chunks[0])
    for c, rws in enumerate(chunks):
        y = jnp.dot(merged, wo_ref[...], preferred_element_type=F32)
        if c + 1 < len(chunks):
            merged = branches(chunks[c + 1])
        norm(rws, y)


def _outproj_call(mod, o, spre, x, w_in, w_conv_out, w_attn_out, w_o, ln_g, ln_b, alpha):
    bsz, seq, _ = x.shape
    tm = OUTPROJ_TM
    assert seq % tm == 0 and tm % OUTPROJ_ROWS == 0
    assert COL_MATTN % D_MODEL == 0 and COL_MCONV % D_MODEL == 0
    hg = HEADS_PER_GROUP
    tok = lambda b, i: (b, i, 0)
    const = lambda b, i: (0, 0)
    resident = dict(pipeline_mode=pl.Buffered(1))
    tiles = (ATTN_WIDTH * 2 + CONV_WIDTH * 2 + 2 * D_MODEL * 4) * tm * 2
    weights = (2 * D_MODEL + CONV_WIDTH + ATTN_WIDTH + D_MODEL) * D_MODEL * 2
    temps = 8 * OUTPROJ_ROWS * D_MODEL * 4
    return pl.pallas_call(
        functools.partial(_outproj_kernel, alpha=alpha, tm=tm),
        grid=(bsz, seq // tm),
        in_specs=[
            pl.BlockSpec((1, 1, 3 * D_MODEL), lambda b, i: (b, 0, 0)),
            pl.BlockSpec((1, hg, tm, HEAD_DIM), lambda b, i: (b, 0, i, 0)),
            pl.BlockSpec((1, tm, CONV_WIDTH), tok),
            pl.BlockSpec((1, tm, D_MODEL), tok),
            pl.BlockSpec((D_MODEL, D_MODEL), lambda b, i: (0, COL_MATTN // D_MODEL), **resident),
            pl.BlockSpec((D_MODEL, D_MODEL), lambda b, i: (0, COL_MCONV // D_MODEL), **resident),
            pl.BlockSpec((CONV_WIDTH, D_MODEL), const, **resident),
            pl.BlockSpec((ATTN_WIDTH, D_MODEL), const, **resident),
            pl.BlockSpec((D_MODEL, D_MODEL), const, **resident),
            pl.BlockSpec((1, D_MODEL), const),
            pl.BlockSpec((1, D_MODEL), const),
        ],
        out_specs=pl.BlockSpec((1, tm, D_MODEL), tok),
        out_shape=jax.ShapeDtypeStruct((bsz, seq, D_MODEL), F32),
        compiler_params=pltpu.CompilerParams(
            dimension_semantics=("arbitrary", "arbitrary"),
            vmem_limit_bytes=_vmem_limit(tiles + weights + temps)),
        name="out_proj",
    )(mod, o, spre, x, w_in, w_in, w_conv_out, w_attn_out, w_o, ln_g, ln_b)


def kernel(x, c, w_ada, b_ada, w_in, conv_w, conv_b, rel_bias, w_attn_out, w_conv_out, w_o, ln_g, ln_b):
    bsz = x.shape[0]
    depth = w_in.shape[0]
    alpha = (2.0 * depth) ** 0.25
    bias = _bias_call(rel_bias)
    for layer in range(depth):
        mod = _ada_call(c, w_ada[layer], b_ada[layer]).reshape(bsz, 1, 3 * D_MODEL)
        w_in_b = w_in[layer].astype(BF16)
        qkv1, qkv2, qkv3, sg, spre = _inproj_call(mod, x, w_in_b, conv_w[layer], conv_b[layer][None, :])
        o = _attn_call(qkv1, qkv2, qkv3, bias, sg)
        x = _outproj_call(mod, o, spre, x, w_in_b, w_conv_out[layer].astype(BF16),
                          w_attn_out[layer].astype(BF16), w_o[layer].astype(BF16),
                          ln_g[layer][None, :], ln_b[layer][None, :], alpha)
    return x
```

```python
import functools
import math

import numpy as np
import jax
import jax.numpy as jnp
from jax import lax
from jax.experimental import pallas as pl
from jax.experimental.pallas import tpu as pltpu

F32 = jnp.float32
BF16 = jnp.bfloat16

D_MODEL = 1024
HEAD_DIM = 128
HEADS_PER_GROUP = 4
DILATED_GROUPS = ((128, 1), (512, 4), (2048, 16))
N_GROUPS = len(DILATED_GROUPS)
N_ATTN_HEADS = N_GROUPS * HEADS_PER_GROUP
QKV_WIDTH = N_ATTN_HEADS * HEAD_DIM
ATTN_WIDTH = HEADS_PER_GROUP * HEAD_DIM
CONV_WIDTH = D_MODEL
CONV_K = 3
N_BUCKETS = 32
MAX_EXACT = 16
MAX_DISTANCE = 2048
BLOCK = 128
LN_EPS = 1e-5
NEG_INF = -1e30
LOG2E = math.log2(math.e)
Q_SCALE = HEAD_DIM ** -0.5 * LOG2E

COL_Q = 0
COL_K = QKV_WIDTH
COL_V = 2 * QKV_WIDTH
COL_GATTN = 3 * QKV_WIDTH
COL_U = COL_GATTN + ATTN_WIDTH
COL_B = COL_U + CONV_WIDTH
COL_C = COL_B + CONV_WIDTH
COL_GCONV = COL_C + CONV_WIDTH
COL_MATTN = COL_GCONV + CONV_WIDTH
COL_MCONV = COL_MATTN + D_MODEL
N_COLS = COL_MCONV + D_MODEL

V7X_VMEM_BYTES = 64 * 1024 * 1024
SUBLANES = 8
INPROJ_TM = 512
OUTPROJ_TM = 1024
OUTPROJ_ROWS = 256
CONV_CHUNK = 256
ATTN_LAG = 2
MERGE_ROWS = 32


def _vmem_limit(estimate_bytes):
    return int(min(V7X_VMEM_BYTES - (4 << 20), estimate_bytes * 5 // 4 + (4 << 20)))


def _bucket_table():
    a = np.arange(BLOCK)[:, None]
    b = np.arange(2 * BLOCK)[None, :]
    steps = a + BLOCK - b
    out = []
    for window, dilation in DILATED_GROUPS:
        n_steps = window // dilation
        valid = (steps >= 0) & (steps <= n_steps)
        dist = np.maximum(steps, 0) * dilation
        n = np.maximum(dist, 1).astype(np.float32)
        large = MAX_EXACT + (np.log(n / np.float32(MAX_EXACT)) / np.float32(math.log(MAX_DISTANCE / MAX_EXACT))
                             * np.float32(N_BUCKETS - MAX_EXACT)).astype(np.int32)
        large = np.minimum(large, N_BUCKETS - 1)
        bucket = np.where(dist < MAX_EXACT, dist, large)
        out.append(np.where(valid, bucket, -1).astype(np.int32))
    return np.stack(out)


_BUCKETS = _bucket_table()


def _sigmoid(v):
    return 0.5 * jnp.tanh(0.5 * v) + 0.5


def _ada_kernel(c_ref, w_ref, b_ref, o_ref):
    c = c_ref[...]
    o_ref[...] = jnp.dot(c * _sigmoid(c), w_ref[...], preferred_element_type=F32) + b_ref[...]


def _ada_call(c, w_ada, b_ada):
    bsz = c.shape[0]
    return pl.pallas_call(
        _ada_kernel,
        out_shape=jax.ShapeDtypeStruct((bsz, 3 * D_MODEL), F32),
        name="ada_mod",
    )(c, w_ada, b_ada[None, :])


def _bias_kernel(rb_ref, bucket_ref, o_ref):
    for g in range(N_GROUPS):
        bk = bucket_ref[g]
        for j in range(HEADS_PER_GROUP):
            h = g * HEADS_PER_GROUP + j
            acc = jnp.full(bk.shape, NEG_INF, F32)
            for k in range(N_BUCKETS):
                acc = jnp.where(bk == k, rb_ref[k, h] * LOG2E, acc)
            o_ref[h, 0] = acc
            o_ref[h, 1] = jnp.concatenate(
                [acc[:, BLOCK:], jnp.full((BLOCK, BLOCK), NEG_INF, F32)], axis=1)


def _bias_call(rel_bias):
    return pl.pallas_call(
        _bias_kernel,
        out_shape=jax.ShapeDtypeStruct((N_ATTN_HEADS, 2, BLOCK, 2 * BLOCK), F32),
        in_specs=[pl.BlockSpec(memory_space=pltpu.SMEM), pl.BlockSpec(memory_space=pltpu.VMEM)],
        out_specs=pl.BlockSpec(memory_space=pltpu.VMEM),
        name="bias_tables",
    )(rel_bias, jnp.asarray(_BUCKETS))


def _inproj_kernel(mod_ref, x_ref, w_ref, cw_ref, cb_ref,
                   qkv1_ref, qkv2_ref, qkv3_ref, sg_ref, spre_ref,
                   hs_ref, hs4_ref, hb_ref, zbuf_ref, *, tm):
    i = pl.program_id(1)

    @pl.when(i == 0)
    def _():
        zbuf_ref[0:SUBLANES, :] = jnp.zeros((SUBLANES, CONV_WIDTH), F32)

    n_slab = D_MODEL // HEAD_DIM
    for s in range(n_slab):
        ls = slice(s * HEAD_DIM, (s + 1) * HEAD_DIM)
        h = x_ref[0, :, ls] * (1.0 + mod_ref[0, :, D_MODEL + s * HEAD_DIM:D_MODEL + (s + 1) * HEAD_DIM]) \
            + mod_ref[0, :, ls]
        hb_ref[0, :, ls] = h.astype(BF16)
        hs_ref[s] = h

    d4, d16 = DILATED_GROUPS[1][1], DILATED_GROUPS[2][1]
    assert d16 == d4 * d4
    n4, n16 = tm // d4, tm // d16
    for s in range(n_slab):
        ls = slice(s * HEAD_DIM, (s + 1) * HEAD_DIM)
        for r in range(d4):
            part = hs_ref[s, pl.ds(r, n4, stride=d4), :]
            hs4_ref[s, r * n4:(r + 1) * n4, :] = part
            hb_ref[1, r * n4:(r + 1) * n4, ls] = part.astype(BF16)
    for s in range(n_slab):
        ls = slice(s * HEAD_DIM, (s + 1) * HEAD_DIM)
        for r16 in range(d16):
            part = hs4_ref[s, pl.ds((r16 % d4) * n4 + r16 // d4, n16, stride=d4), :]
            hb_ref[2, r16 * n16:(r16 + 1) * n16, ls] = part.astype(BF16)

    def proj(g, c0, width):
        return jnp.dot(hb_ref[g], w_ref[:, c0:c0 + width], preferred_element_type=F32)

    def qkv(g, store):
        for t, col in enumerate((COL_Q, COL_K, COL_V)):
            r = proj(g, col + g * ATTN_WIDTH, ATTN_WIDTH)
            if t == 0:
                r = r * Q_SCALE
            r = r.astype(BF16)
            for j in range(HEADS_PER_GROUP):
                store(t, j, r[:, j * HEAD_DIM:(j + 1) * HEAD_DIM])

    def store1(t, j, v):
        qkv1_ref[0, t, j] = v
    qkv(0, store1)

    ga = proj(0, COL_GATTN, ATTN_WIDTH)
    sgv = (ga * _sigmoid(ga)).astype(BF16)
    for j in range(HEADS_PER_GROUP):
        sg_ref[0, j] = sgv[:, j * HEAD_DIM:(j + 1) * HEAD_DIM]

    for c0 in range(0, CONV_WIDTH, CONV_CHUNK):
        cs = slice(c0, c0 + CONV_CHUNK)
        u = proj(0, COL_U + c0, CONV_CHUNK)
        cg = proj(0, COL_C + c0, CONV_CHUNK)
        z = cg * u
        zbuf_ref[SUBLANES:SUBLANES + tm, cs] = z
        zm1 = zbuf_ref[SUBLANES - 1:SUBLANES - 1 + tm, cs]
        zm2 = zbuf_ref[SUBLANES - 2:SUBLANES - 2 + tm, cs]
        y = cw_ref[0:1, cs] * zm2 + cw_ref[1:2, cs] * zm1 + cw_ref[2:3, cs] * z + cb_ref[:, cs]
        bg = proj(0, COL_B + c0, CONV_CHUNK)
        gc = proj(0, COL_GCONV + c0, CONV_CHUNK)
        spre_ref[0, :, cs] = (bg * y * (gc * _sigmoid(gc))).astype(BF16)
    zbuf_ref[0:SUBLANES, :] = zbuf_ref[tm:tm + SUBLANES, :]

    def store_sub(ref, d):
        def store(t, j, v):
            rows = tm // d
            for r in range(d):
                ref[0, t, j, r] = v[r * rows:(r + 1) * rows]
        return store
    qkv(1, store_sub(qkv2_ref, d4))
    qkv(2, store_sub(qkv3_ref, d16))


def _inproj_call(mod, x, w_in, conv_w, conv_b):
    bsz, seq, _ = x.shape
    tm = INPROJ_TM
    n_t = seq // tm
    d4, d16 = DILATED_GROUPS[1][1], DILATED_GROUPS[2][1]
    sub4, sub16 = seq // d4, seq // d16
    assert seq % tm == 0 and tm % (d16 * 16) == 0
    hg = HEADS_PER_GROUP

    in_specs = [
        pl.BlockSpec((1, 1, 3 * D_MODEL), lambda b, i: (b, 0, 0)),
        pl.BlockSpec((1, tm, D_MODEL), lambda b, i: (b, i, 0)),
        pl.BlockSpec((D_MODEL, COL_MATTN), lambda b, i: (0, 0), pipeline_mode=pl.Buffered(1)),
        pl.BlockSpec((CONV_K, CONV_WIDTH), lambda b, i: (0, 0)),
        pl.BlockSpec((1, CONV_WIDTH), lambda b, i: (0, 0)),
    ]
    out_shape = [
        jax.ShapeDtypeStruct((bsz, 3, hg, seq, HEAD_DIM), BF16),
        jax.ShapeDtypeStruct((bsz, 3, hg, d4, sub4, HEAD_DIM), BF16),
        jax.ShapeDtypeStruct((bsz, 3, hg, d16, sub16, HEAD_DIM), BF16),
        jax.ShapeDtypeStruct((bsz, hg, seq, HEAD_DIM), BF16),
        jax.ShapeDtypeStruct((bsz, seq, CONV_WIDTH), BF16),
    ]
    out_specs = [
        pl.BlockSpec((1, 3, hg, tm, HEAD_DIM), lambda b, i: (b, 0, 0, i, 0)),
        pl.BlockSpec((1, 3, hg, d4, tm // d4, HEAD_DIM), lambda b, i: (b, 0, 0, 0, i, 0)),
        pl.BlockSpec((1, 3, hg, d16, tm // d16, HEAD_DIM), lambda b, i: (b, 0, 0, 0, i, 0)),
        pl.BlockSpec((1, hg, tm, HEAD_DIM), lambda b, i: (b, 0, i, 0)),
        pl.BlockSpec((1, tm, CONV_WIDTH), lambda b, i: (b, i, 0)),
    ]
    weights = D_MODEL * COL_MATTN * 2
    x_tiles = tm * D_MODEL * 4 * 2
    out_tiles = (3 * 3 * ATTN_WIDTH + ATTN_WIDTH + CONV_WIDTH) * tm * 2 * 2
    scratch = 2 * tm * D_MODEL * 4 + N_GROUPS * tm * D_MODEL * 2 + (tm + SUBLANES) * CONV_WIDTH * 4
    return pl.pallas_call(
        functools.partial(_inproj_kernel, tm=tm),
        grid=(bsz, n_t),
        in_specs=in_specs,
        out_specs=out_specs,
        out_shape=out_shape,
        scratch_shapes=[pltpu.VMEM((D_MODEL // HEAD_DIM, tm, HEAD_DIM), F32),
                        pltpu.VMEM((D_MODEL // HEAD_DIM, tm, HEAD_DIM), F32),
                        pltpu.VMEM((N_GROUPS, tm, D_MODEL), BF16),
                        pltpu.VMEM((tm + SUBLANES, CONV_WIDTH), F32)],
        compiler_params=pltpu.CompilerParams(
            dimension_semantics=("arbitrary", "arbitrary"),
            vmem_limit_bytes=_vmem_limit(weights + x_tiles + out_tiles + scratch)),
        name="in_proj",
    )(mod, x, w_in, conv_w, conv_b)


def _nt_dot(q, k):
    return lax.dot_general(q, k, (((1,), (1,)), ((), ())), preferred_element_type=F32)


def _softmax_pv(s, v):
    m = jnp.max(s, axis=1, keepdims=True)
    p = jnp.exp2(s - m)
    v_ext = jnp.concatenate([v, jnp.ones((v.shape[0], HEAD_DIM), BF16)], axis=1)
    acc = jnp.dot(p.astype(BF16), v_ext, preferred_element_type=F32)
    return acc[:, :HEAD_DIM], m, acc[:, HEAD_DIM:]


def _attn_kernel(qkv1_ref, qkv2_ref, qkv3_ref, bias_ref, sg_ref, o_ref, *row_refs, seq):
    res1, res2, res3 = (row_refs[3 * g:3 * g + 3] for g in range(N_GROUPS))
    tok_ref = row_refs[3 * N_GROUPS]
    step = pl.program_id(0)
    par = jnp.bitwise_and(step, 1)
    opar = 1 - par

    @pl.when(step == 0)
    def _():
        for r in res1 + res2 + res3:
            r[1] = jnp.ones((seq, HEAD_DIM), F32)

    d4, d16 = DILATED_GROUPS[1][1], DILATED_GROUPS[2][1]
    n_blk = seq // BLOCK
    blk4 = seq // d4 // BLOCK
    assert seq // d16 == BLOCK and d16 == d4 * d4 and d4 == blk4
    sub4 = seq // d4

    def plan(blk):
        f1, nb, r4 = int(blk == 0), blk % blk4, blk // blk4
        f2 = int(nb == 0)
        return dict(q=pl.ds(blk * BLOCK, BLOCK), f1=f1, f2=f2,
                    k1=pl.ds((blk - 1 + f1) * BLOCK, 2 * BLOCK), k2=pl.ds((blk - 1 + f2) * BLOCK, 2 * BLOCK),
                    out3=pl.ds(nb * sub4 + r4, BLOCK, stride=d4))

    def score_block(blk, g):
        p = plan(blk)
        if g == 0:
            return _nt_dot(qkv1_ref[0, 0, 0, p["q"], :], qkv1_ref[0, 1, 0, p["k1"], :]) + bias_ref[0, 0, p["f1"]]
        if g == 1:
            return _nt_dot(qkv2_ref[0, 0, 0, p["q"], :], qkv2_ref[0, 1, 0, p["k2"], :]) + bias_ref[1, 0, p["f2"]]
        return _nt_dot(qkv3_ref[0, 0, 0, p["q"], :], qkv3_ref[0, 1, 0, p["q"], :]) + bias_ref[2, 0, 1, :, 0:BLOCK]

    def put(dst, idx, acc, m, l):
        dst[0][par, idx, :] = acc
        dst[1][par, idx, :] = jnp.broadcast_to(m, (BLOCK, HEAD_DIM))
        dst[2][par, idx, :] = l

    def finish_block(blk, g, s):
        p = plan(blk)
        if g == 0:
            put(res1, p["q"], *_softmax_pv(s, qkv1_ref[0, 2, 0, p["k1"], :]))
        elif g == 1:
            put(res2, p["q"], *_softmax_pv(s, qkv2_ref[0, 2, 0, p["k2"], :]))
        else:
            put(res3, p["out3"], *_softmax_pv(s, qkv3_ref[0, 2, 0, p["q"], :]))

    def merge(base, n_rows):
        for r0 in range(base, base + n_rows, MERGE_ROWS):
            c, p0 = r0 // sub4, r0 % sub4
            tok = pl.ds(d4 * p0 + c, MERGE_ROWS, stride=d4)
            own = pl.ds(r0, MERGE_ROWS)
            srcs = [(res1, tok), (res2, own), (res3, own)]
            ms = [r[1][opar, idx, :] for r, idx in srcs]
            m = jnp.maximum(jnp.maximum(ms[0], ms[1]), ms[2])
            es = [jnp.exp2(mg - m) for mg in ms]
            num = sum(e * r[0][opar, idx, :] for e, (r, idx) in zip(es, srcs))
            den = sum(e * r[2][opar, idx, :] for e, (r, idx) in zip(es, srcs))
            tok_ref[tok, :] = num * (1.0 / den)

    def emit_output(base, n_rows):
        rws = pl.ds(base, n_rows)
        o_ref[0, 0, rws, :] = tok_ref[rws, :].astype(BF16) * sg_ref[0, 0, rws, :]

    blocks = [(blk, g) for blk in range(n_blk) for g in range(N_GROUPS)]
    n_b = len(blocks)
    cuts = [MERGE_ROWS * ((seq // MERGE_ROWS) * k // n_b) for k in range(n_b + 1)]
    last = (d4 - 1) * sub4
    pending = {k: score_block(*blocks[k]) for k in range(min(ATTN_LAG, n_b))}
    for k in range(n_b):
        merge(cuts[k], cuts[k + 1] - cuts[k])
        lo, hi = max(cuts[k], last), cuts[k + 1]
        if hi > lo:
            emit_output(d4 * (lo - last), d4 * (hi - lo))
        if k + ATTN_LAG < n_b:
            pending[k + ATTN_LAG] = score_block(*blocks[k + ATTN_LAG])
        finish_block(*blocks[k], pending.pop(k))


def _attn_call(qkv1, qkv2, qkv3, bias, sg):
    bsz, _, hg, seq, _ = qkv1.shape
    qkv2 = qkv2.reshape(qkv1.shape)
    qkv3 = qkv3.reshape(qkv1.shape)
    bias = bias.reshape(N_GROUPS, hg, 2, BLOCK, 2 * BLOCK)
    n_slots = bsz * hg

    def cur(s):
        c = jnp.minimum(s, n_slots - 1)
        return c // hg, c % hg

    def prev(s):
        p = jnp.maximum(s - 1, 0)
        return p // hg, p % hg

    def qkv_map(s):
        b, j = cur(s)
        return (b, 0, j, 0, 0)

    def bias_map(s):
        return (0, cur(s)[1], 0, 0, 0)

    def lag_map(s):
        b, j = prev(s)
        return (b, j, 0, 0)

    in_tiles = (3 * 3 + 1) * seq * HEAD_DIM * 2 * 2 + N_GROUPS * 2 * BLOCK * 2 * BLOCK * 4 * 2
    out_tiles = seq * HEAD_DIM * 2 * 2
    row_shapes = [(2, seq, HEAD_DIM)] * (3 * N_GROUPS) + [(seq, HEAD_DIM)]
    scratch = sum(math.prod(sh) * 4 for sh in row_shapes)
    qkv_spec = pl.BlockSpec((1, 3, 1, seq, HEAD_DIM), qkv_map)
    return pl.pallas_call(
        functools.partial(_attn_kernel, seq=seq),
        grid=(n_slots + 1,),
        in_specs=[
            qkv_spec, qkv_spec, qkv_spec,
            pl.BlockSpec((N_GROUPS, 1, 2, BLOCK, 2 * BLOCK), bias_map),
            pl.BlockSpec((1, 1, seq, HEAD_DIM), lag_map),
        ],
        out_specs=pl.BlockSpec((1, 1, seq, HEAD_DIM), lag_map),
        out_shape=jax.ShapeDtypeStruct((bsz, hg, seq, HEAD_DIM), BF16),
        scratch_shapes=[pltpu.VMEM(sh, F32) for sh in row_shapes],
        compiler_params=pltpu.CompilerParams(
            dimension_semantics=("arbitrary",),
            vmem_limit_bytes=_vmem_limit(in_tiles + out_tiles + scratch)),
        name="dilated_attn",
    )(qkv1, qkv2, qkv3, bias, sg)


def _outproj_kernel(mod_ref, o_ref, spre_ref, x_ref, wma_ref, wmc_ref, wco_ref, wao_ref, wo_ref,
                    lng_ref, lnb_ref, out_ref, *, alpha, tm):
    shift = mod_ref[0, :, 0:D_MODEL]
    scale1 = 1.0 + mod_ref[0, :, D_MODEL:2 * D_MODEL]
    gate1 = 1.0 + mod_ref[0, :, 2 * D_MODEL:3 * D_MODEL]

    def branches(rws):
        h = (x_ref[0, rws, :] * scale1 + shift).astype(BF16)
        g_attn = _sigmoid(jnp.dot(h, wma_ref[...], preferred_element_type=F32))
        g_conv = _sigmoid(jnp.dot(h, wmc_ref[...], preferred_element_type=F32))
        a_in = jnp.concatenate([o_ref[0, j, rws, :] for j in range(HEADS_PER_GROUP)], axis=1)
        a_out = jnp.dot(a_in, wao_ref[...], preferred_element_type=F32)
        s_out = jnp.dot(spre_ref[0, rws, :], wco_ref[...], preferred_element_type=F32)
        return (g_attn * a_out + g_conv * s_out).astype(BF16)

    def norm(rws, y):
        v = alpha * x_ref[0, rws, :] + gate1 * y
        mu = jnp.mean(v, axis=-1, keepdims=True)
        dv = v - mu
        var = jnp.mean(dv * dv, axis=-1, keepdims=True)
        out_ref[0, rws, :] = dv * lax.rsqrt(var + LN_EPS) * lng_ref[...] + lnb_ref[...]

    chunks = [slice(r0, r0 + OUTPROJ_ROWS) for r0 in range(0, tm, OUTPROJ_ROWS)]
    merged = branches(chunks[0])
    for c, rws in enumerate(chunks):
        nxt = branches(chunks[c + 1]) if c + 1 < len(chunks) else None
        y = jnp.dot(merged, wo_ref[...], preferred_element_type=F32)
        norm(rws, y)
        merged = nxt


def _outproj_call(mod, o, spre, x, w_in, w_conv_out, w_attn_out, w_o, ln_g, ln_b, alpha):
    bsz, seq, _ = x.shape
    tm = OUTPROJ_TM
    assert seq % tm == 0 and tm % OUTPROJ_ROWS == 0
    assert COL_MATTN % D_MODEL == 0 and COL_MCONV % D_MODEL == 0
    hg = HEADS_PER_GROUP
    tok = lambda b, i: (b, i, 0)
    const = lambda b, i: (0, 0)
    resident = dict(pipeline_mode=pl.Buffered(1))
    tiles = (ATTN_WIDTH * 2 + CONV_WIDTH * 2 + 2 * D_MODEL * 4) * tm * 2
    weights = (2 * D_MODEL + CONV_WIDTH + ATTN_WIDTH + D_MODEL) * D_MODEL * 2
    temps = 8 * OUTPROJ_ROWS * D_MODEL * 4
    return pl.pallas_call(
        functools.partial(_outproj_kernel, alpha=alpha, tm=tm),
        grid=(bsz, seq // tm),
        in_specs=[
            pl.BlockSpec((1, 1, 3 * D_MODEL), lambda b, i: (b, 0, 0)),
            pl.BlockSpec((1, hg, tm, HEAD_DIM), lambda b, i: (b, 0, i, 0)),
            pl.BlockSpec((1, tm, CONV_WIDTH), tok),
            pl.BlockSpec((1, tm, D_MODEL), tok),
            pl.BlockSpec((D_MODEL, D_MODEL), lambda b, i: (0, COL_MATTN // D_MODEL), **resident),
            pl.BlockSpec((D_MODEL, D_MODEL), lambda b, i: (0, COL_MCONV // D_MODEL), **resident),
            pl.BlockSpec((CONV_WIDTH, D_MODEL), const, **resident),
            pl.BlockSpec((ATTN_WIDTH, D_MODEL), const, **resident),
            pl.BlockSpec((D_MODEL, D_MODEL), const, **resident),
            pl.BlockSpec((1, D_MODEL), const),
            pl.BlockSpec((1, D_MODEL), const),
        ],
        out_specs=pl.BlockSpec((1, tm, D_MODEL), tok),
        out_shape=jax.ShapeDtypeStruct((bsz, seq, D_MODEL), F32),
        compiler_params=pltpu.CompilerParams(
            dimension_semantics=("arbitrary", "arbitrary"),
            vmem_limit_bytes=_vmem_limit(tiles + weights + temps)),
        name="out_proj",
    )(mod, o, spre, x, w_in, w_in, w_conv_out, w_attn_out, w_o, ln_g, ln_b)


def kernel(x, c, w_ada, b_ada, w_in, conv_w, conv_b, rel_bias, w_attn_out, w_conv_out, w_o, ln_g, ln_b):
    bsz = x.shape[0]
    depth = w_in.shape[0]
    alpha = (2.0 * depth) ** 0.25
    bias = _bias_call(rel_bias)
    for layer in range(depth):
        mod = _ada_call(c, w_ada[layer], b_ada[layer]).reshape(bsz, 1, 3 * D_MODEL)
        w_in_b = w_in[layer].astype(BF16)
        qkv1, qkv2, qkv3, sg, spre = _inproj_call(mod, x, w_in_b, conv_w[layer], conv_b[layer][None, :])
        o = _attn_call(qkv1, qkv2, qkv3, bias, sg)
        x = _outproj_call(mod, o, spre, x, w_in_b, w_conv_out[layer].astype(BF16),
                          w_attn_out[layer].astype(BF16), w_o[layer].astype(BF16),
                          ln_g[layer][None, :], ln_b[layer][None, :], alpha)
    return x
```

```python
import functools
import math

import numpy as np
import jax
import jax.numpy as jnp
from jax import lax
from jax.experimental import pallas as pl
from jax.experimental.pallas import tpu as pltpu

F32 = jnp.float32
BF16 = jnp.bfloat16

D_MODEL = 1024
HEAD_DIM = 128
HEADS_PER_GROUP = 4
DILATED_GROUPS = ((128, 1), (512, 4), (2048, 16))
N_GROUPS = len(DILATED_GROUPS)
N_ATTN_HEADS = N_GROUPS * HEADS_PER_GROUP
QKV_WIDTH = N_ATTN_HEADS * HEAD_DIM
ATTN_WIDTH = HEADS_PER_GROUP * HEAD_DIM
CONV_WIDTH = D_MODEL
CONV_K = 3
N_BUCKETS = 32
MAX_EXACT = 16
MAX_DISTANCE = 2048
BLOCK = 128
LN_EPS = 1e-5
NEG_INF = -1e30
LOG2E = math.log2(math.e)
Q_SCALE = HEAD_DIM ** -0.5 * LOG2E

COL_Q = 0
COL_K = QKV_WIDTH
COL_V = 2 * QKV_WIDTH
COL_GATTN = 3 * QKV_WIDTH
COL_U = COL_GATTN + ATTN_WIDTH
COL_B = COL_U + CONV_WIDTH
COL_C = COL_B + CONV_WIDTH
COL_GCONV = COL_C + CONV_WIDTH
COL_MATTN = COL_GCONV + CONV_WIDTH
COL_MCONV = COL_MATTN + D_MODEL
N_COLS = COL_MCONV + D_MODEL

V7X_VMEM_BYTES = 64 * 1024 * 1024
SUBLANES = 8
INPROJ_TM = 512
OUTPROJ_TM = 1024
OUTPROJ_ROWS = 256
CONV_CHUNK = 256
ATTN_LAG = 2
ATTN_GROUP_ORDER = (2, 0, 1)
MERGE_ROWS = 32


def _vmem_limit(estimate_bytes):
    return int(min(V7X_VMEM_BYTES - (4 << 20), estimate_bytes * 5 // 4 + (4 << 20)))


def _bucket_table():
    a = np.arange(BLOCK)[:, None]
    b = np.arange(2 * BLOCK)[None, :]
    steps = a + BLOCK - b
    out = []
    for window, dilation in DILATED_GROUPS:
        n_steps = window // dilation
        valid = (steps >= 0) & (steps <= n_steps)
        dist = np.maximum(steps, 0) * dilation
        n = np.maximum(dist, 1).astype(np.float32)
        large = MAX_EXACT + (np.log(n / np.float32(MAX_EXACT)) / np.float32(math.log(MAX_DISTANCE / MAX_EXACT))
                             * np.float32(N_BUCKETS - MAX_EXACT)).astype(np.int32)
        large = np.minimum(large, N_BUCKETS - 1)
        bucket = np.where(dist < MAX_EXACT, dist, large)
        out.append(np.where(valid, bucket, -1).astype(np.int32))
    return np.stack(out)


_BUCKETS = _bucket_table()


def _sigmoid(v):
    return 0.5 * jnp.tanh(0.5 * v) + 0.5


def _ada_kernel(c_ref, w_ref, b_ref, o_ref):
    c = c_ref[...]
    o_ref[...] = jnp.dot(c * _sigmoid(c), w_ref[...], preferred_element_type=F32) + b_ref[...]


def _ada_call(c, w_ada, b_ada):
    bsz = c.shape[0]
    return pl.pallas_call(
        _ada_kernel,
        out_shape=jax.ShapeDtypeStruct((bsz, 3 * D_MODEL), F32),
        name="ada_mod",
    )(c, w_ada, b_ada[None, :])


def _bias_kernel(rb_ref, bucket_ref, o_ref):
    for g in range(N_GROUPS):
        bk = bucket_ref[g]
        for j in range(HEADS_PER_GROUP):
            h = g * HEADS_PER_GROUP + j
            acc = jnp.full(bk.shape, NEG_INF, F32)
            for k in range(N_BUCKETS):
                acc = jnp.where(bk == k, rb_ref[k, h] * LOG2E, acc)
            o_ref[h, 0] = acc
            o_ref[h, 1] = jnp.concatenate(
                [acc[:, BLOCK:], jnp.full((BLOCK, BLOCK), NEG_INF, F32)], axis=1)


def _bias_call(rel_bias):
    return pl.pallas_call(
        _bias_kernel,
        out_shape=jax.ShapeDtypeStruct((N_ATTN_HEADS, 2, BLOCK, 2 * BLOCK), F32),
        in_specs=[pl.BlockSpec(memory_space=pltpu.SMEM), pl.BlockSpec(memory_space=pltpu.VMEM)],
        out_specs=pl.BlockSpec(memory_space=pltpu.VMEM),
        name="bias_tables",
    )(rel_bias, jnp.asarray(_BUCKETS))


def _inproj_kernel(mod_ref, x_ref, w_ref, cw_ref, cb_ref,
                   qkv1_ref, qkv2_ref, qkv3_ref, sg_ref, spre_ref,
                   hs_ref, hs4_ref, hb_ref, zbuf_ref, *, tm):
    i = pl.program_id(1)

    @pl.when(i == 0)
    def _():
        zbuf_ref[0:SUBLANES, :] = jnp.zeros((SUBLANES, CONV_WIDTH), F32)

    n_slab = D_MODEL // HEAD_DIM
    for s in range(n_slab):
        ls = slice(s * HEAD_DIM, (s + 1) * HEAD_DIM)
        h = x_ref[0, :, ls] * (1.0 + mod_ref[0, :, D_MODEL + s * HEAD_DIM:D_MODEL + (s + 1) * HEAD_DIM]) \
            + mod_ref[0, :, ls]
        hb_ref[0, :, ls] = h.astype(BF16)
        hs_ref[s] = h

    d4, d16 = DILATED_GROUPS[1][1], DILATED_GROUPS[2][1]
    assert d16 == d4 * d4
    n4, n16 = tm // d4, tm // d16
    for s in range(n_slab):
        ls = slice(s * HEAD_DIM, (s + 1) * HEAD_DIM)
        for r in range(d4):
            part = hs_ref[s, pl.ds(r, n4, stride=d4), :]
            hs4_ref[s, r * n4:(r + 1) * n4, :] = part
            hb_ref[1, r * n4:(r + 1) * n4, ls] = part.astype(BF16)
    for s in range(n_slab):
        ls = slice(s * HEAD_DIM, (s + 1) * HEAD_DIM)
        for r16 in range(d16):
            part = hs4_ref[s, pl.ds((r16 % d4) * n4 + r16 // d4, n16, stride=d4), :]
            hb_ref[2, r16 * n16:(r16 + 1) * n16, ls] = part.astype(BF16)

    def proj(g, c0, width):
        return jnp.dot(hb_ref[g], w_ref[:, c0:c0 + width], preferred_element_type=F32)

    def qkv(g, store):
        for t, col in enumerate((COL_Q, COL_K, COL_V)):
            r = proj(g, col + g * ATTN_WIDTH, ATTN_WIDTH)
            if t == 0:
                r = r * Q_SCALE
            r = r.astype(BF16)
            for j in range(HEADS_PER_GROUP):
                store(t, j, r[:, j * HEAD_DIM:(j + 1) * HEAD_DIM])

    def store1(t, j, v):
        qkv1_ref[0, t, j] = v
    qkv(0, store1)

    ga = proj(0, COL_GATTN, ATTN_WIDTH)
    sgv = (ga * _sigmoid(ga)).astype(BF16)
    for j in range(HEADS_PER_GROUP):
        sg_ref[0, j] = sgv[:, j * HEAD_DIM:(j + 1) * HEAD_DIM]

    for c0 in range(0, CONV_WIDTH, CONV_CHUNK):
        cs = slice(c0, c0 + CONV_CHUNK)
        u = proj(0, COL_U + c0, CONV_CHUNK)
        cg = proj(0, COL_C + c0, CONV_CHUNK)
        z = cg * u
        zbuf_ref[SUBLANES:SUBLANES + tm, cs] = z
        zm1 = zbuf_ref[SUBLANES - 1:SUBLANES - 1 + tm, cs]
        zm2 = zbuf_ref[SUBLANES - 2:SUBLANES - 2 + tm, cs]
        y = cw_ref[0:1, cs] * zm2 + cw_ref[1:2, cs] * zm1 + cw_ref[2:3, cs] * z + cb_ref[:, cs]
        bg = proj(0, COL_B + c0, CONV_CHUNK)
        gc = proj(0, COL_GCONV + c0, CONV_CHUNK)
        spre_ref[0, :, cs] = (bg * y * (gc * _sigmoid(gc))).astype(BF16)
    zbuf_ref[0:SUBLANES, :] = zbuf_ref[tm:tm + SUBLANES, :]

    def store_sub(ref, d):
        def store(t, j, v):
            rows = tm // d
            for r in range(d):
                ref[0, t, j, r] = v[r * rows:(r + 1) * rows]
        return store
    qkv(1, store_sub(qkv2_ref, d4))
    qkv(2, store_sub(qkv3_ref, d16))


def _inproj_call(mod, x, w_in, conv_w, conv_b):
    bsz, seq, _ = x.shape
    tm = INPROJ_TM
    n_t = seq // tm
    d4, d16 = DILATED_GROUPS[1][1], DILATED_GROUPS[2][1]
    sub4, sub16 = seq // d4, seq // d16
    assert seq % tm == 0 and tm % (d16 * 16) == 0
    hg = HEADS_PER_GROUP

    in_specs = [
        pl.BlockSpec((1, 1, 3 * D_MODEL), lambda b, i: (b, 0, 0)),
        pl.BlockSpec((1, tm, D_MODEL), lambda b, i: (b, i, 0)),
        pl.BlockSpec((D_MODEL, COL_MATTN), lambda b, i: (0, 0), pipeline_mode=pl.Buffered(1)),
        pl.BlockSpec((CONV_K, CONV_WIDTH), lambda b, i: (0, 0)),
        pl.BlockSpec((1, CONV_WIDTH), lambda b, i: (0, 0)),
    ]
    out_shape = [
        jax.ShapeDtypeStruct((bsz, 3, hg, seq, HEAD_DIM), BF16),
        jax.ShapeDtypeStruct((bsz, 3, hg, d4, sub4, HEAD_DIM), BF16),
        jax.ShapeDtypeStruct((bsz, 3, hg, d16, sub16, HEAD_DIM), BF16),
        jax.ShapeDtypeStruct((bsz, hg, seq, HEAD_DIM), BF16),
        jax.ShapeDtypeStruct((bsz, seq, CONV_WIDTH), BF16),
    ]
    out_specs = [
        pl.BlockSpec((1, 3, hg, tm, HEAD_DIM), lambda b, i: (b, 0, 0, i, 0)),
        pl.BlockSpec((1, 3, hg, d4, tm // d4, HEAD_DIM), lambda b, i: (b, 0, 0, 0, i, 0)),
        pl.BlockSpec((1, 3, hg, d16, tm // d16, HEAD_DIM), lambda b, i: (b, 0, 0, 0, i, 0)),
        pl.BlockSpec((1, hg, tm, HEAD_DIM), lambda b, i: (b, 0, i, 0)),
        pl.BlockSpec((1, tm, CONV_WIDTH), lambda b, i: (b, i, 0)),
    ]
    weights = D_MODEL * COL_MATTN * 2
    x_tiles = tm * D_MODEL * 4 * 2
    out_tiles = (3 * 3 * ATTN_WIDTH + ATTN_WIDTH + CONV_WIDTH) * tm * 2 * 2
    scratch = 2 * tm * D_MODEL * 4 + N_GROUPS * tm * D_MODEL * 2 + (tm + SUBLANES) * CONV_WIDTH * 4
    return pl.pallas_call(
        functools.partial(_inproj_kernel, tm=tm),
        grid=(bsz, n_t),
        in_specs=in_specs,
        out_specs=out_specs,
        out_shape=out_shape,
        scratch_shapes=[pltpu.VMEM((D_MODEL // HEAD_DIM, tm, HEAD_DIM), F32),
                        pltpu.VMEM((D_MODEL // HEAD_DIM, tm, HEAD_DIM), F32),
                        pltpu.VMEM((N_GROUPS, tm, D_MODEL), BF16),
                        pltpu.VMEM((tm + SUBLANES, CONV_WIDTH), F32)],
        compiler_params=pltpu.CompilerParams(
            dimension_semantics=("arbitrary", "arbitrary"),
            vmem_limit_bytes=_vmem_limit(weights + x_tiles + out_tiles + scratch)),
        name="in_proj",
    )(mod, x, w_in, conv_w, conv_b)


def _nt_dot(q, k):
    return lax.dot_general(q, k, (((1,), (1,)), ((), ())), preferred_element_type=F32)


def _softmax_pv(s, v):
    m = jnp.max(s, axis=1, keepdims=True)
    p = jnp.exp2(s - m)
    v_ext = jnp.concatenate([v, jnp.ones((v.shape[0], HEAD_DIM), BF16)], axis=1)
    acc = jnp.dot(p.astype(BF16), v_ext, preferred_element_type=F32)
    return acc[:, :HEAD_DIM], m, acc[:, HEAD_DIM:]


def _attn_kernel(qkv1_ref, qkv2_ref, qkv3_ref, bias_ref, sg_ref, o_ref, *row_refs, seq):
    res1, res2, res3 = (row_refs[3 * g:3 * g + 3] for g in range(N_GROUPS))
    tok_ref = row_refs[3 * N_GROUPS]
    step = pl.program_id(0)
    par = jnp.bitwise_and(step, 1)
    opar = 1 - par

    @pl.when(step == 0)
    def _():
        for r in res1 + res2 + res3:
            r[1] = jnp.ones((seq, HEAD_DIM), F32)

    d4, d16 = DILATED_GROUPS[1][1], DILATED_GROUPS[2][1]
    n_blk = seq // BLOCK
    blk4 = seq // d4 // BLOCK
    assert seq // d16 == BLOCK and d16 == d4 * d4 and d4 == blk4
    sub4 = seq // d4

    def plan(blk):
        f1, nb, r4 = int(blk == 0), blk % blk4, blk // blk4
        f2 = int(nb == 0)
        return dict(q=pl.ds(blk * BLOCK, BLOCK), f1=f1, f2=f2,
                    k1=pl.ds((blk - 1 + f1) * BLOCK, 2 * BLOCK), k2=pl.ds((blk - 1 + f2) * BLOCK, 2 * BLOCK),
                    out3=pl.ds(nb * sub4 + r4, BLOCK, stride=d4))

    def score_block(blk, g):
        p = plan(blk)
        if g == 0:
            return _nt_dot(qkv1_ref[0, 0, 0, p["q"], :], qkv1_ref[0, 1, 0, p["k1"], :]) + bias_ref[0, 0, p["f1"]]
        if g == 1:
            return _nt_dot(qkv2_ref[0, 0, 0, p["q"], :], qkv2_ref[0, 1, 0, p["k2"], :]) + bias_ref[1, 0, p["f2"]]
        return _nt_dot(qkv3_ref[0, 0, 0, p["q"], :], qkv3_ref[0, 1, 0, p["q"], :]) + bias_ref[2, 0, 1, :, 0:BLOCK]

    def put(dst, idx, acc, m, l):
        dst[0][par, idx, :] = acc
        dst[1][par, idx, :] = jnp.broadcast_to(m, (BLOCK, HEAD_DIM))
        dst[2][par, idx, :] = l

    def finish_block(blk, g, s):
        p = plan(blk)
        if g == 0:
            put(res1, p["q"], *_softmax_pv(s, qkv1_ref[0, 2, 0, p["k1"], :]))
        elif g == 1:
            put(res2, p["q"], *_softmax_pv(s, qkv2_ref[0, 2, 0, p["k2"], :]))
        else:
            put(res3, p["out3"], *_softmax_pv(s, qkv3_ref[0, 2, 0, p["q"], :]))

    def merge(base, n_rows):
        for r0 in range(base, base + n_rows, MERGE_ROWS):
            c, p0 = r0 // sub4, r0 % sub4
            tok = pl.ds(d4 * p0 + c, MERGE_ROWS, stride=d4)
            own = pl.ds(r0, MERGE_ROWS)
            srcs = [(res1, tok), (res2, own), (res3, own)]
            ms = [r[1][opar, idx, :] for r, idx in srcs]
            m = jnp.maximum(jnp.maximum(ms[0], ms[1]), ms[2])
            es = [jnp.exp2(mg - m) for mg in ms]
            num = sum(e * r[0][opar, idx, :] for e, (r, idx) in zip(es, srcs))
            den = sum(e * r[2][opar, idx, :] for e, (r, idx) in zip(es, srcs))
            tok_ref[tok, :] = num * (1.0 / den)

    def emit_output(base, n_rows):
        rws = pl.ds(base, n_rows)
        o_ref[0, 0, rws, :] = tok_ref[rws, :].astype(BF16) * sg_ref[0, 0, rws, :]

    blocks = [(blk, g) for blk in range(n_blk) for g in ATTN_GROUP_ORDER]
    n_b = len(blocks)
    cuts = [MERGE_ROWS * ((seq // MERGE_ROWS) * k // n_b) for k in range(n_b + 1)]
    last = (d4 - 1) * sub4
    pending = {k: score_block(*blocks[k]) for k in range(min(ATTN_LAG, n_b))}
    for k in range(n_b):
        merge(cuts[k], cuts[k + 1] - cuts[k])
        lo, hi = max(cuts[k], last), cuts[k + 1]
        if hi > lo:
            emit_output(d4 * (lo - last), d4 * (hi - lo))
        if k + ATTN_LAG < n_b:
            pending[k + ATTN_LAG] = score_block(*blocks[k + ATTN_LAG])
        finish_block(*blocks[k], pending.pop(k))


def _attn_call(qkv1, qkv2, qkv3, bias, sg):
    bsz, _, hg, seq, _ = qkv1.shape
    qkv2 = qkv2.reshape(qkv1.shape)
    qkv3 = qkv3.reshape(qkv1.shape)
    bias = bias.reshape(N_GROUPS, hg, 2, BLOCK, 2 * BLOCK)
    n_slots = bsz * hg

    def cur(s):
        c = jnp.minimum(s, n_slots - 1)
        return c // hg, c % hg

    def prev(s):
        p = jnp.maximum(s - 1, 0)
        return p // hg, p % hg

    def qkv_map(s):
        b, j = cur(s)
        return (b, 0, j, 0, 0)

    def bias_map(s):
        return (0, cur(s)[1], 0, 0, 0)

    def lag_map(s):
        b, j = prev(s)
        return (b, j, 0, 0)

    in_tiles = (3 * 3 + 1) * seq * HEAD_DIM * 2 * 2 + N_GROUPS * 2 * BLOCK * 2 * BLOCK * 4 * 2
    out_tiles = seq * HEAD_DIM * 2 * 2
    row_shapes = [(2, seq, HEAD_DIM)] * (3 * N_GROUPS) + [(seq, HEAD_DIM)]
    scratch = sum(math.prod(sh) * 4 for sh in row_shapes)
    qkv_spec = pl.BlockSpec((1, 3, 1, seq, HEAD_DIM), qkv_map)
    return pl.pallas_call(
        functools.partial(_attn_kernel, seq=seq),
        grid=(n_slots + 1,),
        in_specs=[
            qkv_spec, qkv_spec, qkv_spec,
            pl.BlockSpec((N_GROUPS, 1, 2, BLOCK, 2 * BLOCK), bias_map),
            pl.BlockSpec((1, 1, seq, HEAD_DIM), lag_map),
        ],
        out_specs=pl.BlockSpec((1, 1, seq, HEAD_DIM), lag_map),
        out_shape=jax.ShapeDtypeStruct((bsz, hg, seq, HEAD_DIM), BF16),
        scratch_shapes=[pltpu.VMEM(sh, F32) for sh in row_shapes],
        compiler_params=pltpu.CompilerParams(
            dimension_semantics=("arbitrary",),
            vmem_limit_bytes=_vmem_limit(in_tiles + out_tiles + scratch)),
        name="dilated_attn",
    )(qkv1, qkv2, qkv3, bias, sg)


def _outproj_kernel(mod_ref, o_ref, spre_ref, x_ref, wma_ref, wmc_ref, wco_ref, wao_ref, wo_ref,
                    lng_ref, lnb_ref, out_ref, *, alpha, tm):
    shift = mod_ref[0, :, 0:D_MODEL]
    scale1 = 1.0 + mod_ref[0, :, D_MODEL:2 * D_MODEL]
    gate1 = 1.0 + mod_ref[0, :, 2 * D_MODEL:3 * D_MODEL]

    def branches(rws):
        h = (x_ref[0, rws, :] * scale1 + shift).astype(BF16)
        g_attn = _sigmoid(jnp.dot(h, wma_ref[...], preferred_element_type=F32))
        g_conv = _sigmoid(jnp.dot(h, wmc_ref[...], preferred_element_type=F32))
        a_in = jnp.concatenate([o_ref[0, j, rws, :] for j in range(HEADS_PER_GROUP)], axis=1)
        a_out = jnp.dot(a_in, wao_ref[...], preferred_element_type=F32)
        s_out = jnp.dot(spre_ref[0, rws, :], wco_ref[...], preferred_element_type=F32)
        return (g_attn * a_out + g_conv * s_out).astype(BF16)

    def norm(rws, y):
        v = alpha * x_ref[0, rws, :] + gate1 * y
        mu = jnp.mean(v, axis=-1, keepdims=True)
        dv = v - mu
        var = jnp.mean(dv * dv, axis=-1, keepdims=True)
        out_ref[0, rws, :] = dv * lax.rsqrt(var + LN_EPS) * lng_ref[...] + lnb_ref[...]

    chunks = [slice(r0, r0 + OUTPROJ_ROWS) for r0 in range(0, tm, OUTPROJ_ROWS)]
    merged = branches(chunks[0])
    for c, rws in enumerate(chunks):
        nxt = branches(chunks[c + 1]) if c + 1 < len(chunks) else None
        y = jnp.dot(merged, wo_ref[...], preferred_element_type=F32)
        norm(rws, y)
        merged = nxt


def _outproj_call(mod, o, spre, x, w_in, w_conv_out, w_attn_out, w_o, ln_g, ln_b, alpha):
    bsz, seq, _ = x.shape
    tm = OUTPROJ_TM
    assert seq % tm == 0 and tm % OUTPROJ_ROWS == 0
    assert COL_MATTN % D_MODEL == 0 and COL_MCONV % D_MODEL == 0
    hg = HEADS_PER_GROUP
    tok = lambda b, i: (b, i, 0)
    const = lambda b, i: (0, 0)
    resident = dict(pipeline_mode=pl.Buffered(1))
    tiles = (ATTN_WIDTH * 2 + CONV_WIDTH * 2 + 2 * D_MODEL * 4) * tm * 2
    weights = (2 * D_MODEL + CONV_WIDTH + ATTN_WIDTH + D_MODEL) * D_MODEL * 2
    temps = 8 * OUTPROJ_ROWS * D_MODEL * 4
    return pl.pallas_call(
        functools.partial(_outproj_kernel, alpha=alpha, tm=tm),
        grid=(bsz, seq // tm),
        in_specs=[
            pl.BlockSpec((1, 1, 3 * D_MODEL), lambda b, i: (b, 0, 0)),
            pl.BlockSpec((1, hg, tm, HEAD_DIM), lambda b, i: (b, 0, i, 0)),
            pl.BlockSpec((1, tm, CONV_WIDTH), tok),
            pl.BlockSpec((1, tm, D_MODEL), tok),
            pl.BlockSpec((D_MODEL, D_MODEL), lambda b, i: (0, COL_MATTN // D_MODEL), **resident),
            pl.BlockSpec((D_MODEL, D_MODEL), lambda b, i: (0, COL_MCONV // D_MODEL), **resident),
            pl.BlockSpec((CONV_WIDTH, D_MODEL), const, **resident),
            pl.BlockSpec((ATTN_WIDTH, D_MODEL), const, **resident),
            pl.BlockSpec((D_MODEL, D_MODEL), const, **resident),
            pl.BlockSpec((1, D_MODEL), const),
            pl.BlockSpec((1, D_MODEL), const),
        ],
        out_specs=pl.BlockSpec((1, tm, D_MODEL), tok),
        out_shape=jax.ShapeDtypeStruct((bsz, seq, D_MODEL), F32),
        compiler_params=pltpu.CompilerParams(
            dimension_semantics=("arbitrary", "arbitrary"),
            vmem_limit_bytes=_vmem_limit(tiles + weights + temps)),
        name="out_proj",
    )(mod, o, spre, x, w_in, w_in, w_conv_out, w_attn_out, w_o, ln_g, ln_b)


def kernel(x, c, w_ada, b_ada, w_in, conv_w, conv_b, rel_bias, w_attn_out, w_conv_out, w_o, ln_g, ln_b):
    bsz = x.shape[0]
    depth = w_in.shape[0]
    alpha = (2.0 * depth) ** 0.25
    bias = _bias_call(rel_bias)
    for layer in range(depth):
        mod = _ada_call(c, w_ada[layer], b_ada[layer]).reshape(bsz, 1, 3 * D_MODEL)
        w_in_b = w_in[layer].astype(BF16)
        qkv1, qkv2, qkv3, sg, spre = _inproj_call(mod, x, w_in_b, conv_w[layer], conv_b[layer][None, :])
        o = _attn_call(qkv1, qkv2, qkv3, bias, sg)
        x = _outproj_call(mod, o, spre, x, w_in_b, w_conv_out[layer].astype(BF16),
                          w_attn_out[layer].astype(BF16), w_o[layer].astype(BF16),
                          ln_g[layer][None, :], ln_b[layer][None, :], alpha)
    return x
```

```python
import functools
import math

import numpy as np
import jax
import jax.numpy as jnp
from jax import lax
from jax.experimental import pallas as pl
from jax.experimental.pallas import tpu as pltpu

F32 = jnp.float32
BF16 = jnp.bfloat16

D_MODEL = 1024
HEAD_DIM = 128
HEADS_PER_GROUP = 4
DILATED_GROUPS = ((128, 1), (512, 4), (2048, 16))
N_GROUPS = len(DILATED_GROUPS)
N_ATTN_HEADS = N_GROUPS * HEADS_PER_GROUP
QKV_WIDTH = N_ATTN_HEADS * HEAD_DIM
ATTN_WIDTH = HEADS_PER_GROUP * HEAD_DIM
CONV_WIDTH = D_MODEL
CONV_K = 3
N_BUCKETS = 32
MAX_EXACT = 16
MAX_DISTANCE = 2048
BLOCK = 128
LN_EPS = 1e-5
NEG_INF = -1e30
LOG2E = math.log2(math.e)
Q_SCALE = HEAD_DIM ** -0.5 * LOG2E

COL_Q = 0
COL_K = QKV_WIDTH
COL_V = 2 * QKV_WIDTH
COL_GATTN = 3 * QKV_WIDTH
COL_U = COL_GATTN + ATTN_WIDTH
COL_B = COL_U + CONV_WIDTH
COL_C = COL_B + CONV_WIDTH
COL_GCONV = COL_C + CONV_WIDTH
COL_MATTN = COL_GCONV + CONV_WIDTH
COL_MCONV = COL_MATTN + D_MODEL
N_COLS = COL_MCONV + D_MODEL

V7X_VMEM_BYTES = 64 * 1024 * 1024
SUBLANES = 8
INPROJ_TM = 512
OUTPROJ_TM = 1024
OUTPROJ_ROWS = 256
CONV_CHUNK = 256
ATTN_LAG = 2
ATTN_GROUP_ORDER = (2, 0, 1)
MERGE_ROWS = 32


def _vmem_limit(estimate_bytes):
    return int(min(V7X_VMEM_BYTES - (4 << 20), estimate_bytes * 5 // 4 + (4 << 20)))


def _bucket_table():
    a = np.arange(BLOCK)[:, None]
    b = np.arange(2 * BLOCK)[None, :]
    steps = a + BLOCK - b
    out = []
    for window, dilation in DILATED_GROUPS:
        n_steps = window // dilation
        valid = (steps >= 0) & (steps <= n_steps)
        dist = np.maximum(steps, 0) * dilation
        n = np.maximum(dist, 1).astype(np.float32)
        large = MAX_EXACT + (np.log(n / np.float32(MAX_EXACT)) / np.float32(math.log(MAX_DISTANCE / MAX_EXACT))
                             * np.float32(N_BUCKETS - MAX_EXACT)).astype(np.int32)
        large = np.minimum(large, N_BUCKETS - 1)
        bucket = np.where(dist < MAX_EXACT, dist, large)
        out.append(np.where(valid, bucket, -1).astype(np.int32))
    return np.stack(out)


_BUCKETS = _bucket_table()
assert all((_BUCKETS[:, a, a:] == _BUCKETS[:, 0, :2 * BLOCK - a]).all() and (_BUCKETS[:, a, :a] == -1).all()
           for a in range(BLOCK))
_BUCKET_ROWS = np.broadcast_to(
    np.pad(_BUCKETS[:, 0:1, :], ((0, 0), (0, 0), (0, 2 * BLOCK)), constant_values=-1),
    (N_GROUPS, SUBLANES, 4 * BLOCK)).copy()


def _sigmoid(v):
    return 0.5 * jnp.tanh(0.5 * v) + 0.5


def _ada_kernel(c_ref, w_ref, b_ref, o_ref):
    c = c_ref[...]
    o_ref[...] = jnp.dot(c * _sigmoid(c), w_ref[...], preferred_element_type=F32) + b_ref[...]


def _ada_call(c, w_ada, b_ada):
    bsz = c.shape[0]
    return pl.pallas_call(
        _ada_kernel,
        out_shape=jax.ShapeDtypeStruct((bsz, 3 * D_MODEL), F32),
        name="ada_mod",
    )(c, w_ada, b_ada[None, :])


def _bias_kernel(rb_ref, bucket_ref, o_ref):
    for g in range(N_GROUPS):
        bk = bucket_ref[g]
        for j in range(HEADS_PER_GROUP):
            h = g * HEADS_PER_GROUP + j
            row = jnp.full(bk.shape, NEG_INF, F32)
            for k in range(N_BUCKETS):
                row = jnp.where(bk == k, rb_ref[k, h] * LOG2E, row)
            full = jnp.broadcast_to(row[0:1, :], (BLOCK, row.shape[1]))
            acc = pltpu.roll(full, 0, 1, stride=1, stride_axis=0)[:, :2 * BLOCK]
            o_ref[h, 0] = acc
            o_ref[h, 1] = jnp.concatenate(
                [acc[:, BLOCK:], jnp.full((BLOCK, BLOCK), NEG_INF, F32)], axis=1)


def _bias_call(rel_bias):
    return pl.pallas_call(
        _bias_kernel,
        out_shape=jax.ShapeDtypeStruct((N_ATTN_HEADS, 2, BLOCK, 2 * BLOCK), F32),
        in_specs=[pl.BlockSpec(memory_space=pltpu.SMEM), pl.BlockSpec(memory_space=pltpu.VMEM)],
        out_specs=pl.BlockSpec(memory_space=pltpu.VMEM),
        name="bias_tables",
    )(rel_bias, jnp.asarray(_BUCKET_ROWS))


def _inproj_kernel(mod_ref, x_ref, w_ref, cw_ref, cb_ref,
                   qkv1_ref, qkv2_ref, qkv3_ref, sg_ref, spre_ref,
                   hs_ref, hs4_ref, hb_ref, zbuf_ref, *, tm):
    i = pl.program_id(1)

    @pl.when(i == 0)
    def _():
        zbuf_ref[0:SUBLANES, :] = jnp.zeros((SUBLANES, CONV_WIDTH), F32)

    n_slab = D_MODEL // HEAD_DIM
    for s in range(n_slab):
        ls = slice(s * HEAD_DIM, (s + 1) * HEAD_DIM)
        h = x_ref[0, :, ls] * (1.0 + mod_ref[0, :, D_MODEL + s * HEAD_DIM:D_MODEL + (s + 1) * HEAD_DIM]) \
            + mod_ref[0, :, ls]
        hb_ref[0, :, ls] = h.astype(BF16)
        hs_ref[s] = h

    d4, d16 = DILATED_GROUPS[1][1], DILATED_GROUPS[2][1]
    assert d16 == d4 * d4
    n4, n16 = tm // d4, tm // d16
    for s in range(n_slab):
        ls = slice(s * HEAD_DIM, (s + 1) * HEAD_DIM)
        for r in range(d4):
            part = hs_ref[s, pl.ds(r, n4, stride=d4), :]
            hs4_ref[s, r * n4:(r + 1) * n4, :] = part
            hb_ref[1, r * n4:(r + 1) * n4, ls] = part.astype(BF16)
    for s in range(n_slab):
        ls = slice(s * HEAD_DIM, (s + 1) * HEAD_DIM)
        for r16 in range(d16):
            part = hs4_ref[s, pl.ds((r16 % d4) * n4 + r16 // d4, n16, stride=d4), :]
            hb_ref[2, r16 * n16:(r16 + 1) * n16, ls] = part.astype(BF16)

    def proj(g, c0, width):
        return jnp.dot(hb_ref[g], w_ref[:, c0:c0 + width], preferred_element_type=F32)

    def qkv(g, store):
        for t, col in enumerate((COL_Q, COL_K, COL_V)):
            r = proj(g, col + g * ATTN_WIDTH, ATTN_WIDTH)
            if t == 0:
                r = r * Q_SCALE
            r = r.astype(BF16)
            for j in range(HEADS_PER_GROUP):
                store(t, j, r[:, j * HEAD_DIM:(j + 1) * HEAD_DIM])

    def store1(t, j, v):
        qkv1_ref[0, t, j] = v
    qkv(0, store1)

    ga = proj(0, COL_GATTN, ATTN_WIDTH)
    sgv = (ga * _sigmoid(ga)).astype(BF16)
    for j in range(HEADS_PER_GROUP):
        sg_ref[0, j] = sgv[:, j * HEAD_DIM:(j + 1) * HEAD_DIM]

    for c0 in range(0, CONV_WIDTH, CONV_CHUNK):
        cs = slice(c0, c0 + CONV_CHUNK)
        u = proj(0, COL_U + c0, CONV_CHUNK)
        cg = proj(0, COL_C + c0, CONV_CHUNK)
        z = cg * u
        zbuf_ref[SUBLANES:SUBLANES + tm, cs] = z
        zm1 = zbuf_ref[SUBLANES - 1:SUBLANES - 1 + tm, cs]
        zm2 = zbuf_ref[SUBLANES - 2:SUBLANES - 2 + tm, cs]
        y = cw_ref[0:1, cs] * zm2 + cw_ref[1:2, cs] * zm1 + cw_ref[2:3, cs] * z + cb_ref[:, cs]
        bg = proj(0, COL_B + c0, CONV_CHUNK)
        gc = proj(0, COL_GCONV + c0, CONV_CHUNK)
        spre_ref[0, :, cs] = (bg * y * (gc * _sigmoid(gc))).astype(BF16)
    zbuf_ref[0:SUBLANES, :] = zbuf_ref[tm:tm + SUBLANES, :]

    def store_sub(ref, d):
        def store(t, j, v):
            rows = tm // d
            for r in range(d):
                ref[0, t, j, r] = v[r * rows:(r + 1) * rows]
        return store
    qkv(1, store_sub(qkv2_ref, d4))
    qkv(2, store_sub(qkv3_ref, d16))


def _inproj_call(mod, x, w_in, conv_w, conv_b):
    bsz, seq, _ = x.shape
    tm = INPROJ_TM
    n_t = seq // tm
    d4, d16 = DILATED_GROUPS[1][1], DILATED_GROUPS[2][1]
    sub4, sub16 = seq // d4, seq // d16
    assert seq % tm == 0 and tm % (d16 * 16) == 0
    hg = HEADS_PER_GROUP

    in_specs = [
        pl.BlockSpec((1, 1, 3 * D_MODEL), lambda b, i: (b, 0, 0)),
        pl.BlockSpec((1, tm, D_MODEL), lambda b, i: (b, i, 0)),
        pl.BlockSpec((D_MODEL, COL_MATTN), lambda b, i: (0, 0), pipeline_mode=pl.Buffered(1)),
        pl.BlockSpec((CONV_K, CONV_WIDTH), lambda b, i: (0, 0)),
        pl.BlockSpec((1, CONV_WIDTH), lambda b, i: (0, 0)),
    ]
    out_shape = [
        jax.ShapeDtypeStruct((bsz, 3, hg, seq, HEAD_DIM), BF16),
        jax.ShapeDtypeStruct((bsz, 3, hg, d4, sub4, HEAD_DIM), BF16),
        jax.ShapeDtypeStruct((bsz, 3, hg, d16, sub16, HEAD_DIM), BF16),
        jax.ShapeDtypeStruct((bsz, hg, seq, HEAD_DIM), BF16),
        jax.ShapeDtypeStruct((bsz, seq, CONV_WIDTH), BF16),
    ]
    out_specs = [
        pl.BlockSpec((1, 3, hg, tm, HEAD_DIM), lambda b, i: (b, 0, 0, i, 0)),
        pl.BlockSpec((1, 3, hg, d4, tm // d4, HEAD_DIM), lambda b, i: (b, 0, 0, 0, i, 0)),
        pl.BlockSpec((1, 3, hg, d16, tm // d16, HEAD_DIM), lambda b, i: (b, 0, 0, 0, i, 0)),
        pl.BlockSpec((1, hg, tm, HEAD_DIM), lambda b, i: (b, 0, i, 0)),
        pl.BlockSpec((1, tm, CONV_WIDTH), lambda b, i: (b, i, 0)),
    ]
    weights = D_MODEL * COL_MATTN * 2
    x_tiles = tm * D_MODEL * 4 * 2
    out_tiles = (3 * 3 * ATTN_WIDTH + ATTN_WIDTH + CONV_WIDTH) * tm * 2 * 2
    scratch = 2 * tm * D_MODEL * 4 + N_GROUPS * tm * D_MODEL * 2 + (tm + SUBLANES) * CONV_WIDTH * 4
    return pl.pallas_call(
        functools.partial(_inproj_kernel, tm=tm),
        grid=(bsz, n_t),
        in_specs=in_specs,
        out_specs=out_specs,
        out_shape=out_shape,
        scratch_shapes=[pltpu.VMEM((D_MODEL // HEAD_DIM, tm, HEAD_DIM), F32),
                        pltpu.VMEM((D_MODEL // HEAD_DIM, tm, HEAD_DIM), F32),
                        pltpu.VMEM((N_GROUPS, tm, D_MODEL), BF16),
                        pltpu.VMEM((tm + SUBLANES, CONV_WIDTH), F32)],
        compiler_params=pltpu.CompilerParams(
            dimension_semantics=("arbitrary", "arbitrary"),
            vmem_limit_bytes=_vmem_limit(weights + x_tiles + out_tiles + scratch)),
        name="in_proj",
    )(mod, x, w_in, conv_w, conv_b)


def _nt_dot(q, k):
    return lax.dot_general(q, k, (((1,), (1,)), ((), ())), preferred_element_type=F32)


def _softmax_pv(s, v):
    m = jnp.max(s, axis=1, keepdims=True)
    p = jnp.exp2(s - m)
    v_ext = jnp.concatenate([v, jnp.ones((v.shape[0], HEAD_DIM), BF16)], axis=1)
    acc = jnp.dot(p.astype(BF16), v_ext, preferred_element_type=F32)
    return acc[:, :HEAD_DIM], m, acc[:, HEAD_DIM:]


def _attn_kernel(qkv1_ref, qkv2_ref, qkv3_ref, bias_ref, sg_ref, o_ref, *row_refs, seq):
    res1, res2, res3 = (row_refs[3 * g:3 * g + 3] for g in range(N_GROUPS))
    tok_ref = row_refs[3 * N_GROUPS]
    step = pl.program_id(0)
    par = jnp.bitwise_and(step, 1)
    opar = 1 - par

    @pl.when(step == 0)
    def _():
        for r in res1 + res2 + res3:
            r[1] = jnp.ones((seq, HEAD_DIM), F32)

    d4, d16 = DILATED_GROUPS[1][1], DILATED_GROUPS[2][1]
    n_blk = seq // BLOCK
    blk4 = seq // d4 // BLOCK
    assert seq // d16 == BLOCK and d16 == d4 * d4 and d4 == blk4
    sub4 = seq // d4

    def plan(blk):
        f1, nb, r4 = int(blk == 0), blk % blk4, blk // blk4
        f2 = int(nb == 0)
        return dict(q=pl.ds(blk * BLOCK, BLOCK), f1=f1, f2=f2,
                    k1=pl.ds((blk - 1 + f1) * BLOCK, 2 * BLOCK), k2=pl.ds((blk - 1 + f2) * BLOCK, 2 * BLOCK),
                    out3=pl.ds(nb * sub4 + r4, BLOCK, stride=d4))

    def score_block(blk, g):
        p = plan(blk)
        if g == 0:
            return _nt_dot(qkv1_ref[0, 0, 0, p["q"], :], qkv1_ref[0, 1, 0, p["k1"], :]) + bias_ref[0, 0, p["f1"]]
        if g == 1:
            return _nt_dot(qkv2_ref[0, 0, 0, p["q"], :], qkv2_ref[0, 1, 0, p["k2"], :]) + bias_ref[1, 0, p["f2"]]
        return _nt_dot(qkv3_ref[0, 0, 0, p["q"], :], qkv3_ref[0, 1, 0, p["q"], :]) + bias_ref[2, 0, 1, :, 0:BLOCK]

    def put(dst, idx, acc, m, l):
        dst[0][par, idx, :] = acc
        dst[1][par, idx, :] = jnp.broadcast_to(m, (BLOCK, HEAD_DIM))
        dst[2][par, idx, :] = l

    def finish_block(blk, g, s):
        p = plan(blk)
        if g == 0:
            put(res1, p["q"], *_softmax_pv(s, qkv1_ref[0, 2, 0, p["k1"], :]))
        elif g == 1:
            put(res2, p["q"], *_softmax_pv(s, qkv2_ref[0, 2, 0, p["k2"], :]))
        else:
            put(res3, p["out3"], *_softmax_pv(s, qkv3_ref[0, 2, 0, p["q"], :]))

    def merge(base, n_rows):
        for r0 in range(base, base + n_rows, MERGE_ROWS):
            c, p0 = r0 // sub4, r0 % sub4
            tok = pl.ds(d4 * p0 + c, MERGE_ROWS, stride=d4)
            own = pl.ds(r0, MERGE_ROWS)
            srcs = [(res1, tok), (res2, own), (res3, own)]
            ms = [r[1][opar, idx, :] for r, idx in srcs]
            m = jnp.maximum(jnp.maximum(ms[0], ms[1]), ms[2])
            es = [jnp.exp2(mg - m) for mg in ms]
            num = sum(e * r[0][opar, idx, :] for e, (r, idx) in zip(es, srcs))
            den = sum(e * r[2][opar, idx, :] for e, (r, idx) in zip(es, srcs))
            tok_ref[tok, :] = num * (1.0 / den)

    def emit_output(base, n_rows):
        rws = pl.ds(base, n_rows)
        o_ref[0, 0, rws, :] = tok_ref[rws, :].astype(BF16) * sg_ref[0, 0, rws, :]

    blocks = [(blk, g) for blk in range(n_blk) for g in ATTN_GROUP_ORDER]
    n_b = len(blocks)
    cuts = [MERGE_ROWS * ((seq // MERGE_ROWS) * k // n_b) for k in range(n_b + 1)]
    last = (d4 - 1) * sub4
    pending = {k: score_block(*blocks[k]) for k in range(min(ATTN_LAG, n_b))}
    for k in range(n_b):
        merge(cuts[k], cuts[k + 1] - cuts[k])
        lo, hi = max(cuts[k], last), cuts[k + 1]
        if hi > lo:
            emit_output(d4 * (lo - last), d4 * (hi - lo))
        if k + ATTN_LAG < n_b:
            pending[k + ATTN_LAG] = score_block(*blocks[k + ATTN_LAG])
        finish_block(*blocks[k], pending.pop(k))


def _attn_call(qkv1, qkv2, qkv3, bias, sg):
    bsz, _, hg, seq, _ = qkv1.shape
    qkv2 = qkv2.reshape(qkv1.shape)
    qkv3 = qkv3.reshape(qkv1.shape)
    bias = bias.reshape(N_GROUPS, hg, 2, BLOCK, 2 * BLOCK)
    n_slots = bsz * hg

    def cur(s):
        c = jnp.minimum(s, n_slots - 1)
        return c // hg, c % hg

    def prev(s):
        p = jnp.maximum(s - 1, 0)
        return p // hg, p % hg

    def qkv_map(s):
        b, j = cur(s)
        return (b, 0, j, 0, 0)

    def bias_map(s):
        return (0, cur(s)[1], 0, 0, 0)

    def lag_map(s):
        b, j = prev(s)
        return (b, j, 0, 0)

    in_tiles = (3 * 3 + 1) * seq * HEAD_DIM * 2 * 2 + N_GROUPS * 2 * BLOCK * 2 * BLOCK * 4 * 2
    out_tiles = seq * HEAD_DIM * 2 * 2
    row_shapes = [(2, seq, HEAD_DIM)] * (3 * N_GROUPS) + [(seq, HEAD_DIM)]
    scratch = sum(math.prod(sh) * 4 for sh in row_shapes)
    qkv_spec = pl.BlockSpec((1, 3, 1, seq, HEAD_DIM), qkv_map)
    return pl.pallas_call(
        functools.partial(_attn_kernel, seq=seq),
        grid=(n_slots + 1,),
        in_specs=[
            qkv_spec, qkv_spec, qkv_spec,
            pl.BlockSpec((N_GROUPS, 1, 2, BLOCK, 2 * BLOCK), bias_map),
            pl.BlockSpec((1, 1, seq, HEAD_DIM), lag_map),
        ],
        out_specs=pl.BlockSpec((1, 1, seq, HEAD_DIM), lag_map),
        out_shape=jax.ShapeDtypeStruct((bsz, hg, seq, HEAD_DIM), BF16),
        scratch_shapes=[pltpu.VMEM(sh, F32) for sh in row_shapes],
        compiler_params=pltpu.CompilerParams(
            dimension_semantics=("arbitrary",),
            vmem_limit_bytes=_vmem_limit(in_tiles + out_tiles + scratch)),
        name="dilated_attn",
    )(qkv1, qkv2, qkv3, bias, sg)


def _outproj_kernel(mod_ref, o_ref, spre_ref, x_ref, wma_ref, wmc_ref, wco_ref, wao_ref, wo_ref,
                    lng_ref, lnb_ref, out_ref, *, alpha, tm):
    shift = mod_ref[0, :, 0:D_MODEL]
    scale1 = 1.0 + mod_ref[0, :, D_MODEL:2 * D_MODEL]
    gate1 = 1.0 + mod_ref[0, :, 2 * D_MODEL:3 * D_MODEL]

    def branches(rws):
        h = (x_ref[0, rws, :] * scale1 + shift).astype(BF16)
        g_attn = _sigmoid(jnp.dot(h, wma_ref[...], preferred_element_type=F32))
        g_conv = _sigmoid(jnp.dot(h, wmc_ref[...], preferred_element_type=F32))
        a_in = jnp.concatenate([o_ref[0, j, rws, :] for j in range(HEADS_PER_GROUP)], axis=1)
        a_out = jnp.dot(a_in, wao_ref[...], preferred_element_type=F32)
        s_out = jnp.dot(spre_ref[0, rws, :], wco_ref[...], preferred_element_type=F32)
        return (g_attn * a_out + g_conv * s_out).astype(BF16)

    def norm(rws, y):
        v = alpha * x_ref[0, rws, :] + gate1 * y
        mu = jnp.mean(v, axis=-1, keepdims=True)
        dv = v - mu
        var = jnp.mean(dv * dv, axis=-1, keepdims=True)
        out_ref[0, rws, :] = dv * lax.rsqrt(var + LN_EPS) * lng_ref[...] + lnb_ref[...]

    chunks = [slice(r0, r0 + OUTPROJ_ROWS) for r0 in range(0, tm, OUTPROJ_ROWS)]
    merged = branches(chunks[0])
    for c, rws in enumerate(chunks):
        nxt = branches(chunks[c + 1]) if c + 1 < len(chunks) else None
        y = jnp.dot(merged, wo_ref[...], preferred_element_type=F32)
        norm(rws, y)
        merged = nxt


def _outproj_call(mod, o, spre, x, w_in, w_conv_out, w_attn_out, w_o, ln_g, ln_b, alpha):
    bsz, seq, _ = x.shape
    tm = OUTPROJ_TM
    assert seq % tm == 0 and tm % OUTPROJ_ROWS == 0
    assert COL_MATTN % D_MODEL == 0 and COL_MCONV % D_MODEL == 0
    hg = HEADS_PER_GROUP
    tok = lambda b, i: (b, i, 0)
    const = lambda b, i: (0, 0)
    resident = dict(pipeline_mode=pl.Buffered(1))
    tiles = (ATTN_WIDTH * 2 + CONV_WIDTH * 2 + 2 * D_MODEL * 4) * tm * 2
    weights = (2 * D_MODEL + CONV_WIDTH + ATTN_WIDTH + D_MODEL) * D_MODEL * 2
    temps = 8 * OUTPROJ_ROWS * D_MODEL * 4
    return pl.pallas_call(
        functools.partial(_outproj_kernel, alpha=alpha, tm=tm),
        grid=(bsz, seq // tm),
        in_specs=[
            pl.BlockSpec((1, 1, 3 * D_MODEL), lambda b, i: (b, 0, 0)),
            pl.BlockSpec((1, hg, tm, HEAD_DIM), lambda b, i: (b, 0, i, 0)),
            pl.BlockSpec((1, tm, CONV_WIDTH), tok),
            pl.BlockSpec((1, tm, D_MODEL), tok),
            pl.BlockSpec((D_MODEL, D_MODEL), lambda b, i: (0, COL_MATTN // D_MODEL), **resident),
            pl.BlockSpec((D_MODEL, D_MODEL), lambda b, i: (0, COL_MCONV // D_MODEL), **resident),
            pl.BlockSpec((CONV_WIDTH, D_MODEL), const, **resident),
            pl.BlockSpec((ATTN_WIDTH, D_MODEL), const, **resident),
            pl.BlockSpec((D_MODEL, D_MODEL), const, **resident),
            pl.BlockSpec((1, D_MODEL), const),
            pl.BlockSpec((1, D_MODEL), const),
        ],
        out_specs=pl.BlockSpec((1, tm, D_MODEL), tok),
        out_shape=jax.ShapeDtypeStruct((bsz, seq, D_MODEL), F32),
        compiler_params=pltpu.CompilerParams(
            dimension_semantics=("arbitrary", "arbitrary"),
            vmem_limit_bytes=_vmem_limit(tiles + weights + temps)),
        name="out_proj",
    )(mod, o, spre, x, w_in, w_in, w_conv_out, w_attn_out, w_o, ln_g, ln_b)


def kernel(x, c, w_ada, b_ada, w_in, conv_w, conv_b, rel_bias, w_attn_out, w_conv_out, w_o, ln_g, ln_b):
    bsz = x.shape[0]
    depth = w_in.shape[0]
    alpha = (2.0 * depth) ** 0.25
    bias = _bias_call(rel_bias)
    for layer in range(depth):
        mod = _ada_call(c, w_ada[layer], b_ada[layer]).reshape(bsz, 1, 3 * D_MODEL)
        w_in_b = w_in[layer].astype(BF16)
        qkv1, qkv2, qkv3, sg, spre = _inproj_call(mod, x, w_in_b, conv_w[layer], conv_b[layer][None, :])
        o = _attn_call(qkv1, qkv2, qkv3, bias, sg)
        x = _outproj_call(mod, o, spre, x, w_in_b, w_conv_out[layer].astype(BF16),
                          w_attn_out[layer].astype(BF16), w_o[layer].astype(BF16),
                          ln_g[layer][None, :], ln_b[layer][None, :], alpha)
    return x
```

```python
import functools
import math

import numpy as np
import jax
import jax.numpy as jnp
from jax import lax
from jax.experimental import pallas as pl
from jax.experimental.pallas import tpu as pltpu

F32 = jnp.float32
BF16 = jnp.bfloat16

D_MODEL = 1024
HEAD_DIM = 128
HEADS_PER_GROUP = 4
DILATED_GROUPS = ((128, 1), (512, 4), (2048, 16))
N_GROUPS = len(DILATED_GROUPS)
N_ATTN_HEADS = N_GROUPS * HEADS_PER_GROUP
QKV_WIDTH = N_ATTN_HEADS * HEAD_DIM
ATTN_WIDTH = HEADS_PER_GROUP * HEAD_DIM
CONV_WIDTH = D_MODEL
CONV_K = 3
N_BUCKETS = 32
MAX_EXACT = 16
MAX_DISTANCE = 2048
BLOCK = 128
LN_EPS = 1e-5
NEG_INF = -1e30
LOG2E = math.log2(math.e)
Q_SCALE = HEAD_DIM ** -0.5 * LOG2E

COL_Q = 0
COL_K = QKV_WIDTH
COL_V = 2 * QKV_WIDTH
COL_GATTN = 3 * QKV_WIDTH
COL_U = COL_GATTN + ATTN_WIDTH
COL_B = COL_U + CONV_WIDTH
COL_C = COL_B + CONV_WIDTH
COL_GCONV = COL_C + CONV_WIDTH
COL_MATTN = COL_GCONV + CONV_WIDTH
COL_MCONV = COL_MATTN + D_MODEL
N_COLS = COL_MCONV + D_MODEL

V7X_VMEM_BYTES = 64 * 1024 * 1024
SUBLANES = 8
INPROJ_TM = 512
OUTPROJ_TM = 1024
OUTPROJ_ROWS = 256
CONV_CHUNK = 256
ATTN_LAG = 2
ATTN_GROUP_ORDER = (2, 0, 1)
MERGE_ROWS = 32


def _vmem_limit(estimate_bytes):
    return int(min(V7X_VMEM_BYTES - (4 << 20), estimate_bytes * 5 // 4 + (4 << 20)))


def _bucket_table():
    a = np.arange(BLOCK)[:, None]
    b = np.arange(2 * BLOCK)[None, :]
    steps = a + BLOCK - b
    out = []
    for window, dilation in DILATED_GROUPS:
        n_steps = window // dilation
        valid = (steps >= 0) & (steps <= n_steps)
        dist = np.maximum(steps, 0) * dilation
        n = np.maximum(dist, 1).astype(np.float32)
        large = MAX_EXACT + (np.log(n / np.float32(MAX_EXACT)) / np.float32(math.log(MAX_DISTANCE / MAX_EXACT))
                             * np.float32(N_BUCKETS - MAX_EXACT)).astype(np.int32)
        large = np.minimum(large, N_BUCKETS - 1)
        bucket = np.where(dist < MAX_EXACT, dist, large)
        out.append(np.where(valid, bucket, -1).astype(np.int32))
    return np.stack(out)


_BUCKETS = _bucket_table()
assert all((_BUCKETS[:, a, a:] == _BUCKETS[:, 0, :2 * BLOCK - a]).all() and (_BUCKETS[:, a, :a] == -1).all()
           for a in range(BLOCK))
_BUCKET_ROWS = np.broadcast_to(
    np.pad(_BUCKETS[:, 0:1, :], ((0, 0), (0, 0), (0, 2 * BLOCK)), constant_values=-1),
    (N_GROUPS, SUBLANES, 4 * BLOCK)).copy()


def _sigmoid(v):
    return 0.5 * jnp.tanh(0.5 * v) + 0.5


def _ada_kernel(c_ref, w_ref, b_ref, o_ref):
    c = c_ref[...]
    mod = jnp.dot(c * _sigmoid(c), w_ref[...], preferred_element_type=F32) + b_ref[...]
    for b in range(mod.shape[0]):
        o_ref[b] = mod[b:b + 1, :]


def _ada_call(c, w_ada, b_ada):
    bsz = c.shape[0]
    return pl.pallas_call(
        _ada_kernel,
        out_shape=jax.ShapeDtypeStruct((bsz, 1, 3 * D_MODEL), F32),
        name="ada_mod",
    )(c, w_ada, b_ada[None, :])


def _bias_kernel(rb_ref, bucket_ref, o_ref):
    for g in range(N_GROUPS):
        bk = bucket_ref[g]
        for j in range(HEADS_PER_GROUP):
            h = g * HEADS_PER_GROUP + j
            row = jnp.full(bk.shape, NEG_INF, F32)
            for k in range(N_BUCKETS):
                row = jnp.where(bk == k, rb_ref[k, h] * LOG2E, row)
            full = jnp.broadcast_to(row[0:1, :], (BLOCK, row.shape[1]))
            acc = pltpu.roll(full, 0, 1, stride=1, stride_axis=0)[:, :2 * BLOCK]
            o_ref[h, 0] = acc
            o_ref[h, 1] = jnp.concatenate(
                [acc[:, BLOCK:], jnp.full((BLOCK, BLOCK), NEG_INF, F32)], axis=1)


def _bias_call(rel_bias):
    return pl.pallas_call(
        _bias_kernel,
        out_shape=jax.ShapeDtypeStruct((N_ATTN_HEADS, 2, BLOCK, 2 * BLOCK), F32),
        in_specs=[pl.BlockSpec(memory_space=pltpu.SMEM), pl.BlockSpec(memory_space=pltpu.VMEM)],
        out_specs=pl.BlockSpec(memory_space=pltpu.VMEM),
        name="bias_tables",
    )(rel_bias, jnp.asarray(_BUCKET_ROWS))


def _inproj_kernel(mod_ref, x_ref, w_ref, cw_ref, cb_ref,
                   qkv1_ref, qkv2_ref, qkv3_ref, sg_ref, spre_ref,
                   hs_ref, hs4_ref, hb_ref, zbuf_ref, *, tm):
    i = pl.program_id(1)

    @pl.when(i == 0)
    def _():
        zbuf_ref[0:SUBLANES, :] = jnp.zeros((SUBLANES, CONV_WIDTH), F32)

    n_slab = D_MODEL // HEAD_DIM
    for s in range(n_slab):
        ls = slice(s * HEAD_DIM, (s + 1) * HEAD_DIM)
        h = x_ref[0, :, ls] * (1.0 + mod_ref[0, :, D_MODEL + s * HEAD_DIM:D_MODEL + (s + 1) * HEAD_DIM]) \
            + mod_ref[0, :, ls]
        hb_ref[0, :, ls] = h.astype(BF16)
        hs_ref[s] = h

    d4, d16 = DILATED_GROUPS[1][1], DILATED_GROUPS[2][1]
    assert d16 == d4 * d4
    n4, n16 = tm // d4, tm // d16
    for s in range(n_slab):
        ls = slice(s * HEAD_DIM, (s + 1) * HEAD_DIM)
        for r in range(d4):
            part = hs_ref[s, pl.ds(r, n4, stride=d4), :]
            hs4_ref[s, r * n4:(r + 1) * n4, :] = part
            hb_ref[1, r * n4:(r + 1) * n4, ls] = part.astype(BF16)
    for s in range(n_slab):
        ls = slice(s * HEAD_DIM, (s + 1) * HEAD_DIM)
        for r16 in range(d16):
            part = hs4_ref[s, pl.ds((r16 % d4) * n4 + r16 // d4, n16, stride=d4), :]
            hb_ref[2, r16 * n16:(r16 + 1) * n16, ls] = part.astype(BF16)

    def proj(g, c0, width):
        return jnp.dot(hb_ref[g], w_ref[:, c0:c0 + width], preferred_element_type=F32)

    def qkv(g, store):
        for t, col in enumerate((COL_Q, COL_K, COL_V)):
            r = proj(g, col + g * ATTN_WIDTH, ATTN_WIDTH)
            if t == 0:
                r = r * Q_SCALE
            r = r.astype(BF16)
            for j in range(HEADS_PER_GROUP):
                store(t, j, r[:, j * HEAD_DIM:(j + 1) * HEAD_DIM])

    def store1(t, j, v):
        qkv1_ref[0, t, j] = v
    qkv(0, store1)

    ga = proj(0, COL_GATTN, ATTN_WIDTH)
    sgv = (ga * _sigmoid(ga)).astype(BF16)
    for j in range(HEADS_PER_GROUP):
        sg_ref[0, j] = sgv[:, j * HEAD_DIM:(j + 1) * HEAD_DIM]

    for c0 in range(0, CONV_WIDTH, CONV_CHUNK):
        cs = slice(c0, c0 + CONV_CHUNK)
        u = proj(0, COL_U + c0, CONV_CHUNK)
        cg = proj(0, COL_C + c0, CONV_CHUNK)
        z = cg * u
        zbuf_ref[SUBLANES:SUBLANES + tm, cs] = z
        zm1 = zbuf_ref[SUBLANES - 1:SUBLANES - 1 + tm, cs]
        zm2 = zbuf_ref[SUBLANES - 2:SUBLANES - 2 + tm, cs]
        y = cw_ref[0:1, cs] * zm2 + cw_ref[1:2, cs] * zm1 + cw_ref[2:3, cs] * z + cb_ref[:, cs]
        bg = proj(0, COL_B + c0, CONV_CHUNK)
        gc = proj(0, COL_GCONV + c0, CONV_CHUNK)
        spre_ref[0, :, cs] = (bg * y * (gc * _sigmoid(gc))).astype(BF16)
    zbuf_ref[0:SUBLANES, :] = zbuf_ref[tm:tm + SUBLANES, :]

    def store_sub(ref, d):
        def store(t, j, v):
            rows = tm // d
            for r in range(d):
                ref[0, t, j, r] = v[r * rows:(r + 1) * rows]
        return store
    qkv(1, store_sub(qkv2_ref, d4))
    qkv(2, store_sub(qkv3_ref, d16))


def _inproj_call(mod, x, w_in, conv_w, conv_b):
    bsz, seq, _ = x.shape
    tm = INPROJ_TM
    n_t = seq // tm
    d4, d16 = DILATED_GROUPS[1][1], DILATED_GROUPS[2][1]
    sub4, sub16 = seq // d4, seq // d16
    assert seq % tm == 0 and tm % (d16 * 16) == 0
    hg = HEADS_PER_GROUP

    in_specs = [
        pl.BlockSpec((1, 1, 3 * D_MODEL), lambda b, i: (b, 0, 0)),
        pl.BlockSpec((1, tm, D_MODEL), lambda b, i: (b, i, 0)),
        pl.BlockSpec((D_MODEL, COL_MATTN), lambda b, i: (0, 0), pipeline_mode=pl.Buffered(1)),
        pl.BlockSpec((CONV_K, CONV_WIDTH), lambda b, i: (0, 0)),
        pl.BlockSpec((1, CONV_WIDTH), lambda b, i: (0, 0)),
    ]
    out_shape = [
        jax.ShapeDtypeStruct((bsz, 3, hg, seq, HEAD_DIM), BF16),
        jax.ShapeDtypeStruct((bsz, 3, hg, d4, sub4, HEAD_DIM), BF16),
        jax.ShapeDtypeStruct((bsz, 3, hg, d16, sub16, HEAD_DIM), BF16),
        jax.ShapeDtypeStruct((bsz, hg, seq, HEAD_DIM), BF16),
        jax.ShapeDtypeStruct((bsz, seq, CONV_WIDTH), BF16),
    ]
    out_specs = [
        pl.BlockSpec((1, 3, hg, tm, HEAD_DIM), lambda b, i: (b, 0, 0, i, 0)),
        pl.BlockSpec((1, 3, hg, d4, tm // d4, HEAD_DIM), lambda b, i: (b, 0, 0, 0, i, 0)),
        pl.BlockSpec((1, 3, hg, d16, tm // d16, HEAD_DIM), lambda b, i: (b, 0, 0, 0, i, 0)),
        pl.BlockSpec((1, hg, tm, HEAD_DIM), lambda b, i: (b, 0, i, 0)),
        pl.BlockSpec((1, tm, CONV_WIDTH), lambda b, i: (b, i, 0)),
    ]
    weights = D_MODEL * COL_MATTN * 2
    x_tiles = tm * D_MODEL * 4 * 2
    out_tiles = (3 * 3 * ATTN_WIDTH + ATTN_WIDTH + CONV_WIDTH) * tm * 2 * 2
    scratch = 2 * tm * D_MODEL * 4 + N_GROUPS * tm * D_MODEL * 2 + (tm + SUBLANES) * CONV_WIDTH * 4
    return pl.pallas_call(
        functools.partial(_inproj_kernel, tm=tm),
        grid=(bsz, n_t),
        in_specs=in_specs,
        out_specs=out_specs,
        out_shape=out_shape,
        scratch_shapes=[pltpu.VMEM((D_MODEL // HEAD_DIM, tm, HEAD_DIM), F32),
                        pltpu.VMEM((D_MODEL // HEAD_DIM, tm, HEAD_DIM), F32),
                        pltpu.VMEM((N_GROUPS, tm, D_MODEL), BF16),
                        pltpu.VMEM((tm + SUBLANES, CONV_WIDTH), F32)],
        compiler_params=pltpu.CompilerParams(
            dimension_semantics=("arbitrary", "arbitrary"),
            vmem_limit_bytes=_vmem_limit(weights + x_tiles + out_tiles + scratch)),
        name="in_proj",
    )(mod, x, w_in, conv_w, conv_b)


def _nt_dot(q, k):
    return lax.dot_general(q, k, (((1,), (1,)), ((), ())), preferred_element_type=F32)


def _softmax_pv(s, v):
    m = jnp.max(s, axis=1, keepdims=True)
    p = jnp.exp2(s - m)
    v_ext = jnp.concatenate([v, jnp.ones((v.shape[0], HEAD_DIM), BF16)], axis=1)
    acc = jnp.dot(p.astype(BF16), v_ext, preferred_element_type=F32)
    return acc[:, :HEAD_DIM], m, acc[:, HEAD_DIM:]


def _attn_kernel(qkv1_ref, qkv2_ref, qkv3_ref, bias_ref, sg_ref, o_ref, *row_refs, seq):
    res1, res2, res3 = (row_refs[3 * g:3 * g + 3] for g in range(N_GROUPS))
    tok_ref = row_refs[3 * N_GROUPS]
    step = pl.program_id(0)
    par = jnp.bitwise_and(step, 1)
    opar = 1 - par

    @pl.when(step == 0)
    def _():
        for r in res1 + res2 + res3:
            r[1] = jnp.ones((seq, HEAD_DIM), F32)

    d4, d16 = DILATED_GROUPS[1][1], DILATED_GROUPS[2][1]
    n_blk = seq // BLOCK
    blk4 = seq // d4 // BLOCK
    assert seq // d16 == BLOCK and d16 == d4 * d4 and d4 == blk4
    sub4 = seq // d4

    def plan(blk):
        f1, nb, r4 = int(blk == 0), blk % blk4, blk // blk4
        f2 = int(nb == 0)
        return dict(q=pl.ds(blk * BLOCK, BLOCK), f1=f1, f2=f2,
                    k1=pl.ds((blk - 1 + f1) * BLOCK, 2 * BLOCK), k2=pl.ds((blk - 1 + f2) * BLOCK, 2 * BLOCK),
                    out3=pl.ds(nb * sub4 + r4, BLOCK, stride=d4))

    def score_block(blk, g):
        p = plan(blk)
        if g == 0:
            return _nt_dot(qkv1_ref[0, 0, 0, p["q"], :], qkv1_ref[0, 1, 0, p["k1"], :]) + bias_ref[0, 0, p["f1"]]
        if g == 1:
            return _nt_dot(qkv2_ref[0, 0, 0, p["q"], :], qkv2_ref[0, 1, 0, p["k2"], :]) + bias_ref[1, 0, p["f2"]]
        return _nt_dot(qkv3_ref[0, 0, 0, p["q"], :], qkv3_ref[0, 1, 0, p["q"], :]) + bias_ref[2, 0, 1, :, 0:BLOCK]

    def put(dst, idx, acc, m, l):
        dst[0][par, idx, :] = acc
        dst[1][par, idx, :] = jnp.broadcast_to(m, (BLOCK, HEAD_DIM))
        dst[2][par, idx, :] = l

    def finish_block(blk, g, s):
        p = plan(blk)
        if g == 0:
            put(res1, p["q"], *_softmax_pv(s, qkv1_ref[0, 2, 0, p["k1"], :]))
        elif g == 1:
            put(res2, p["q"], *_softmax_pv(s, qkv2_ref[0, 2, 0, p["k2"], :]))
        else:
            put(res3, p["out3"], *_softmax_pv(s, qkv3_ref[0, 2, 0, p["q"], :]))

    def merge(base, n_rows):
        for r0 in range(base, base + n_rows, MERGE_ROWS):
            c, p0 = r0 // sub4, r0 % sub4
            tok = pl.ds(d4 * p0 + c, MERGE_ROWS, stride=d4)
            own = pl.ds(r0, MERGE_ROWS)
            srcs = [(res1, tok), (res2, own), (res3, own)]
            ms = [r[1][opar, idx, :] for r, idx in srcs]
            m = jnp.maximum(jnp.maximum(ms[0], ms[1]), ms[2])
            es = [jnp.exp2(mg - m) for mg in ms]
            num = sum(e * r[0][opar, idx, :] for e, (r, idx) in zip(es, srcs))
            den = sum(e * r[2][opar, idx, :] for e, (r, idx) in zip(es, srcs))
            tok_ref[tok, :] = num * (1.0 / den)

    def emit_output(base, n_rows):
        rws = pl.ds(base, n_rows)
        o_ref[0, 0, rws, :] = tok_ref[rws, :].astype(BF16) * sg_ref[0, 0, rws, :]

    blocks = [(blk, g) for blk in range(n_blk) for g in ATTN_GROUP_ORDER]
    n_b = len(blocks)
    cuts = [MERGE_ROWS * ((seq // MERGE_ROWS) * k // n_b) for k in range(n_b + 1)]
    last = (d4 - 1) * sub4
    pending = {k: score_block(*blocks[k]) for k in range(min(ATTN_LAG, n_b))}
    for k in range(n_b):
        merge(cuts[k], cuts[k + 1] - cuts[k])
        lo, hi = max(cuts[k], last), cuts[k + 1]
        if hi > lo:
            emit_output(d4 * (lo - last), d4 * (hi - lo))
        if k + ATTN_LAG < n_b:
            pending[k + ATTN_LAG] = score_block(*blocks[k + ATTN_LAG])
        finish_block(*blocks[k], pending.pop(k))


def _attn_call(qkv1, qkv2, qkv3, bias, sg):
    bsz, _, hg, seq, _ = qkv1.shape
    qkv2 = qkv2.reshape(qkv1.shape)
    qkv3 = qkv3.reshape(qkv1.shape)
    bias = bias.reshape(N_GROUPS, hg, 2, BLOCK, 2 * BLOCK)
    n_slots = bsz * hg

    def cur(s):
        c = jnp.minimum(s, n_slots - 1)
        return c // hg, c % hg

    def prev(s):
        p = jnp.maximum(s - 1, 0)
        return p // hg, p % hg

    def qkv_map(s):
        b, j = cur(s)
        return (b, 0, j, 0, 0)

    def bias_map(s):
        return (0, cur(s)[1], 0, 0, 0)

    def lag_map(s):
        b, j = prev(s)
        return (b, j, 0, 0)

    in_tiles = (3 * 3 + 1) * seq * HEAD_DIM * 2 * 2 + N_GROUPS * 2 * BLOCK * 2 * BLOCK * 4 * 2
    out_tiles = seq * HEAD_DIM * 2 * 2
    row_shapes = [(2, seq, HEAD_DIM)] * (3 * N_GROUPS) + [(seq, HEAD_DIM)]
    scratch = sum(math.prod(sh) * 4 for sh in row_shapes)
    qkv_spec = pl.BlockSpec((1, 3, 1, seq, HEAD_DIM), qkv_map)
    return pl.pallas_call(
        functools.partial(_attn_kernel, seq=seq),
        grid=(n_slots + 1,),
        in_specs=[
            qkv_spec, qkv_spec, qkv_spec,
            pl.BlockSpec((N_GROUPS, 1, 2, BLOCK, 2 * BLOCK), bias_map),
            pl.BlockSpec((1, 1, seq, HEAD_DIM), lag_map),
        ],
        out_specs=pl.BlockSpec((1, 1, seq, HEAD_DIM), lag_map),
        out_shape=jax.ShapeDtypeStruct((bsz, hg, seq, HEAD_DIM), BF16),
        scratch_shapes=[pltpu.VMEM(sh, F32) for sh in row_shapes],
        compiler_params=pltpu.CompilerParams(
            dimension_semantics=("arbitrary",),
            vmem_limit_bytes=_vmem_limit(in_tiles + out_tiles + scratch)),
        name="dilated_attn",
    )(qkv1, qkv2, qkv3, bias, sg)


def _outproj_kernel(mod_ref, o_ref, spre_ref, x_ref, wma_ref, wmc_ref, wco_ref, wao_ref, wo_ref,
                    lng_ref, lnb_ref, out_ref, *, alpha, tm):
    shift = mod_ref[0, :, 0:D_MODEL]
    scale1 = 1.0 + mod_ref[0, :, D_MODEL:2 * D_MODEL]
    gate1 = 1.0 + mod_ref[0, :, 2 * D_MODEL:3 * D_MODEL]

    def branches(rws):
        h = (x_ref[0, rws, :] * scale1 + shift).astype(BF16)
        g_attn = _sigmoid(jnp.dot(h, wma_ref[...], preferred_element_type=F32))
        g_conv = _sigmoid(jnp.dot(h, wmc_ref[...], preferred_element_type=F32))
        a_in = jnp.concatenate([o_ref[0, j, rws, :] for j in range(HEADS_PER_GROUP)], axis=1)
        a_out = jnp.dot(a_in, wao_ref[...], preferred_element_type=F32)
        s_out = jnp.dot(spre_ref[0, rws, :], wco_ref[...], preferred_element_type=F32)
        return (g_attn * a_out + g_conv * s_out).astype(BF16)

    def norm(rws, y):
        v = alpha * x_ref[0, rws, :] + gate1 * y
        mu = jnp.mean(v, axis=-1, keepdims=True)
        dv = v - mu
        var = jnp.mean(dv * dv, axis=-1, keepdims=True)
        out_ref[0, rws, :] = dv * lax.rsqrt(var + LN_EPS) * lng_ref[...] + lnb_ref[...]

    chunks = [slice(r0, r0 + OUTPROJ_ROWS) for r0 in range(0, tm, OUTPROJ_ROWS)]
    merged = branches(chunks[0])
    for c, rws in enumerate(chunks):
        nxt = branches(chunks[c + 1]) if c + 1 < len(chunks) else None
        y = jnp.dot(merged, wo_ref[...], preferred_element_type=F32)
        norm(rws, y)
        merged = nxt


def _outproj_call(mod, o, spre, x, w_in, w_conv_out, w_attn_out, w_o, ln_g, ln_b, alpha):
    bsz, seq, _ = x.shape
    tm = OUTPROJ_TM
    assert seq % tm == 0 and tm % OUTPROJ_ROWS == 0
    assert COL_MATTN % D_MODEL == 0 and COL_MCONV % D_MODEL == 0
    hg = HEADS_PER_GROUP
    tok = lambda b, i: (b, i, 0)
    const = lambda b, i: (0, 0)
    resident = dict(pipeline_mode=pl.Buffered(1))
    tiles = (ATTN_WIDTH * 2 + CONV_WIDTH * 2 + 2 * D_MODEL * 4) * tm * 2
    weights = (2 * D_MODEL + CONV_WIDTH + ATTN_WIDTH + D_MODEL) * D_MODEL * 2
    temps = 8 * OUTPROJ_ROWS * D_MODEL * 4
    return pl.pallas_call(
        functools.partial(_outproj_kernel, alpha=alpha, tm=tm),
        grid=(bsz, seq // tm),
        in_specs=[
            pl.BlockSpec((1, 1, 3 * D_MODEL), lambda b, i: (b, 0, 0)),
            pl.BlockSpec((1, hg, tm, HEAD_DIM), lambda b, i: (b, 0, i, 0)),
            pl.BlockSpec((1, tm, CONV_WIDTH), tok),
            pl.BlockSpec((1, tm, D_MODEL), tok),
            pl.BlockSpec((D_MODEL, D_MODEL), lambda b, i: (0, COL_MATTN // D_MODEL), **resident),
            pl.BlockSpec((D_MODEL, D_MODEL), lambda b, i: (0, COL_MCONV // D_MODEL), **resident),
            pl.BlockSpec((CONV_WIDTH, D_MODEL), const, **resident),
            pl.BlockSpec((ATTN_WIDTH, D_MODEL), const, **resident),
            pl.BlockSpec((D_MODEL, D_MODEL), const, **resident),
            pl.BlockSpec((1, D_MODEL), const),
            pl.BlockSpec((1, D_MODEL), const),
        ],
        out_specs=pl.BlockSpec((1, tm, D_MODEL), tok),
        out_shape=jax.ShapeDtypeStruct((bsz, seq, D_MODEL), F32),
        compiler_params=pltpu.CompilerParams(
            dimension_semantics=("arbitrary", "arbitrary"),
            vmem_limit_bytes=_vmem_limit(tiles + weights + temps)),
        name="out_proj",
    )(mod, o, spre, x, w_in, w_in, w_conv_out, w_attn_out, w_o, ln_g, ln_b)


def kernel(x, c, w_ada, b_ada, w_in, conv_w, conv_b, rel_bias, w_attn_out, w_conv_out, w_o, ln_g, ln_b):
    depth = w_in.shape[0]
    alpha = (2.0 * depth) ** 0.25
    bias = _bias_call(rel_bias)
    for layer in range(depth):
        mod = _ada_call(c, w_ada[layer], b_ada[layer])
        w_in_b = w_in[layer].astype(BF16)
        qkv1, qkv2, qkv3, sg, spre = _inproj_call(mod, x, w_in_b, conv_w[layer], conv_b[layer][None, :])
        o = _attn_call(qkv1, qkv2, qkv3, bias, sg)
        x = _outproj_call(mod, o, spre, x, w_in_b, w_conv_out[layer].astype(BF16),
                          w_attn_out[layer].astype(BF16), w_o[layer].astype(BF16),
                          ln_g[layer][None, :], ln_b[layer][None, :], alpha)
    return x
```
